```python
import math
import jax, jax.numpy as jnp
from jax import lax
import numpy as np

D_MODEL = 1024
BATCH = 2
SEQ = 8192
DEPTH = 2

D_MIX = D_MODEL
HGRN_WIDTH = D_MIX // 4
HGRN_HEAD_DIM = 64
HGRN_HEADS = HGRN_WIDTH // HGRN_HEAD_DIM
HGRN_CHUNK = 16
LB_FLOOR = 1e-30
S5_WIDTH = D_MIX // 4
S5_GROUP = 16
S5_GROUPS = S5_WIDTH // S5_GROUP
S5_STATE = 64
S5_DT_MIN = 1e-3
S5_DT_MAX = 1e-1
ATTN_WIDTH = D_MIX - HGRN_WIDTH - S5_WIDTH
DIFF_HEAD_DIM = 64
DIFF_V_DIM = 2 * DIFF_HEAD_DIM
DIFF_HEADS = ATTN_WIDTH // DIFF_V_DIM
ROPE_DIM = DIFF_HEAD_DIM // 4
ROPE_THETA = 500000.0
Q_BLOCK = 128
MASK_VALUE = -1e30
EPS = 1e-6
D_IN = 4 * HGRN_WIDTH + 2 * S5_WIDTH + 4 * ATTN_WIDTH

kernel_name = "hymba_style_hgrn2_s5_diffattn"


def _in_splits():
    sizes = [HGRN_WIDTH] * 4 + [S5_WIDTH] * 2 + [ATTN_WIDTH] * 4
    return [int(v) for v in np.cumsum(sizes)[:-1]]


def rmsnorm(x, w):
    xf = x.astype(jnp.float32)
    return xf * lax.rsqrt(jnp.mean(xf * xf, axis=-1, keepdims=True) + EPS) * w.astype(jnp.float32)


def hgrn2_mixer(q, f_raw, v, lb, g_norm_w):
    b, s, _ = q.shape
    n = s // HGRN_CHUNK
    q = jax.nn.silu(q)
    log_f = jnp.logaddexp(jnp.log(jnp.maximum(lb, LB_FLOOR)), jnp.log1p(-lb) + jax.nn.log_sigmoid(f_raw))
    k = -jnp.expm1(log_f)

    def to_chunks(t):
        return t.reshape(b, n, HGRN_CHUNK, HGRN_HEADS, HGRN_HEAD_DIM).transpose(0, 3, 1, 2, 4)

    qc, kc, vc, lfc = to_chunks(q), to_chunks(k), to_chunks(v), to_chunks(log_f)
    cum = jnp.cumsum(lfc, axis=3)
    causal = jnp.tril(jnp.ones((HGRN_CHUNK, HGRN_CHUNK), dtype=bool))[:, :, None]
    rel = cum[..., :, None, :] - cum[..., None, :, :]
    decay = jnp.where(causal, jnp.exp(jnp.where(causal, rel, 0.0)), 0.0)
    scores = jnp.einsum('bhntd,bhnsd,bhntsd->bhnts', qc, kc, decay)
    o_intra = jnp.einsum('bhnts,bhnsv->bhntv', scores, vc)
    last = cum[..., -1:, :]
    chunk_upd = jnp.einsum('bhnsd,bhnsv->bhndv', kc * jnp.exp(last - cum), vc)
    chunk_decay = jnp.exp(last[..., 0, :])

    def step(state, inp):
        dec, upd = inp
        return dec[..., None] * state + upd, state

    init = jnp.zeros((b, HGRN_HEADS, HGRN_HEAD_DIM, HGRN_HEAD_DIM), jnp.float32)
    _, prev = lax.scan(step, init, (jnp.moveaxis(chunk_decay, 2, 0), jnp.moveaxis(chunk_upd, 2, 0)))
    prev = jnp.moveaxis(prev, 0, 2)
    o_inter = jnp.einsum('bhntd,bhndv->bhntv', qc * jnp.exp(cum), prev)
    o = (o_intra + o_inter).transpose(0, 2, 3, 1, 4).reshape(b, s, HGRN_HEADS, HGRN_HEAD_DIM)
    o = rmsnorm(o, g_norm_w)
    return o.reshape(b, s, HGRN_WIDTH)


def s5_mixer(u, a_re, a_im, b_re, b_im, c_re, c_im, log_dt, d_skip, glu_w, glu_b):
    bsz, s, _ = u.shape
    ug = u.reshape(bsz, s, S5_GROUPS, S5_GROUP)
    dt = jnp.exp(log_dt)[:, None]
    mag = jnp.exp(dt * a_re)
    ab_re = mag * jnp.cos(dt * a_im)
    ab_im = mag * jnp.sin(dt * a_im)
    den = a_re * a_re + a_im * a_im
    num_re = ab_re - 1.0
    num_im = ab_im
    z_re = (num_re * a_re + num_im * a_im) / den
    z_im = (num_im * a_re - num_re * a_im) / den
    bb_re = z_re[..., None] * b_re - z_im[..., None] * b_im
    bb_im = z_re[..., None] * b_im + z_im[..., None] * b_re
    bu_re = jnp.einsum('gpc,bsgc->bsgp', bb_re, ug)
    bu_im = jnp.einsum('gpc,bsgc->bsgp', bb_im, ug)
    at_re = jnp.broadcast_to(ab_re, bu_re.shape)
    at_im = jnp.broadcast_to(ab_im, bu_im.shape)

    def combine(left, right):
        a1r, a1i, b1r, b1i = left
        a2r, a2i, b2r, b2i = right
        return (a1r * a2r - a1i * a2i,
                a1r * a2i + a1i * a2r,
                a2r * b1r - a2i * b1i + b2r,
                a2r * b1i + a2i * b1r + b2i)

    _, _, x_re, x_im = lax.associative_scan(combine, (at_re, at_im, bu_re, bu_im), axis=1)
    y = jnp.einsum('gcp,bsgp->bsgc', c_re, x_re) - jnp.einsum('gcp,bsgp->bsgc', c_im, x_im)
    y = y.reshape(bsz, s, S5_WIDTH) + d_skip * u
    y = jax.nn.gelu(y)
    return y * jax.nn.sigmoid(y @ glu_w + glu_b)


def partial_rope(t, cos, sin):
    half = ROPE_DIM // 2
    r1 = t[..., :half]
    r2 = t[..., half:ROPE_DIM]
    rot = jnp.concatenate([r1 * cos - r2 * sin, r2 * cos + r1 * sin], axis=-1)
    return jnp.concatenate([rot, t[..., ROPE_DIM:]], axis=-1)


def diff_attention(q, k, v, lam, lambda_init, subln_w):
    b, s, _ = q.shape
    pos = jnp.arange(s, dtype=jnp.float32)
    inv_freq = ROPE_THETA ** (-jnp.arange(0, ROPE_DIM, 2, dtype=jnp.float32) / ROPE_DIM)
    ang = pos[:, None] * inv_freq[None, :]
    cos, sin = jnp.cos(ang), jnp.sin(ang)
    q = q.reshape(b, s, DIFF_HEADS, 2, DIFF_HEAD_DIM).transpose(0, 2, 3, 1, 4)
    k = k.reshape(b, s, DIFF_HEADS, 2, DIFF_HEAD_DIM).transpose(0, 2, 3, 1, 4)
    v = v.reshape(b, s, DIFF_HEADS, DIFF_V_DIM).transpose(0, 2, 1, 3)
    q = partial_rope(q, cos, sin) * (DIFF_HEAD_DIM ** -0.5)
    k = partial_rope(k, cos, sin)
    outs = []
    for blk in range(s // Q_BLOCK):
        q0 = blk * Q_BLOCK
        kv_len = q0 + Q_BLOCK
        qb = q[:, :, :, q0:kv_len]
        sc = jnp.einsum('bhmqd,bhmkd->bhmqk', qb, k[:, :, :, :kv_len])
        mask = (q0 + jnp.arange(Q_BLOCK))[:, None] >= jnp.arange(kv_len)[None, :]
        p = jax.nn.softmax(jnp.where(mask, sc, MASK_VALUE), axis=-1)
        w = p[:, :, 0] - lam * p[:, :, 1]
        outs.append(jnp.einsum('bhqk,bhkv->bhqv', w, v[:, :, :kv_len]))
    o = jnp.concatenate(outs, axis=2)
    o = rmsnorm(o, subln_w) * (1.0 - lambda_init)
    return o.transpose(0, 2, 1, 3).reshape(b, s, ATTN_WIDTH)


def setup_inputs(seed: int = 0) -> dict:
    key = jax.random.key(seed)
    ks = jax.random.split(key, 22)
    nrm = jax.random.normal
    f32 = jnp.float32
    return {
        "x": nrm(ks[0], (BATCH, SEQ, D_MODEL), f32),
        "norm_w": 1.0 + 0.02 * nrm(ks[1], (DEPTH, D_MODEL), f32),
        "w_in": nrm(ks[2], (DEPTH, D_MODEL, D_IN), f32) * D_MODEL ** -0.5,
        "w_out": nrm(ks[3], (DEPTH, D_MIX, D_MODEL), f32) * D_MIX ** -0.5,
        "hgrn_lb_logits": nrm(ks[4], (DEPTH, HGRN_WIDTH), f32),
        "hgrn_norm_w": 1.0 + 0.02 * nrm(ks[5], (DEPTH, HGRN_HEAD_DIM), f32),
        "s5_a_re": -0.5 + 0.01 * nrm(ks[6], (DEPTH, S5_GROUPS, S5_STATE), f32),
        "s5_a_im": math.pi * jnp.arange(S5_STATE, dtype=f32) + 0.01 * nrm(ks[7], (DEPTH, S5_GROUPS, S5_STATE), f32),
        "s5_b_re": nrm(ks[8], (DEPTH, S5_GROUPS, S5_STATE, S5_GROUP), f32) * (2 * S5_GROUP) ** -0.5,
        "s5_b_im": nrm(ks[9], (DEPTH, S5_GROUPS, S5_STATE, S5_GROUP), f32) * (2 * S5_GROUP) ** -0.5,
        "s5_c_re": nrm(ks[10], (DEPTH, S5_GROUPS, S5_GROUP, S5_STATE), f32) * S5_STATE ** -0.5,
        "s5_c_im": nrm(ks[11], (DEPTH, S5_GROUPS, S5_GROUP, S5_STATE), f32) * S5_STATE ** -0.5,
        "s5_log_dt": jax.random.uniform(ks[12], (DEPTH, S5_GROUPS), f32, math.log(S5_DT_MIN), math.log(S5_DT_MAX)),
        "s5_d": nrm(ks[13], (DEPTH, S5_WIDTH), f32),
        "s5_glu_w": nrm(ks[14], (DEPTH, S5_WIDTH, S5_WIDTH), f32) * S5_WIDTH ** -0.5,
        "s5_glu_b": 0.01 * nrm(ks[15], (DEPTH, S5_WIDTH), f32),
        "diff_lq1": 0.1 * nrm(ks[16], (DEPTH, DIFF_HEAD_DIM), f32),
        "diff_lk1": 0.1 * nrm(ks[17], (DEPTH, DIFF_HEAD_DIM), f32),
        "diff_lq2": 0.1 * nrm(ks[18], (DEPTH, DIFF_HEAD_DIM), f32),
        "diff_lk2": 0.1 * nrm(ks[19], (DEPTH, DIFF_HEAD_DIM), f32),
        "diff_subln_w": 1.0 + 0.02 * nrm(ks[20], (DEPTH, DIFF_V_DIM), f32),
        "final_norm_w": 1.0 + 0.02 * nrm(ks[21], (D_MODEL,), f32),
    }


def reference(x, norm_w, w_in, w_out, hgrn_lb_logits, hgrn_norm_w, s5_a_re, s5_a_im, s5_b_re, s5_b_im,
              s5_c_re, s5_c_im, s5_log_dt, s5_d, s5_glu_w, s5_glu_b, diff_lq1, diff_lk1, diff_lq2, diff_lk2,
              diff_subln_w, final_norm_w):
    f32 = jnp.float32
    in_dtype = x.dtype
    h_res = x.astype(f32)
    lb_p = jax.nn.softmax(hgrn_lb_logits.astype(f32), axis=0)
    lb_all = jnp.cumsum(lb_p, axis=0) - lb_p[0:1]
    splits = _in_splits()
    for l in range(DEPTH):
        h = rmsnorm(h_res, norm_w[l])
        proj = h @ w_in[l].astype(f32)
        hq, hf, hi, hg, su, sg, aq, ak, av, ag = jnp.split(proj, splits, axis=-1)
        o_h = hgrn2_mixer(hq, hf, hi, lb_all[l], hgrn_norm_w[l]) * jax.nn.silu(hg)
        o_s = s5_mixer(su, s5_a_re[l].astype(f32), s5_a_im[l].astype(f32), s5_b_re[l].astype(f32),
                       s5_b_im[l].astype(f32), s5_c_re[l].astype(f32), s5_c_im[l].astype(f32),
                       s5_log_dt[l].astype(f32), s5_d[l].astype(f32), s5_glu_w[l].astype(f32),
                       s5_glu_b[l].astype(f32)) * jax.nn.silu(sg)
        lambda_init = 0.8 - 0.6 * math.exp(-0.3 * l)
        lam = (jnp.exp(jnp.sum(diff_lq1[l].astype(f32) * diff_lk1[l].astype(f32)))
               - jnp.exp(jnp.sum(diff_lq2[l].astype(f32) * diff_lk2[l].astype(f32))) + lambda_init)
        o_a = diff_attention(aq, ak, av, lam, lambda_init, diff_subln_w[l]) * jax.nn.silu(ag)
        mix = jnp.concatenate([o_h, o_s, o_a], axis=-1)
        h_res = h_res + mix @ w_out[l].astype(f32)
    return rmsnorm(h_res, final_norm_w).astype(in_dtype)
```

```python
import functools
import math

import jax
import jax.numpy as jnp
from jax import lax
from jax.experimental import pallas as pl
from jax.experimental.pallas import tpu as pltpu

F32 = jnp.float32
BF16 = jnp.bfloat16

D_MODEL = 1024
HGRN_W = 256
HGRN_HD = 64
CHUNK = 16
LB_FLOOR = 1e-30
S5_W = 256
S5_C = 16
S5_G = S5_W // S5_C
S5_P = 64
S5_L = 16
ATT_W = 512
ATT_DH = 64
ATT_DV = 128
ATT_H = ATT_W // ATT_DV
ROPE_DIM = 16
ROPE_THETA = 500000.0
MASK_VALUE = -1e30
EPS = 1e-6
D_IN = 4 * HGRN_W + 2 * S5_W + 4 * ATT_W
COL_S5 = 4 * HGRN_W
COL_AQ = COL_S5 + 2 * S5_W
COL_AK = COL_AQ + ATT_W
COL_AV = COL_AK + ATT_W
COL_AG = COL_AV + ATT_W

VMEM_LIMIT = 56 * 1024 * 1024


def _cparams(sem):
    return pltpu.CompilerParams(dimension_semantics=sem, vmem_limit_bytes=VMEM_LIMIT)


def _silu(x):
    return x * jax.nn.sigmoid(x)


def _dot(a, b):
    return jnp.dot(a, b, preferred_element_type=F32)


def _dot_split(a, b):
    ah = a.astype(BF16)
    al = (a - ah.astype(F32)).astype(BF16)
    bh = b.astype(BF16)
    bl = (b - bh.astype(F32)).astype(BF16)
    return _dot(ah, bh) + _dot(ah, bl) + _dot(al, bh)


def _rope(t, rc, ra, rb):
    return t * rc + pltpu.roll(t, 128 - ROPE_DIM // 2, axis=1) * ra + pltpu.roll(t, ROPE_DIM // 2, axis=1) * rb


def _inproj_kernel(x_ref, nw_ref, w_ref, rope_ref, h_ref, s_ref, q_ref, k_ref, v_ref, g_ref):
    x = x_ref[...]
    hn = x * lax.rsqrt(jnp.mean(x * x, axis=-1, keepdims=True) + EPS) * nw_ref[...]
    proj = _dot(hn.astype(BF16), w_ref[...])
    h_ref[...] = proj[:, :COL_S5]
    s_ref[...] = proj[:, COL_S5:COL_AQ]
    rc = rope_ref[:, 0:128]
    ra = rope_ref[:, 128:256]
    rb = rope_ref[:, 256:384]
    for h in range(ATT_H):
        lo = h * ATT_DV
        q = proj[:, COL_AQ + lo:COL_AQ + lo + ATT_DV]
        k = proj[:, COL_AK + lo:COL_AK + lo + ATT_DV]
        q_ref[:, lo:lo + ATT_DV] = (_rope(q, rc, ra, rb) * (ATT_DH ** -0.5)).astype(BF16)
        k_ref[:, lo:lo + ATT_DV] = _rope(k, rc, ra, rb).astype(BF16)
    v_ref[...] = proj[:, COL_AV:COL_AG].astype(BF16)
    g_ref[...] = proj[:, COL_AG:]


def _rope_tables(seq):
    pos = jnp.arange(seq, dtype=F32)
    inv_freq = ROPE_THETA ** (-jnp.arange(0, ROPE_DIM, 2, dtype=F32) / ROPE_DIM)
    ang = pos[:, None] * inv_freq[None, :]
    cos, sin = jnp.cos(ang), jnp.sin(ang)
    half = ROPE_DIM // 2
    zeros = jnp.zeros((seq, ATT_DH - ROPE_DIM), F32)
    zh = jnp.zeros((seq, half), F32)
    rc = jnp.concatenate([cos, cos, jnp.ones((seq, ATT_DH - ROPE_DIM), F32)], axis=1)
    ra = jnp.concatenate([-sin, zh, zeros], axis=1)
    rb = jnp.concatenate([zh, sin, zeros], axis=1)
    return jnp.concatenate([jnp.tile(t, (1, 2)) for t in (rc, ra, rb)], axis=1)


def _inproj(x2, norm_w, w_bf, rope, seq, tm):
    t = x2.shape[0]
    nblk = seq // tm
    row = lambda i: (i, 0)
    return pl.pallas_call(
        _inproj_kernel,
        grid=(t // tm,),
        in_specs=[pl.BlockSpec((tm, D_MODEL), row),
                  pl.BlockSpec((1, D_MODEL), lambda i: (0, 0)),
                  pl.BlockSpec((D_MODEL, D_IN), lambda i: (0, 0)),
                  pl.BlockSpec((tm, 384), lambda i: (i % nblk, 0))],
        out_specs=[pl.BlockSpec((tm, COL_S5), row), pl.BlockSpec((tm, 2 * S5_W), row),
                   pl.BlockSpec((tm, ATT_W), row), pl.BlockSpec((tm, ATT_W), row),
                   pl.BlockSpec((tm, ATT_W), row), pl.BlockSpec((tm, ATT_W), row)],
        out_shape=[jax.ShapeDtypeStruct((t, COL_S5), F32), jax.ShapeDtypeStruct((t, 2 * S5_W), F32),
                   jax.ShapeDtypeStruct((t, ATT_W), BF16), jax.ShapeDtypeStruct((t, ATT_W), BF16),
                   jax.ShapeDtypeStruct((t, ATT_W), BF16), jax.ShapeDtypeStruct((t, ATT_W), F32)],
        compiler_params=_cparams(("parallel",)),
        name="inproj",
    )(x2, norm_w, w_bf, rope)


def _outproj_kernel(res_ref, mh_ref, ms_ref, ma_ref, w_ref, fw_ref, o_ref, *, final):
    acc = res_ref[...]
    acc = acc + _dot(mh_ref[...].astype(BF16), w_ref[0:HGRN_W, :])
    acc = acc + _dot(ms_ref[...].astype(BF16), w_ref[HGRN_W:HGRN_W + S5_W, :])
    acc = acc + _dot(ma_ref[...].astype(BF16), w_ref[HGRN_W + S5_W:, :])
    if final:
        acc = acc * lax.rsqrt(jnp.mean(acc * acc, axis=-1, keepdims=True) + EPS) * fw_ref[...]
    o_ref[...] = acc


def _outproj(res, mh, ms, ma, w_bf, fw, final, tm):
    t = res.shape[0]
    row = lambda i: (i, 0)
    return pl.pallas_call(
        functools.partial(_outproj_kernel, final=final),
        grid=(t // tm,),
        in_specs=[pl.BlockSpec((tm, D_MODEL), row), pl.BlockSpec((tm, HGRN_W), row),
                  pl.BlockSpec((tm, S5_W), row), pl.BlockSpec((tm, ATT_W), row),
                  pl.BlockSpec((D_MODEL, D_MODEL), lambda i: (0, 0)),
                  pl.BlockSpec((1, D_MODEL), lambda i: (0, 0))],
        out_specs=pl.BlockSpec((tm, D_MODEL), row),
        out_shape=jax.ShapeDtypeStruct((t, D_MODEL), F32),
        compiler_params=_cparams(("parallel",)),
        name="outproj",
    )(res, mh, ms, ma, w_bf, fw)


def _attn_kernel(q_ref, k_ref, v_ref, g_ref, lq1_ref, lk1_ref, lq2_ref, lk2_ref, sw_ref, o_ref,
                 m_s, l_s, acc_s, *, lam_init, tb):
    i = pl.program_id(2)
    q = q_ref[...]
    lane = lax.broadcasted_iota(jnp.int32, (tb, ATT_DV), 1)
    qm = (jnp.where(lane < ATT_DH, q, jnp.zeros_like(q)), jnp.where(lane >= ATT_DH, q, jnp.zeros_like(q)))
    m_s[...] = jnp.full(m_s.shape, MASK_VALUE, F32)
    l_s[...] = jnp.zeros(l_s.shape, F32)
    acc_s[...] = jnp.zeros(acc_s.shape, F32)

    def step(j, masked):
        start = pl.multiple_of(j * tb, tb)
        kb = k_ref[pl.ds(start, tb), :]
        vb = v_ref[pl.ds(start, tb), :]
        for m in range(2):
            s = lax.dot_general(qm[m], kb, (((1,), (1,)), ((), ())), preferred_element_type=F32)
            if masked:
                r = lax.broadcasted_iota(jnp.int32, (tb, tb), 0)
                c = lax.broadcasted_iota(jnp.int32, (tb, tb), 1)
                s = jnp.where(r >= c, s, MASK_VALUE)
            m_prev = m_s[m]
            m_new = jnp.maximum(m_prev, jnp.max(s, axis=-1, keepdims=True))
            alpha = jnp.exp(m_prev - m_new)
            p = jnp.exp(s - m_new)
            l_s[m] = alpha * l_s[m] + jnp.sum(p, axis=-1, keepdims=True)
            acc_s[m] = alpha * acc_s[m] + _dot(p.astype(BF16), vb)
            m_s[m] = m_new

    def body(j, carry):
        step(j, False)
        return carry

    lax.fori_loop(0, i, body, 0)
    step(i, True)

    lam = (jnp.exp(jnp.sum(lq1_ref[...] * lk1_ref[...], axis=-1, keepdims=True))
           - jnp.exp(jnp.sum(lq2_ref[...] * lk2_ref[...], axis=-1, keepdims=True)) + lam_init)
    o = acc_s[0] / l_s[0] - lam * (acc_s[1] / l_s[1])
    o = o * lax.rsqrt(jnp.mean(o * o, axis=-1, keepdims=True) + EPS) * sw_ref[...] * (1.0 - lam_init)
    o_ref[...] = (o * _silu(g_ref[...])).astype(o_ref.dtype)


def _attention(q, k, v, g, lq1, lk1, lq2, lk2, sw, lam_init, batch, seq, tb):
    t = q.shape[0]
    nq = seq // tb
    qmap = lambda b, h, i: (b * nq + i, h)
    kvmap = lambda b, h, i: (b, h)
    cmap = lambda b, h, i: (0, 0)
    return pl.pallas_call(
        functools.partial(_attn_kernel, lam_init=lam_init, tb=tb),
        grid=(batch, ATT_H, nq),
        in_specs=[pl.BlockSpec((tb, ATT_DV), qmap), pl.BlockSpec((seq, ATT_DV), kvmap),
                  pl.BlockSpec((seq, ATT_DV), kvmap), pl.BlockSpec((tb, ATT_DV), qmap),
                  pl.BlockSpec((1, ATT_DH), cmap), pl.BlockSpec((1, ATT_DH), cmap),
                  pl.BlockSpec((1, ATT_DH), cmap), pl.BlockSpec((1, ATT_DH), cmap),
                  pl.BlockSpec((1, ATT_DV), cmap)],
        out_specs=pl.BlockSpec((tb, ATT_DV), qmap),
        out_shape=jax.ShapeDtypeStruct((t, ATT_W), F32),
        scratch_shapes=[pltpu.VMEM((2, tb, 1), F32), pltpu.VMEM((2, tb, 1), F32),
                        pltpu.VMEM((2, tb, ATT_DV), F32)],
        compiler_params=_cparams(("parallel", "parallel", "arbitrary")),
        name="diffattn",
    )(q, k, v, g, lq1, lk1, lq2, lk2, sw)


def _head_ones(dtype):
    r = lax.broadcasted_iota(jnp.int32, (HGRN_W, HGRN_W), 0) // HGRN_HD
    c = lax.broadcasted_iota(jnp.int32, (HGRN_W, HGRN_W), 1) // HGRN_HD
    return (r == c).astype(dtype)


def _hgrn_kernel(h_ref, lbl_ref, nw_ref, o_ref, st_ref, q_s, k_s, v_s, c_s, qt_s, kh_s, oo_s, *, layer, ts):
    @pl.when(pl.program_id(1) == 0)
    def _():
        st_ref[...] = jnp.zeros(st_ref.shape, F32)

    hq = h_ref[:, 0:HGRN_W]
    hf = h_ref[:, HGRN_W:2 * HGRN_W]
    hi = h_ref[:, 2 * HGRN_W:3 * HGRN_W]
    hg = h_ref[:, 3 * HGRN_W:4 * HGRN_W]

    lg = lbl_ref[...]
    e = jnp.exp(lg - jnp.max(lg, axis=0, keepdims=True))
    p = e / jnp.sum(e, axis=0, keepdims=True)
    lb = jnp.sum(p[0:layer + 1, :], axis=0, keepdims=True) - p[0:1, :]
    log_lb = jnp.log(jnp.maximum(lb, LB_FLOOR))
    log_sig = -(jnp.maximum(-hf, 0.0) + jnp.log1p(jnp.exp(-jnp.abs(hf))))
    b = jnp.log1p(-lb) + log_sig
    log_f = jnp.maximum(log_lb, b) + jnp.log1p(jnp.exp(-jnp.abs(log_lb - b)))
    k = 1.0 - jnp.exp(log_f)
    q = _silu(hq)

    r16 = lax.broadcasted_iota(jnp.int32, (ts, HGRN_W), 0) % CHUNK
    cum = log_f
    rem = jnp.where(r16 < CHUNK - 1, pltpu.roll(log_f, ts - 1, axis=0), 0.0)
    sh = 1
    while sh < CHUNK:
        cum = cum + jnp.where(r16 >= sh, pltpu.roll(cum, sh, axis=0), 0.0)
        rem = rem + jnp.where(r16 < CHUNK - sh, pltpu.roll(rem, ts - sh, axis=0), 0.0)
        sh *= 2
    q_s[...] = q
    k_s[...] = k
    v_s[...] = hi
    c_s[...] = cum
    qt_s[...] = q * jnp.exp(cum)
    kh_s[...] = k * jnp.exp(rem)

    ones_bf = _head_ones(BF16)
    ones_f = _head_ones(F32)
    trow = lax.broadcasted_iota(jnp.int32, (CHUNK, HGRN_W), 0)

    def chunk(n, carry):
        r0 = pl.multiple_of(n * CHUNK, CHUNK)
        c = c_s[pl.ds(r0, CHUNK), :]
        qc = q_s[pl.ds(r0, CHUNK), :]
        kc = k_s[pl.ds(r0, CHUNK), :]
        vc = v_s[pl.ds(r0, CHUNK), :]
        rows = []
        for s in range(CHUNK):
            msk = trow >= s
            d = jnp.where(msk, jnp.exp(jnp.where(msk, c - c[s:s + 1, :], 0.0)), 0.0)
            rows.append((qc * d * kc[s:s + 1, :]).astype(BF16))
        sc = _dot(jnp.concatenate(rows, axis=0), ones_bf)
        o = jnp.zeros((CHUNK, HGRN_W), F32)
        for s in range(CHUNK):
            o = o + sc[s * CHUNK:(s + 1) * CHUNK, :] * vc[s:s + 1, :]
        st = st_ref[...]
        qt = qt_s[pl.ds(r0, CHUNK), :]
        kh = kh_s[pl.ds(r0, CHUNK), :]
        o = o + lax.dot_general(qt.astype(BF16), st.astype(BF16), (((1,), (1,)), ((), ())),
                                preferred_element_type=F32)
        upd = lax.dot_general(vc.astype(BF16), kh.astype(BF16), (((0,), (0,)), ((), ())),
                              preferred_element_type=F32)
        dec = jnp.exp(c[CHUNK - 1:CHUNK, :])
        st_ref[...] = st * dec + upd * ones_f
        oo_s[pl.ds(r0, CHUNK), :] = o
        return carry

    lax.fori_loop(0, ts // CHUNK, chunk, 0)

    o = oo_s[...]
    ms = _dot_split(o * o, ones_f) * (1.0 / HGRN_HD)
    o_ref[...] = (o * lax.rsqrt(ms + EPS) * nw_ref[...] * _silu(hg)).astype(o_ref.dtype)


def _hgrn(hproj, lb_logits, nw_tiled, layer, batch, seq, ts):
    t = hproj.shape[0]
    ns = seq // ts
    depth = lb_logits.shape[0]
    blk = pltpu.VMEM((ts, HGRN_W), F32)
    return pl.pallas_call(
        functools.partial(_hgrn_kernel, layer=layer, ts=ts),
        grid=(batch, ns),
        in_specs=[pl.BlockSpec((ts, 4 * HGRN_W), lambda b, i: (b * ns + i, 0)),
                  pl.BlockSpec((depth, HGRN_W), lambda b, i: (0, 0)),
                  pl.BlockSpec((1, HGRN_W), lambda b, i: (0, 0))],
        out_specs=pl.BlockSpec((ts, HGRN_W), lambda b, i: (b * ns + i, 0)),
        out_shape=jax.ShapeDtypeStruct((t, HGRN_W), F32),
        scratch_shapes=[pltpu.VMEM((HGRN_W, HGRN_W), F32), blk, blk, blk, blk, blk, blk, blk],
        compiler_params=_cparams(("parallel", "arbitrary")),
        name="hgrn2",
    )(hproj, lb_logits, nw_tiled)


def _s5_param_kernel(prow_ref, pcol_ref, bt_ref, btile_ref, ctile_ref,
                     m_ref, gr_ref, gi_ref, hr_ref, hi_ref, d_ref):
    are, aim = prow_ref[0:1, :], prow_ref[1:2, :]
    dt = jnp.exp(prow_ref[2:3, :])
    lr, li = dt * are, dt * aim
    mag = jnp.exp(lr)
    abr, abi = mag * jnp.cos(li), mag * jnp.sin(li)
    den = are * are + aim * aim
    nr, ni = abr - 1.0, abi
    zr = (nr * are + ni * aim) / den
    zi = (ni * are - nr * aim) / den

    arec, aimc = pcol_ref[:, 0:1], pcol_ref[:, 1:2]
    dtc = jnp.exp(pcol_ref[:, 2:3])
    lrc, lic = dtc * arec, dtc * aimc
    cr, ci = ctile_ref[0], ctile_ref[1]
    tl = (lax.broadcasted_iota(jnp.int32, (S5_P, S5_L * S5_C), 1) // S5_C).astype(F32)

    def powers(n):
        mg = jnp.exp(n * lrc)
        return mg * jnp.cos(n * lic), mg * jnp.sin(n * lic)

    er, ei = powers(tl)
    w1 = er * cr - ei * ci
    w2 = -(ei * cr + er * ci)
    bbr = zr * bt_ref[0] - zi * bt_ref[1]
    bbi = zr * bt_ref[1] + zi * bt_ref[0]
    r0 = _dot_split(bbr, w1) + _dot_split(bbi, w2)
    lane = lax.broadcasted_iota(jnp.int32, (S5_C, S5_L * S5_C), 1)
    m_ref[0:S5_C, :] = r0.astype(m_ref.dtype)
    for s in range(1, S5_L):
        blk = jnp.where(lane >= s * S5_C, pltpu.roll(r0, s * S5_C, axis=1), 0.0)
        m_ref[s * S5_C:(s + 1) * S5_C, :] = blk.astype(m_ref.dtype)

    e = (S5_L - 1 - lax.broadcasted_iota(jnp.int32, (S5_L * S5_C, S5_P), 0) // S5_C).astype(F32)
    mg = jnp.exp(e * lr)
    fr, fi = mg * jnp.cos(e * li), mg * jnp.sin(e * li)
    btr = zr * btile_ref[0] - zi * btile_ref[1]
    bti = zr * btile_ref[1] + zi * btile_ref[0]
    gr_ref[...] = (fr * btr - fi * bti).astype(gr_ref.dtype)
    gi_ref[...] = (fr * bti + fi * btr).astype(gi_ref.dtype)

    pr, pi = powers(tl + 1.0)
    hr_ref[...] = (cr * pr - ci * pi).astype(hr_ref.dtype)
    hi_ref[...] = (-(cr * pi + ci * pr)).astype(hi_ref.dtype)

    mg = jnp.exp(float(S5_L) * lr)
    d_ref[0:1, :] = mg * jnp.cos(float(S5_L) * li)
    d_ref[1:2, :] = mg * jnp.sin(float(S5_L) * li)


def _s5_params(prow, pcol, bt, btile, ctile):
    g = prow.shape[0]
    wide = S5_L * S5_C
    m3 = lambda i: (i, 0, 0)
    m4 = lambda i: (i, 0, 0, 0)
    return pl.pallas_call(
        _s5_param_kernel,
        grid=(g,),
        in_specs=[pl.BlockSpec((None, 3, S5_P), m3), pl.BlockSpec((None, S5_P, 3), m3),
                  pl.BlockSpec((None, 2, S5_C, S5_P), m4), pl.BlockSpec((None, 2, wide, S5_P), m4),
                  pl.BlockSpec((None, 2, S5_P, wide), m4)],
        out_specs=[pl.BlockSpec((None, wide, wide), m3), pl.BlockSpec((None, wide, S5_P), m3),
                   pl.BlockSpec((None, wide, S5_P), m3), pl.BlockSpec((None, S5_P, wide), m3),
                   pl.BlockSpec((None, S5_P, wide), m3), pl.BlockSpec((None, 2, S5_P), m3)],
        out_shape=[jax.ShapeDtypeStruct((g, wide, wide), BF16), jax.ShapeDtypeStruct((g, wide, S5_P), BF16),
                   jax.ShapeDtypeStruct((g, wide, S5_P), BF16), jax.ShapeDtypeStruct((g, S5_P, wide), BF16),
                   jax.ShapeDtypeStruct((g, S5_P, wide), BF16), jax.ShapeDtypeStruct((g, 2, S5_P), F32)],
        compiler_params=_cparams(("parallel",)),
        name="s5_params",
    )(prow, pcol, bt, btile, ctile)


def _s5_scan_kernel(u_ref, m_ref, gr_ref, gi_ref, hr_ref, hi_ref, d_ref, y_ref, *, nchunk):
    u = u_ref[...]
    rows = u.shape[0]
    y = _dot(u, m_ref[...])
    xr = _dot(u, gr_ref[...])
    xi = _dot(u, gi_ref[...])
    rn = lax.broadcasted_iota(jnp.int32, (rows, S5_P), 0) % nchunk
    pr, pi = d_ref[0:1, :], d_ref[1:2, :]
    sh = 1
    while sh < nchunk:
        keep = rn >= sh
        sr = jnp.where(keep, pltpu.roll(xr, sh, axis=0), 0.0)
        si = jnp.where(keep, pltpu.roll(xi, sh, axis=0), 0.0)
        xr, xi = xr + pr * sr - pi * si, xi + pr * si + pi * sr
        pr, pi = pr * pr - pi * pi, 2.0 * pr * pi
        sh *= 2
    first = rn >= 1
    xr = jnp.where(first, pltpu.roll(xr, 1, axis=0), 0.0)
    xi = jnp.where(first, pltpu.roll(xi, 1, axis=0), 0.0)
    y = y + _dot(xr.astype(BF16), hr_ref[...]) + _dot(xi.astype(BF16), hi_ref[...])
    y_ref[...] = y


def _s5_scan(ug, m, gr, gi, hr, hi, d, nchunk):
    g, rows, wide = ug.shape
    m3 = lambda i: (i, 0, 0)
    return pl.pallas_call(
        functools.partial(_s5_scan_kernel, nchunk=nchunk),
        grid=(g,),
        in_specs=[pl.BlockSpec((None, rows, wide), m3), pl.BlockSpec((None, wide, wide), m3),
                  pl.BlockSpec((None, wide, S5_P), m3), pl.BlockSpec((None, wide, S5_P), m3),
                  pl.BlockSpec((None, S5_P, wide), m3), pl.BlockSpec((None, S5_P, wide), m3),
                  pl.BlockSpec((None, 2, S5_P), m3)],
        out_specs=pl.BlockSpec((None, rows, wide), m3),
        out_shape=jax.ShapeDtypeStruct((g, rows, wide), F32),
        compiler_params=_cparams(("parallel",)),
        name="s5_scan",
    )(ug, m, gr, gi, hr, hi, d)


def _s5_post_kernel(y_ref, s_ref, d_ref, w_ref, b_ref, o_ref):
    u = s_ref[:, 0:S5_W]
    sg = s_ref[:, S5_W:2 * S5_W]
    y = y_ref[...] + d_ref[...] * u
    cdf = 0.5 * (1.0 + jnp.tanh(math.sqrt(2.0 / math.pi) * (y + 0.044715 * (y * y * y))))
    y = y * cdf
    z = _dot(y.astype(BF16), w_ref[...]) + b_ref[...]
    o_ref[...] = (y * jax.nn.sigmoid(z) * _silu(sg)).astype(o_ref.dtype)


def _s5_post(y, sproj, d, w_bf, b, tm):
    t = y.shape[0]
    row = lambda i: (i, 0)
    cst = lambda i: (0, 0)
    return pl.pallas_call(
        _s5_post_kernel,
        grid=(t // tm,),
        in_specs=[pl.BlockSpec((tm, S5_W), row), pl.BlockSpec((tm, 2 * S5_W), row),
                  pl.BlockSpec((1, S5_W), cst), pl.BlockSpec((S5_W, S5_W), cst), pl.BlockSpec((1, S5_W), cst)],
        out_specs=pl.BlockSpec((tm, S5_W), row),
        out_shape=jax.ShapeDtypeStruct((t, S5_W), F32),
        compiler_params=_cparams(("parallel",)),
        name="s5_post",
    )(y, sproj, d, w_bf, b)


def _s5(sproj, a_re, a_im, b_re, b_im, c_re, c_im, log_dt, d_skip, glu_w, glu_b, batch, seq, tm):
    t = sproj.shape[0]
    nchunk = seq // S5_L
    ldt = jnp.broadcast_to(log_dt[:, None], a_re.shape)
    prow = jnp.stack([a_re, a_im, ldt], axis=1)
    pcol = jnp.stack([a_re, a_im, ldt], axis=2)
    bt = jnp.stack([b_re, b_im], axis=1).transpose(0, 1, 3, 2)
    btile = jnp.tile(bt, (1, 1, S5_L, 1))
    ctile = jnp.tile(jnp.stack([c_re, c_im], axis=1).transpose(0, 1, 3, 2), (1, 1, 1, S5_L))
    m, gr, gi, hr, hi, d = _s5_params(prow, pcol, bt, btile, ctile)
    u = sproj[:, 0:S5_W].astype(BF16)
    ug = u.reshape(batch * nchunk, S5_L, S5_G, S5_C).transpose(2, 0, 1, 3).reshape(S5_G, batch * nchunk, S5_L * S5_C)
    yg = _s5_scan(ug, m, gr, gi, hr, hi, d, nchunk)
    y = yg.reshape(S5_G, batch * nchunk, S5_L, S5_C).transpose(1, 2, 0, 3).reshape(t, S5_W)
    return _s5_post(y, sproj, d_skip[None, :], glu_w.astype(BF16), glu_b[None, :], tm)


def _pick(n, pref):
    b = min(n, pref)
    while n % b:
        b //= 2
    return b


def kernel(x, norm_w, w_in, w_out, hgrn_lb_logits, hgrn_norm_w, s5_a_re, s5_a_im, s5_b_re, s5_b_im, s5_c_re, s5_c_im, s5_log_dt, s5_d, s5_glu_w, s5_glu_b, diff_lq1, diff_lk1, diff_lq2, diff_lk2, diff_subln_w, final_norm_w):
    batch, seq, _ = x.shape
    depth = norm_w.shape[0]
    t = batch * seq
    tm = _pick(seq, 512)
    tb = _pick(seq, 512)
    ts = _pick(seq, 512)
    rope = _rope_tables(seq)
    h_res = x.astype(F32).reshape(t, D_MODEL)
    for l in range(depth):
        hp, sp, aq, ak, av, ag = _inproj(h_res, norm_w[l][None, :].astype(F32), w_in[l].astype(BF16), rope, seq, tm)
        mix_h = _hgrn(hp, hgrn_lb_logits.astype(F32), jnp.tile(hgrn_norm_w[l].astype(F32), HGRN_W // HGRN_HD)[None, :],
                      l, batch, seq, ts)
        mix_s = _s5(sp, s5_a_re[l], s5_a_im[l], s5_b_re[l], s5_b_im[l], s5_c_re[l], s5_c_im[l], s5_log_dt[l],
                    s5_d[l], s5_glu_w[l], s5_glu_b[l], batch, seq, tm)
        lam_init = 0.8 - 0.6 * math.exp(-0.3 * l)
        mix_a = _attention(aq, ak, av, ag, diff_lq1[l][None, :], diff_lk1[l][None, :], diff_lq2[l][None, :],
                           diff_lk2[l][None, :], diff_subln_w[l][None, :], lam_init, batch, seq, tb)
        h_res = _outproj(h_res, mix_h, mix_s, mix_a, w_out[l].astype(BF16), final_norm_w[None, :].astype(F32),
                         l == depth - 1, tm)
    return h_res.reshape(batch, seq, D_MODEL).astype(x.dtype)
```

```python
import functools
import math

import jax
import jax.numpy as jnp
from jax import lax
from jax.experimental import pallas as pl
from jax.experimental.pallas import tpu as pltpu

F32 = jnp.float32
BF16 = jnp.bfloat16

D_MODEL = 1024
HGRN_W = 256
HGRN_HD = 64
CHUNK = 16
LB_FLOOR = 1e-30
S5_W = 256
S5_C = 16
S5_G = S5_W // S5_C
S5_P = 64
S5_L = 16
ATT_W = 512
ATT_DH = 64
ATT_DV = 128
ATT_H = ATT_W // ATT_DV
ROPE_DIM = 16
ROPE_THETA = 500000.0
MASK_VALUE = -1e30
Q_SCALE = ATT_DH ** -0.5 * math.log2(math.e)
EPS = 1e-6
D_IN = 4 * HGRN_W + 2 * S5_W + 4 * ATT_W
COL_S5 = 4 * HGRN_W
COL_AQ = COL_S5 + 2 * S5_W
COL_AK = COL_AQ + ATT_W
COL_AV = COL_AK + ATT_W
COL_AG = COL_AV + ATT_W

VMEM_LIMIT = 56 * 1024 * 1024


def _cparams(sem):
    return pltpu.CompilerParams(dimension_semantics=sem, vmem_limit_bytes=VMEM_LIMIT)


def _silu(x):
    return x * jax.nn.sigmoid(x)


def _dot(a, b):
    return jnp.dot(a, b, preferred_element_type=F32)


def _dot_split(a, b):
    ah = a.astype(BF16)
    al = (a - ah.astype(F32)).astype(BF16)
    bh = b.astype(BF16)
    bl = (b - bh.astype(F32)).astype(BF16)
    return _dot(ah, bh) + _dot(ah, bl) + _dot(al, bh)


def _rope(t, rc, ra, rb):
    return t * rc + pltpu.roll(t, 128 - ROPE_DIM // 2, axis=1) * ra + pltpu.roll(t, ROPE_DIM // 2, axis=1) * rb


def _inproj_kernel(x_ref, nw_ref, w_ref, rope_ref, h_ref, s_ref, q_ref, k_ref, v_ref, g_ref):
    x = x_ref[...]
    hn = x * lax.rsqrt(jnp.mean(x * x, axis=-1, keepdims=True) + EPS) * nw_ref[...]
    proj = _dot(hn.astype(BF16), w_ref[...])
    h_ref[...] = proj[:, :COL_S5]
    s_ref[...] = proj[:, COL_S5:COL_AQ]
    rc = rope_ref[:, 0:128]
    ra = rope_ref[:, 128:256]
    rb = rope_ref[:, 256:384]
    for h in range(ATT_H):
        lo = h * ATT_DV
        q = proj[:, COL_AQ + lo:COL_AQ + lo + ATT_DV]
        k = proj[:, COL_AK + lo:COL_AK + lo + ATT_DV]
        q_ref[:, lo:lo + ATT_DV] = (_rope(q, rc, ra, rb) * Q_SCALE).astype(BF16)
        k_ref[:, lo:lo + ATT_DV] = _rope(k, rc, ra, rb).astype(BF16)
    v_ref[...] = proj[:, COL_AV:COL_AG].astype(BF16)
    g_ref[...] = proj[:, COL_AG:]


def _rope_tables(seq):
    pos = jnp.arange(seq, dtype=F32)
    inv_freq = ROPE_THETA ** (-jnp.arange(0, ROPE_DIM, 2, dtype=F32) / ROPE_DIM)
    ang = pos[:, None] * inv_freq[None, :]
    cos, sin = jnp.cos(ang), jnp.sin(ang)
    half = ROPE_DIM // 2
    zeros = jnp.zeros((seq, ATT_DH - ROPE_DIM), F32)
    zh = jnp.zeros((seq, half), F32)
    rc = jnp.concatenate([cos, cos, jnp.ones((seq, ATT_DH - ROPE_DIM), F32)], axis=1)
    ra = jnp.concatenate([-sin, zh, zeros], axis=1)
    rb = jnp.concatenate([zh, sin, zeros], axis=1)
    return jnp.concatenate([jnp.tile(t, (1, 2)) for t in (rc, ra, rb)], axis=1)


def _inproj(x2, norm_w, w_bf, rope, seq, tm):
    t = x2.shape[0]
    nblk = seq // tm
    row = lambda i: (i, 0)
    return pl.pallas_call(
        _inproj_kernel,
        grid=(t // tm,),
        in_specs=[pl.BlockSpec((tm, D_MODEL), row),
                  pl.BlockSpec((1, D_MODEL), lambda i: (0, 0)),
                  pl.BlockSpec((D_MODEL, D_IN), lambda i: (0, 0)),
                  pl.BlockSpec((tm, 384), lambda i: (i % nblk, 0))],
        out_specs=[pl.BlockSpec((tm, COL_S5), row), pl.BlockSpec((tm, 2 * S5_W), row),
                   pl.BlockSpec((tm, ATT_W), row), pl.BlockSpec((tm, ATT_W), row),
                   pl.BlockSpec((tm, ATT_W), row), pl.BlockSpec((tm, ATT_W), row)],
        out_shape=[jax.ShapeDtypeStruct((t, COL_S5), F32), jax.ShapeDtypeStruct((t, 2 * S5_W), F32),
                   jax.ShapeDtypeStruct((t, ATT_W), BF16), jax.ShapeDtypeStruct((t, ATT_W), BF16),
                   jax.ShapeDtypeStruct((t, ATT_W), BF16), jax.ShapeDtypeStruct((t, ATT_W), F32)],
        compiler_params=_cparams(("parallel",)),
        name="inproj",
    )(x2, norm_w, w_bf, rope)


def _outproj_kernel(res_ref, mh_ref, ms_ref, ma_ref, w_ref, fw_ref, o_ref, *, final):
    acc = res_ref[...]
    acc = acc + _dot(mh_ref[...].astype(BF16), w_ref[0:HGRN_W, :])
    acc = acc + _dot(ms_ref[...].astype(BF16), w_ref[HGRN_W:HGRN_W + S5_W, :])
    acc = acc + _dot(ma_ref[...].astype(BF16), w_ref[HGRN_W + S5_W:, :])
    if final:
        acc = acc * lax.rsqrt(jnp.mean(acc * acc, axis=-1, keepdims=True) + EPS) * fw_ref[...]
    o_ref[...] = acc


def _outproj(res, mh, ms, ma, w_bf, fw, final, tm):
    t = res.shape[0]
    row = lambda i: (i, 0)
    return pl.pallas_call(
        functools.partial(_outproj_kernel, final=final),
        grid=(t // tm,),
        in_specs=[pl.BlockSpec((tm, D_MODEL), row), pl.BlockSpec((tm, HGRN_W), row),
                  pl.BlockSpec((tm, S5_W), row), pl.BlockSpec((tm, ATT_W), row),
                  pl.BlockSpec((D_MODEL, D_MODEL), lambda i: (0, 0)),
                  pl.BlockSpec((1, D_MODEL), lambda i: (0, 0))],
        out_specs=pl.BlockSpec((tm, D_MODEL), row),
        out_shape=jax.ShapeDtypeStruct((t, D_MODEL), F32),
        compiler_params=_cparams(("parallel",)),
        name="outproj",
    )(res, mh, ms, ma, w_bf, fw)


def _attn_kernel(q_ref, k_ref, v_ref, g_ref, lq1_ref, lk1_ref, lq2_ref, lk2_ref, sw_ref, o_ref,
                 vt_s, acc_s, *, lam_init, tb):
    i = pl.program_id(2)
    nblk = vt_s.shape[0]

    @pl.when(i == 0)
    def _():
        for j in range(nblk):
            vt_s[j] = v_ref[j * tb:(j + 1) * tb, :].T

    q = q_ref[...]
    lane = lax.broadcasted_iota(jnp.int32, (tb, ATT_DV), 1)
    qm = (jnp.where(lane < ATT_DH, q, jnp.zeros_like(q)), jnp.where(lane >= ATT_DH, q, jnp.zeros_like(q)))
    acc_s[...] = jnp.zeros(acc_s.shape, F32)

    def step(j, stats, masked):
        kb = k_ref[pl.ds(pl.multiple_of(j * tb, tb), tb), :]
        vtb = vt_s[j]
        out = []
        for m in range(2):
            m_prev, l_prev = stats[2 * m], stats[2 * m + 1]
            s = lax.dot_general(kb, qm[m], (((1,), (1,)), ((), ())), preferred_element_type=F32)
            if masked:
                r = lax.broadcasted_iota(jnp.int32, (tb, tb), 0)
                c = lax.broadcasted_iota(jnp.int32, (tb, tb), 1)
                s = jnp.where(r <= c, s, MASK_VALUE)
            m_new = jnp.maximum(m_prev, jnp.max(s, axis=0, keepdims=True))
            alpha = jnp.exp2(m_prev - m_new)
            p = jnp.exp2(s - m_new)
            out += [m_new, alpha * l_prev + jnp.sum(p, axis=0, keepdims=True)]
            acc_s[m] = alpha * acc_s[m] + _dot(vtb, p.astype(BF16))
        return tuple(out)

    neg = jnp.full((1, tb), MASK_VALUE, F32)
    zero = jnp.zeros((1, tb), F32)
    stats = lax.fori_loop(0, i, lambda j, st: step(j, st, False), (neg, zero, neg, zero))
    _, l0, _, l1 = step(i, stats, True)

    lam = (jnp.exp(jnp.sum(lq1_ref[...] * lk1_ref[...], axis=-1, keepdims=True))
           - jnp.exp(jnp.sum(lq2_ref[...] * lk2_ref[...], axis=-1, keepdims=True)) + lam_init)
    ot = acc_s[0] / l0 - lam * (acc_s[1] / l1)
    o = ot.T
    o = o * lax.rsqrt(jnp.mean(o * o, axis=-1, keepdims=True) + EPS) * sw_ref[...] * (1.0 - lam_init)
    o_ref[...] = (o * _silu(g_ref[...])).astype(o_ref.dtype)


def _attention(q, k, v, g, lq1, lk1, lq2, lk2, sw, lam_init, batch, seq, tb):
    t = q.shape[0]
    nq = seq // tb
    qmap = lambda b, h, i: (b * nq + i, h)
    kvmap = lambda b, h, i: (b, h)
    cmap = lambda b, h, i: (0, 0)
    return pl.pallas_call(
        functools.partial(_attn_kernel, lam_init=lam_init, tb=tb),
        grid=(batch, ATT_H, nq),
        in_specs=[pl.BlockSpec((tb, ATT_DV), qmap), pl.BlockSpec((seq, ATT_DV), kvmap),
                  pl.BlockSpec((seq, ATT_DV), kvmap), pl.BlockSpec((tb, ATT_DV), qmap),
                  pl.BlockSpec((1, ATT_DH), cmap), pl.BlockSpec((1, ATT_DH), cmap),
                  pl.BlockSpec((1, ATT_DH), cmap), pl.BlockSpec((1, ATT_DH), cmap),
                  pl.BlockSpec((1, ATT_DV), cmap)],
        out_specs=pl.BlockSpec((tb, ATT_DV), qmap),
        out_shape=jax.ShapeDtypeStruct((t, ATT_W), F32),
        scratch_shapes=[pltpu.VMEM((nq, ATT_DV, tb), BF16), pltpu.VMEM((2, ATT_DV, tb), F32)],
        compiler_params=_cparams(("parallel", "parallel", "arbitrary")),
        name="diffattn",
    )(q, k, v, g, lq1, lk1, lq2, lk2, sw)


def _head_ones(dtype):
    r = lax.broadcasted_iota(jnp.int32, (HGRN_W, HGRN_W), 0) // HGRN_HD
    c = lax.broadcasted_iota(jnp.int32, (HGRN_W, HGRN_W), 1) // HGRN_HD
    return (r == c).astype(dtype)


def _hgrn_kernel(h_ref, lbl_ref, nw_ref, o_ref, st_ref, q_s, k_s, v_s, c_s, qt_s, kh_s, oo_s, *, layer, ts):
    @pl.when(pl.program_id(1) == 0)
    def _():
        st_ref[...] = jnp.zeros(st_ref.shape, F32)

    hq = h_ref[:, 0:HGRN_W]
    hf = h_ref[:, HGRN_W:2 * HGRN_W]
    hi = h_ref[:, 2 * HGRN_W:3 * HGRN_W]
    hg = h_ref[:, 3 * HGRN_W:4 * HGRN_W]

    lg = lbl_ref[...]
    e = jnp.exp(lg - jnp.max(lg, axis=0, keepdims=True))
    p = e / jnp.sum(e, axis=0, keepdims=True)
    lb = jnp.sum(p[0:layer + 1, :], axis=0, keepdims=True) - p[0:1, :]
    log_lb = jnp.log(jnp.maximum(lb, LB_FLOOR))
    log_sig = -(jnp.maximum(-hf, 0.0) + jnp.log1p(jnp.exp(-jnp.abs(hf))))
    b = jnp.log1p(-lb) + log_sig
    log_f = jnp.maximum(log_lb, b) + jnp.log1p(jnp.exp(-jnp.abs(log_lb - b)))
    k = 1.0 - jnp.exp(log_f)
    q = _silu(hq)

    r16 = lax.broadcasted_iota(jnp.int32, (ts, HGRN_W), 0) % CHUNK
    cum = log_f
    rem = jnp.where(r16 < CHUNK - 1, pltpu.roll(log_f, ts - 1, axis=0), 0.0)
    sh = 1
    while sh < CHUNK:
        cum = cum + jnp.where(r16 >= sh, pltpu.roll(cum, sh, axis=0), 0.0)
        rem = rem + jnp.where(r16 < CHUNK - sh, pltpu.roll(rem, ts - sh, axis=0), 0.0)
        sh *= 2
    q_s[...] = q
    k_s[...] = k
    v_s[...] = hi
    c_s[...] = cum
    qt_s[...] = q * jnp.exp(cum)
    kh_s[...] = k * jnp.exp(rem)

    ones_bf = _head_ones(BF16)
    ones_f = _head_ones(F32)
    trow = lax.broadcasted_iota(jnp.int32, (CHUNK, HGRN_W), 0)

    def chunk(n, carry):
        r0 = pl.multiple_of(n * CHUNK, CHUNK)
        c = c_s[pl.ds(r0, CHUNK), :]
        qc = q_s[pl.ds(r0, CHUNK), :]
        kc = k_s[pl.ds(r0, CHUNK), :]
        vc = v_s[pl.ds(r0, CHUNK), :]
        rows = []
        for s in range(CHUNK):
            msk = trow >= s
            d = jnp.where(msk, jnp.exp(jnp.where(msk, c - c[s:s + 1, :], 0.0)), 0.0)
            rows.append((qc * d * kc[s:s + 1, :]).astype(BF16))
        sc = _dot(jnp.concatenate(rows, axis=0), ones_bf)
        o = jnp.zeros((CHUNK, HGRN_W), F32)
        for s in range(CHUNK):
            o = o + sc[s * CHUNK:(s + 1) * CHUNK, :] * vc[s:s + 1, :]
        st = st_ref[...]
        qt = qt_s[pl.ds(r0, CHUNK), :]
        kh = kh_s[pl.ds(r0, CHUNK), :]
        o = o + lax.dot_general(qt.astype(BF16), st.astype(BF16), (((1,), (1,)), ((), ())),
                                preferred_element_type=F32)
        upd = lax.dot_general(vc.astype(BF16), kh.astype(BF16), (((0,), (0,)), ((), ())),
                              preferred_element_type=F32)
        dec = jnp.exp(c[CHUNK - 1:CHUNK, :])
        st_ref[...] = st * dec + upd * ones_f
        oo_s[pl.ds(r0, CHUNK), :] = o
        return carry

    lax.fori_loop(0, ts // CHUNK, chunk, 0)

    o = oo_s[...]
    ms = _dot_split(o * o, ones_f) * (1.0 / HGRN_HD)
    o_ref[...] = (o * lax.rsqrt(ms + EPS) * nw_ref[...] * _silu(hg)).astype(o_ref.dtype)


def _hgrn(hproj, lb_logits, nw_tiled, layer, batch, seq, ts):
    t = hproj.shape[0]
    ns = seq // ts
    depth = lb_logits.shape[0]
    blk = pltpu.VMEM((ts, HGRN_W), F32)
    return pl.pallas_call(
        functools.partial(_hgrn_kernel, layer=layer, ts=ts),
        grid=(batch, ns),
        in_specs=[pl.BlockSpec((ts, 4 * HGRN_W), lambda b, i: (b * ns + i, 0)),
                  pl.BlockSpec((depth, HGRN_W), lambda b, i: (0, 0)),
                  pl.BlockSpec((1, HGRN_W), lambda b, i: (0, 0))],
        out_specs=pl.BlockSpec((ts, HGRN_W), lambda b, i: (b * ns + i, 0)),
        out_shape=jax.ShapeDtypeStruct((t, HGRN_W), F32),
        scratch_shapes=[pltpu.VMEM((HGRN_W, HGRN_W), F32), blk, blk, blk, blk, blk, blk, blk],
        compiler_params=_cparams(("parallel", "arbitrary")),
        name="hgrn2",
    )(hproj, lb_logits, nw_tiled)


def _s5_param_kernel(prow_ref, pcol_ref, bt_ref, btile_ref, ctile_ref,
                     m_ref, gr_ref, gi_ref, hr_ref, hi_ref, d_ref):
    are, aim = prow_ref[0:1, :], prow_ref[1:2, :]
    dt = jnp.exp(prow_ref[2:3, :])
    lr, li = dt * are, dt * aim
    mag = jnp.exp(lr)
    abr, abi = mag * jnp.cos(li), mag * jnp.sin(li)
    den = are * are + aim * aim
    nr, ni = abr - 1.0, abi
    zr = (nr * are + ni * aim) / den
    zi = (ni * are - nr * aim) / den

    arec, aimc = pcol_ref[:, 0:1], pcol_ref[:, 1:2]
    dtc = jnp.exp(pcol_ref[:, 2:3])
    lrc, lic = dtc * arec, dtc * aimc
    cr, ci = ctile_ref[0], ctile_ref[1]
    tl = (lax.broadcasted_iota(jnp.int32, (S5_P, S5_L * S5_C), 1) // S5_C).astype(F32)

    def powers(n):
        mg = jnp.exp(n * lrc)
        return mg * jnp.cos(n * lic), mg * jnp.sin(n * lic)

    er, ei = powers(tl)
    w1 = er * cr - ei * ci
    w2 = -(ei * cr + er * ci)
    bbr = zr * bt_ref[0] - zi * bt_ref[1]
    bbi = zr * bt_ref[1] + zi * bt_ref[0]
    r0 = _dot_split(bbr, w1) + _dot_split(bbi, w2)
    lane = lax.broadcasted_iota(jnp.int32, (S5_C, S5_L * S5_C), 1)
    m_ref[0:S5_C, :] = r0.astype(m_ref.dtype)
    for s in range(1, S5_L):
        blk = jnp.where(lane >= s * S5_C, pltpu.roll(r0, s * S5_C, axis=1), 0.0)
        m_ref[s * S5_C:(s + 1) * S5_C, :] = blk.astype(m_ref.dtype)

    e = (S5_L - 1 - lax.broadcasted_iota(jnp.int32, (S5_L * S5_C, S5_P), 0) // S5_C).astype(F32)
    mg = jnp.exp(e * lr)
    fr, fi = mg * jnp.cos(e * li), mg * jnp.sin(e * li)
    btr = zr * btile_ref[0] - zi * btile_ref[1]
    bti = zr * btile_ref[1] + zi * btile_ref[0]
    gr_ref[...] = (fr * btr - fi * bti).astype(gr_ref.dtype)
    gi_ref[...] = (fr * bti + fi * btr).astype(gi_ref.dtype)

    pr, pi = powers(tl + 1.0)
    hr_ref[...] = (cr * pr - ci * pi).astype(hr_ref.dtype)
    hi_ref[...] = (-(cr * pi + ci * pr)).astype(hi_ref.dtype)

    mg = jnp.exp(float(S5_L) * lr)
    d_ref[0:1, :] = mg * jnp.cos(float(S5_L) * li)
    d_ref[1:2, :] = mg * jnp.sin(float(S5_L) * li)


def _s5_params(prow, pcol, bt, btile, ctile):
    g = prow.shape[0]
    wide = S5_L * S5_C
    m3 = lambda i: (i, 0, 0)
    m4 = lambda i: (i, 0, 0, 0)
    return pl.pallas_call(
        _s5_param_kernel,
        grid=(g,),
        in_specs=[pl.BlockSpec((None, 3, S5_P), m3), pl.BlockSpec((None, S5_P, 3), m3),
                  pl.BlockSpec((None, 2, S5_C, S5_P), m4), pl.BlockSpec((None, 2, wide, S5_P), m4),
                  pl.BlockSpec((None, 2, S5_P, wide), m4)],
        out_specs=[pl.BlockSpec((None, wide, wide), m3), pl.BlockSpec((None, wide, S5_P), m3),
                   pl.BlockSpec((None, wide, S5_P), m3), pl.BlockSpec((None, S5_P, wide), m3),
                   pl.BlockSpec((None, S5_P, wide), m3), pl.BlockSpec((None, 2, S5_P), m3)],
        out_shape=[jax.ShapeDtypeStruct((g, wide, wide), BF16), jax.ShapeDtypeStruct((g, wide, S5_P), BF16),
                   jax.ShapeDtypeStruct((g, wide, S5_P), BF16), jax.ShapeDtypeStruct((g, S5_P, wide), BF16),
                   jax.ShapeDtypeStruct((g, S5_P, wide), BF16), jax.ShapeDtypeStruct((g, 2, S5_P), F32)],
        compiler_params=_cparams(("parallel",)),
        name="s5_params",
    )(prow, pcol, bt, btile, ctile)


def _s5_scan_kernel(u_ref, m_ref, gr_ref, gi_ref, hr_ref, hi_ref, d_ref, y_ref, *, nchunk):
    u = u_ref[...]
    rows = u.shape[0]
    y = _dot(u, m_ref[...])
    xr = _dot(u, gr_ref[...])
    xi = _dot(u, gi_ref[...])
    rn = lax.broadcasted_iota(jnp.int32, (rows, S5_P), 0) % nchunk
    pr, pi = d_ref[0:1, :], d_ref[1:2, :]
    sh = 1
    while sh < nchunk:
        keep = rn >= sh
        sr = jnp.where(keep, pltpu.roll(xr, sh, axis=0), 0.0)
        si = jnp.where(keep, pltpu.roll(xi, sh, axis=0), 0.0)
        xr, xi = xr + pr * sr - pi * si, xi + pr * si + pi * sr
        pr, pi = pr * pr - pi * pi, 2.0 * pr * pi
        sh *= 2
    first = rn >= 1
    xr = jnp.where(first, pltpu.roll(xr, 1, axis=0), 0.0)
    xi = jnp.where(first, pltpu.roll(xi, 1, axis=0), 0.0)
    y = y + _dot(xr.astype(BF16), hr_ref[...]) + _dot(xi.astype(BF16), hi_ref[...])
    y_ref[...] = y


def _s5_scan(ug, m, gr, gi, hr, hi, d, nchunk):
    g, rows, wide = ug.shape
    m3 = lambda i: (i, 0, 0)
    return pl.pallas_call(
        functools.partial(_s5_scan_kernel, nchunk=nchunk),
        grid=(g,),
        in_specs=[pl.BlockSpec((None, rows, wide), m3), pl.BlockSpec((None, wide, wide), m3),
                  pl.BlockSpec((None, wide, S5_P), m3), pl.BlockSpec((None, wide, S5_P), m3),
                  pl.BlockSpec((None, S5_P, wide), m3), pl.BlockSpec((None, S5_P, wide), m3),
                  pl.BlockSpec((None, 2, S5_P), m3)],
        out_specs=pl.BlockSpec((None, rows, wide), m3),
        out_shape=jax.ShapeDtypeStruct((g, rows, wide), F32),
        compiler_params=_cparams(("parallel",)),
        name="s5_scan",
    )(ug, m, gr, gi, hr, hi, d)


def _s5_post_kernel(y_ref, s_ref, d_ref, w_ref, b_ref, o_ref):
    u = s_ref[:, 0:S5_W]
    sg = s_ref[:, S5_W:2 * S5_W]
    y = y_ref[...] + d_ref[...] * u
    cdf = 0.5 * (1.0 + jnp.tanh(math.sqrt(2.0 / math.pi) * (y + 0.044715 * (y * y * y))))
    y = y * cdf
    z = _dot(y.astype(BF16), w_ref[...]) + b_ref[...]
    o_ref[...] = (y * jax.nn.sigmoid(z) * _silu(sg)).astype(o_ref.dtype)


def _s5_post(y, sproj, d, w_bf, b, tm):
    t = y.shape[0]
    row = lambda i: (i, 0)
    cst = lambda i: (0, 0)
    return pl.pallas_call(
        _s5_post_kernel,
        grid=(t // tm,),
        in_specs=[pl.BlockSpec((tm, S5_W), row), pl.BlockSpec((tm, 2 * S5_W), row),
                  pl.BlockSpec((1, S5_W), cst), pl.BlockSpec((S5_W, S5_W), cst), pl.BlockSpec((1, S5_W), cst)],
        out_specs=pl.BlockSpec((tm, S5_W), row),
        out_shape=jax.ShapeDtypeStruct((t, S5_W), F32),
        compiler_params=_cparams(("parallel",)),
        name="s5_post",
    )(y, sproj, d, w_bf, b)


def _s5(sproj, a_re, a_im, b_re, b_im, c_re, c_im, log_dt, d_skip, glu_w, glu_b, batch, seq, tm):
    t = sproj.shape[0]
    nchunk = seq // S5_L
    ldt = jnp.broadcast_to(log_dt[:, None], a_re.shape)
    prow = jnp.stack([a_re, a_im, ldt], axis=1)
    pcol = jnp.stack([a_re, a_im, ldt], axis=2)
    bt = jnp.stack([b_re, b_im], axis=1).transpose(0, 1, 3, 2)
    btile = jnp.tile(bt, (1, 1, S5_L, 1))
    ctile = jnp.tile(jnp.stack([c_re, c_im], axis=1).transpose(0, 1, 3, 2), (1, 1, 1, S5_L))
    m, gr, gi, hr, hi, d = _s5_params(prow, pcol, bt, btile, ctile)
    u = sproj[:, 0:S5_W].astype(BF16)
    ug = u.reshape(batch * nchunk, S5_L, S5_G, S5_C).transpose(2, 0, 1, 3).reshape(S5_G, batch * nchunk, S5_L * S5_C)
    yg = _s5_scan(ug, m, gr, gi, hr, hi, d, nchunk)
    y = yg.reshape(S5_G, batch * nchunk, S5_L, S5_C).transpose(1, 2, 0, 3).reshape(t, S5_W)
    return _s5_post(y, sproj, d_skip[None, :], glu_w.astype(BF16), glu_b[None, :], tm)


def _pick(n, pref):
    b = min(n, pref)
    while n % b:
        b //= 2
    return b


def kernel(x, norm_w, w_in, w_out, hgrn_lb_logits, hgrn_norm_w, s5_a_re, s5_a_im, s5_b_re, s5_b_im, s5_c_re, s5_c_im, s5_log_dt, s5_d, s5_glu_w, s5_glu_b, diff_lq1, diff_lk1, diff_lq2, diff_lk2, diff_subln_w, final_norm_w):
    batch, seq, _ = x.shape
    depth = norm_w.shape[0]
    t = batch * seq
    tm = _pick(seq, 512)
    tb = _pick(seq, 512)
    ts = _pick(seq, 512)
    rope = _rope_tables(seq)
    h_res = x.astype(F32).reshape(t, D_MODEL)
    for l in range(depth):
        hp, sp, aq, ak, av, ag = _inproj(h_res, norm_w[l][None, :].astype(F32), w_in[l].astype(BF16), rope, seq, tm)
        mix_h = _hgrn(hp, hgrn_lb_logits.astype(F32), jnp.tile(hgrn_norm_w[l].astype(F32), HGRN_W // HGRN_HD)[None, :],
                      l, batch, seq, ts)
        mix_s = _s5(sp, s5_a_re[l], s5_a_im[l], s5_b_re[l], s5_b_im[l], s5_c_re[l], s5_c_im[l], s5_log_dt[l],
                    s5_d[l], s5_glu_w[l], s5_glu_b[l], batch, seq, tm)
        lam_init = 0.8 - 0.6 * math.exp(-0.3 * l)
        mix_a = _attention(aq, ak, av, ag, diff_lq1[l][None, :], diff_lk1[l][None, :], diff_lq2[l][None, :],
                           diff_lk2[l][None, :], diff_subln_w[l][None, :], lam_init, batch, seq, tb)
        h_res = _outproj(h_res, mix_h, mix_s, mix_a, w_out[l].astype(BF16), final_norm_w[None, :].astype(F32),
                         l == depth - 1, tm)
    return h_res.reshape(batch, seq, D_MODEL).astype(x.dtype)
```

```python
import functools
import math

import jax
import jax.numpy as jnp
from jax import lax
from jax.experimental import pallas as pl
from jax.experimental.pallas import tpu as pltpu

F32 = jnp.float32
BF16 = jnp.bfloat16

D_MODEL = 1024
HGRN_W = 256
HGRN_HD = 64
CHUNK = 16
LB_FLOOR = 1e-30
S5_W = 256
S5_C = 16
S5_G = S5_W // S5_C
S5_P = 64
S5_L = 16
ATT_W = 512
ATT_DH = 64
ATT_DV = 128
ATT_H = ATT_W // ATT_DV
ATT_CT = 256
ATT_VA = ATT_DV + 16
ROPE_DIM = 16
ROPE_THETA = 500000.0
MASK_VALUE = -1e30
Q_SCALE = ATT_DH ** -0.5 * math.log2(math.e)
EPS = 1e-6
D_IN = 4 * HGRN_W + 2 * S5_W + 4 * ATT_W
COL_S5 = 4 * HGRN_W
COL_AQ = COL_S5 + 2 * S5_W
COL_AK = COL_AQ + ATT_W
COL_AV = COL_AK + ATT_W
COL_AG = COL_AV + ATT_W

VMEM_LIMIT = 56 * 1024 * 1024


def _cparams(sem):
    return pltpu.CompilerParams(dimension_semantics=sem, vmem_limit_bytes=VMEM_LIMIT)


def _silu(x):
    return x * jax.nn.sigmoid(x)


def _dot(a, b):
    return jnp.dot(a, b, preferred_element_type=F32)


def _dot_split(a, b):
    ah = a.astype(BF16)
    al = (a - ah.astype(F32)).astype(BF16)
    bh = b.astype(BF16)
    bl = (b - bh.astype(F32)).astype(BF16)
    return _dot(ah, bh) + _dot(ah, bl) + _dot(al, bh)


def _rope(t, rc, ra, rb):
    return t * rc + pltpu.roll(t, 128 - ROPE_DIM // 2, axis=1) * ra + pltpu.roll(t, ROPE_DIM // 2, axis=1) * rb


def _inproj_kernel(x_ref, nw_ref, w_ref, rope_ref, h_ref, s_ref, q_ref, k_ref, v_ref, g_ref):
    x = x_ref[...]
    hn = x * lax.rsqrt(jnp.mean(x * x, axis=-1, keepdims=True) + EPS) * nw_ref[...]
    proj = _dot(hn.astype(BF16), w_ref[...])
    h_ref[...] = proj[:, :COL_S5]
    s_ref[...] = proj[:, COL_S5:COL_AQ]
    rc = rope_ref[:, 0:128]
    ra = rope_ref[:, 128:256]
    rb = rope_ref[:, 256:384]
    for h in range(ATT_H):
        lo = h * ATT_DV
        q = proj[:, COL_AQ + lo:COL_AQ + lo + ATT_DV]
        k = proj[:, COL_AK + lo:COL_AK + lo + ATT_DV]
        q_ref[:, lo:lo + ATT_DV] = (_rope(q, rc, ra, rb) * Q_SCALE).astype(BF16)
        k_ref[:, lo:lo + ATT_DV] = _rope(k, rc, ra, rb).astype(BF16)
    v_ref[...] = proj[:, COL_AV:COL_AG].astype(BF16)
    g_ref[...] = proj[:, COL_AG:]


def _rope_tables(seq):
    pos = jnp.arange(seq, dtype=F32)
    inv_freq = ROPE_THETA ** (-jnp.arange(0, ROPE_DIM, 2, dtype=F32) / ROPE_DIM)
    ang = pos[:, None] * inv_freq[None, :]
    cos, sin = jnp.cos(ang), jnp.sin(ang)
    half = ROPE_DIM // 2
    zeros = jnp.zeros((seq, ATT_DH - ROPE_DIM), F32)
    zh = jnp.zeros((seq, half), F32)
    rc = jnp.concatenate([cos, cos, jnp.ones((seq, ATT_DH - ROPE_DIM), F32)], axis=1)
    ra = jnp.concatenate([-sin, zh, zeros], axis=1)
    rb = jnp.concatenate([zh, sin, zeros], axis=1)
    return jnp.concatenate([jnp.tile(t, (1, 2)) for t in (rc, ra, rb)], axis=1)


def _inproj(x2, norm_w, w_bf, rope, seq, tm):
    t = x2.shape[0]
    nblk = seq // tm
    row = lambda i: (i, 0)
    return pl.pallas_call(
        _inproj_kernel,
        grid=(t // tm,),
        in_specs=[pl.BlockSpec((tm, D_MODEL), row),
                  pl.BlockSpec((1, D_MODEL), lambda i: (0, 0)),
                  pl.BlockSpec((D_MODEL, D_IN), lambda i: (0, 0)),
                  pl.BlockSpec((tm, 384), lambda i: (i % nblk, 0))],
        out_specs=[pl.BlockSpec((tm, COL_S5), row), pl.BlockSpec((tm, 2 * S5_W), row),
                   pl.BlockSpec((tm, ATT_W), row), pl.BlockSpec((tm, ATT_W), row),
                   pl.BlockSpec((tm, ATT_W), row), pl.BlockSpec((tm, ATT_W), row)],
        out_shape=[jax.ShapeDtypeStruct((t, COL_S5), F32), jax.ShapeDtypeStruct((t, 2 * S5_W), F32),
                   jax.ShapeDtypeStruct((t, ATT_W), BF16), jax.ShapeDtypeStruct((t, ATT_W), BF16),
                   jax.ShapeDtypeStruct((t, ATT_W), BF16), jax.ShapeDtypeStruct((t, ATT_W), F32)],
        compiler_params=_cparams(("parallel",)),
        name="inproj",
    )(x2, norm_w, w_bf, rope)


def _outproj_kernel(res_ref, mh_ref, ms_ref, ma_ref, w_ref, fw_ref, o_ref, *, final):
    acc = res_ref[...]
    acc = acc + _dot(mh_ref[...].astype(BF16), w_ref[0:HGRN_W, :])
    acc = acc + _dot(ms_ref[...].astype(BF16), w_ref[HGRN_W:HGRN_W + S5_W, :])
    acc = acc + _dot(ma_ref[...].astype(BF16), w_ref[HGRN_W + S5_W:, :])
    if final:
        acc = acc * lax.rsqrt(jnp.mean(acc * acc, axis=-1, keepdims=True) + EPS) * fw_ref[...]
    o_ref[...] = acc


def _outproj(res, mh, ms, ma, w_bf, fw, final, tm):
    t = res.shape[0]
    row = lambda i: (i, 0)
    return pl.pallas_call(
        functools.partial(_outproj_kernel, final=final),
        grid=(t // tm,),
        in_specs=[pl.BlockSpec((tm, D_MODEL), row), pl.BlockSpec((tm, HGRN_W), row),
                  pl.BlockSpec((tm, S5_W), row), pl.BlockSpec((tm, ATT_W), row),
                  pl.BlockSpec((D_MODEL, D_MODEL), lambda i: (0, 0)),
                  pl.BlockSpec((1, D_MODEL), lambda i: (0, 0))],
        out_specs=pl.BlockSpec((tm, D_MODEL), row),
        out_shape=jax.ShapeDtypeStruct((t, D_MODEL), F32),
        compiler_params=_cparams(("parallel",)),
        name="outproj",
    )(res, mh, ms, ma, w_bf, fw)


def _attn_kernel(q_ref, k_ref, v_ref, g_ref, lq1_ref, lk1_ref, lq2_ref, lk2_ref, sw_ref, o_ref,
                 vt_s, qq_s, acc_s, s0_s, s1_s, *, lam_init, tq, tk):
    i = pl.program_id(2)
    nkb = vt_s.shape[0]
    ntile = 2 * tq // ATT_CT
    per_map = tq // ATT_CT
    nt = (((1,), (1,)), ((), ()))

    @pl.when(i == 0)
    def _():
        for j in range(nkb):
            vt_s[j, 0:ATT_DV, :] = v_ref[j * tk:(j + 1) * tk, :].T
            vt_s[j, ATT_DV:, :] = jnp.ones((ATT_VA - ATT_DV, tk), BF16)

    q = q_ref[...]
    lane = lax.broadcasted_iota(jnp.int32, (tq, ATT_DV), 1)
    qq_s[0:tq, :] = jnp.where(lane < ATT_DH, q, jnp.zeros_like(q))
    qq_s[tq:2 * tq, :] = jnp.where(lane >= ATT_DH, q, jnp.zeros_like(q))
    acc_s[...] = jnp.zeros(acc_s.shape, F32)

    def tile_mode(c, koff):
        qs = (c % per_map) * ATT_CT
        if koff is None or qs >= koff + tk - 1:
            return "full"
        return "skip" if qs + ATT_CT - 1 < koff else "causal"

    def stage(kq, sq_s, koff_q, ssm_s, ksm, koff_sm, m_run, cmax):
        new_m, new_cmax = list(m_run), [None] * ntile
        if kq is not None:
            kb = k_ref[pl.ds(kq * tk if isinstance(kq, int) else pl.multiple_of(kq * tk, tk), tk), :]
        if ksm is not None:
            vtb = vt_s[ksm]
        for c in range(ntile):
            cols = slice(c * ATT_CT, (c + 1) * ATT_CT)
            if kq is not None and tile_mode(c, koff_q) != "skip":
                s = lax.dot_general(kb, qq_s[cols, :], nt, preferred_element_type=F32)
                if tile_mode(c, koff_q) == "causal":
                    kpos = lax.broadcasted_iota(jnp.int32, (tk, ATT_CT), 0) + koff_q
                    qpos = lax.broadcasted_iota(jnp.int32, (tk, ATT_CT), 1) + (c % per_map) * ATT_CT
                    s = jnp.where(kpos <= qpos, s, MASK_VALUE)
                sq_s[:, cols] = s
                new_cmax[c] = jnp.max(s, axis=0, keepdims=True)
            if ksm is not None and tile_mode(c, koff_sm) != "skip":
                m_new = jnp.maximum(m_run[c], cmax[c])
                alpha = jnp.exp2(m_run[c] - m_new)
                p = jnp.exp2(ssm_s[:, cols] - m_new)
                acc_s[:, cols] = alpha * acc_s[:, cols] + _dot(vtb, p.astype(BF16))
                new_m[c] = m_new
        return tuple(new_m), tuple(new_cmax)

    neg = tuple(jnp.full((1, ATT_CT), MASK_VALUE, F32) for _ in range(ntile))
    d0 = 2 * i
    _, cm = stage(d0, s0_s, 0, None, None, None, neg, None)
    m_run, cm1 = stage(d0 + 1, s1_s, tk, s0_s, d0, 0, neg, cm)
    cm1 = tuple(neg[c] if x is None else x for c, x in enumerate(cm1))
    m_run, cm0 = stage(0, s0_s, None, s1_s, d0 + 1, tk, m_run, cm1)

    def pair(t, carry):
        m_run, cm0 = carry
        m_run, cm1 = stage(2 * t + 1, s1_s, None, s0_s, 2 * t, None, m_run, cm0)
        m_run, cm0 = stage(jnp.minimum(2 * t + 2, d0 - 1), s0_s, None, s1_s, 2 * t + 1, None, m_run, cm1)
        return m_run, cm0

    lax.fori_loop(0, i, pair, (m_run, cm0))

    lam = (jnp.exp(jnp.sum(lq1_ref[...] * lk1_ref[...], axis=-1, keepdims=True))
           - jnp.exp(jnp.sum(lq2_ref[...] * lk2_ref[...], axis=-1, keepdims=True)) + lam_init)
    on = acc_s[0:ATT_DV, :] / acc_s[ATT_DV:ATT_DV + 1, :]
    ot = on[:, 0:tq] - lam * on[:, tq:2 * tq]
    o = ot.T
    o = o * lax.rsqrt(jnp.mean(o * o, axis=-1, keepdims=True) + EPS) * sw_ref[...] * (1.0 - lam_init)
    o_ref[...] = (o * _silu(g_ref[...])).astype(o_ref.dtype)


def _attention(q, k, v, g, lq1, lk1, lq2, lk2, sw, lam_init, batch, seq, tq):
    t = q.shape[0]
    tk = tq // 2
    nq = seq // tq
    qmap = lambda b, h, i: (b * nq + i, h)
    kvmap = lambda b, h, i: (b, h)
    cmap = lambda b, h, i: (0, 0)
    return pl.pallas_call(
        functools.partial(_attn_kernel, lam_init=lam_init, tq=tq, tk=tk),
        grid=(batch, ATT_H, nq),
        in_specs=[pl.BlockSpec((tq, ATT_DV), qmap), pl.BlockSpec((seq, ATT_DV), kvmap),
                  pl.BlockSpec((seq, ATT_DV), kvmap), pl.BlockSpec((tq, ATT_DV), qmap),
                  pl.BlockSpec((1, ATT_DH), cmap), pl.BlockSpec((1, ATT_DH), cmap),
                  pl.BlockSpec((1, ATT_DH), cmap), pl.BlockSpec((1, ATT_DH), cmap),
                  pl.BlockSpec((1, ATT_DV), cmap)],
        out_specs=pl.BlockSpec((tq, ATT_DV), qmap),
        out_shape=jax.ShapeDtypeStruct((t, ATT_W), F32),
        scratch_shapes=[pltpu.VMEM((seq // tk, ATT_VA, tk), BF16), pltpu.VMEM((2 * tq, ATT_DV), BF16),
                        pltpu.VMEM((ATT_VA, 2 * tq), F32),
                        pltpu.VMEM((tk, 2 * tq), F32), pltpu.VMEM((tk, 2 * tq), F32)],
        compiler_params=_cparams(("parallel", "parallel", "arbitrary")),
        name="diffattn",
    )(q, k, v, g, lq1, lk1, lq2, lk2, sw)


def _head_ones(dtype):
    r = lax.broadcasted_iota(jnp.int32, (HGRN_W, HGRN_W), 0) // HGRN_HD
    c = lax.broadcasted_iota(jnp.int32, (HGRN_W, HGRN_W), 1) // HGRN_HD
    return (r == c).astype(dtype)


def _hgrn_kernel(h_ref, lbl_ref, nw_ref, o_ref, st_ref, q_s, k_s, v_s, c_s, qt_s, kh_s, oo_s, *, layer, ts):
    @pl.when(pl.program_id(1) == 0)
    def _():
        st_ref[...] = jnp.zeros(st_ref.shape, F32)

    hq = h_ref[:, 0:HGRN_W]
    hf = h_ref[:, HGRN_W:2 * HGRN_W]
    hi = h_ref[:, 2 * HGRN_W:3 * HGRN_W]
    hg = h_ref[:, 3 * HGRN_W:4 * HGRN_W]

    lg = lbl_ref[...]
    e = jnp.exp(lg - jnp.max(lg, axis=0, keepdims=True))
    p = e / jnp.sum(e, axis=0, keepdims=True)
    lb = jnp.sum(p[0:layer + 1, :], axis=0, keepdims=True) - p[0:1, :]
    log_lb = jnp.log(jnp.maximum(lb, LB_FLOOR))
    log_sig = -(jnp.maximum(-hf, 0.0) + jnp.log1p(jnp.exp(-jnp.abs(hf))))
    b = jnp.log1p(-lb) + log_sig
    log_f = jnp.maximum(log_lb, b) + jnp.log1p(jnp.exp(-jnp.abs(log_lb - b)))
    k = 1.0 - jnp.exp(log_f)
    q = _silu(hq)

    r16 = lax.broadcasted_iota(jnp.int32, (ts, HGRN_W), 0) % CHUNK
    cum = log_f
    rem = jnp.where(r16 < CHUNK - 1, pltpu.roll(log_f, ts - 1, axis=0), 0.0)
    sh = 1
    while sh < CHUNK:
        cum = cum + jnp.where(r16 >= sh, pltpu.roll(cum, sh, axis=0), 0.0)
        rem = rem + jnp.where(r16 < CHUNK - sh, pltpu.roll(rem, ts - sh, axis=0), 0.0)
        sh *= 2
    q_s[...] = q
    k_s[...] = k
    v_s[...] = hi
    c_s[...] = cum
    qt_s[...] = q * jnp.exp(cum)
    kh_s[...] = k * jnp.exp(rem)

    ones_bf = _head_ones(BF16)
    ones_f = _head_ones(F32)
    trow = lax.broadcasted_iota(jnp.int32, (CHUNK, HGRN_W), 0)

    def chunk(n, carry):
        r0 = pl.multiple_of(n * CHUNK, CHUNK)
        c = c_s[pl.ds(r0, CHUNK), :]
        qc = q_s[pl.ds(r0, CHUNK), :]
        kc = k_s[pl.ds(r0, CHUNK), :]
        vc = v_s[pl.ds(r0, CHUNK), :]
        rows = []
        for s in range(CHUNK):
            msk = trow >= s
            d = jnp.where(msk, jnp.exp(jnp.where(msk, c - c[s:s + 1, :], 0.0)), 0.0)
            rows.append((qc * d * kc[s:s + 1, :]).astype(BF16))
        sc = _dot(jnp.concatenate(rows, axis=0), ones_bf)
        o = jnp.zeros((CHUNK, HGRN_W), F32)
        for s in range(CHUNK):
            o = o + sc[s * CHUNK:(s + 1) * CHUNK, :] * vc[s:s + 1, :]
        st = st_ref[...]
        qt = qt_s[pl.ds(r0, CHUNK), :]
        kh = kh_s[pl.ds(r0, CHUNK), :]
        o = o + lax.dot_general(qt.astype(BF16), st.astype(BF16), (((1,), (1,)), ((), ())),
                                preferred_element_type=F32)
        upd = lax.dot_general(vc.astype(BF16), kh.astype(BF16), (((0,), (0,)), ((), ())),
                              preferred_element_type=F32)
        dec = jnp.exp(c[CHUNK - 1:CHUNK, :])
        st_ref[...] = st * dec + upd * ones_f
        oo_s[pl.ds(r0, CHUNK), :] = o
        return carry

    lax.fori_loop(0, ts // CHUNK, chunk, 0)

    o = oo_s[...]
    ms = _dot_split(o * o, ones_f) * (1.0 / HGRN_HD)
    o_ref[...] = (o * lax.rsqrt(ms + EPS) * nw_ref[...] * _silu(hg)).astype(o_ref.dtype)


def _hgrn(hproj, lb_logits, nw_tiled, layer, batch, seq, ts):
    t = hproj.shape[0]
    ns = seq // ts
    depth = lb_logits.shape[0]
    blk = pltpu.VMEM((ts, HGRN_W), F32)
    return pl.pallas_call(
        functools.partial(_hgrn_kernel, layer=layer, ts=ts),
        grid=(batch, ns),
        in_specs=[pl.BlockSpec((ts, 4 * HGRN_W), lambda b, i: (b * ns + i, 0)),
                  pl.BlockSpec((depth, HGRN_W), lambda b, i: (0, 0)),
                  pl.BlockSpec((1, HGRN_W), lambda b, i: (0, 0))],
        out_specs=pl.BlockSpec((ts, HGRN_W), lambda b, i: (b * ns + i, 0)),
        out_shape=jax.ShapeDtypeStruct((t, HGRN_W), F32),
        scratch_shapes=[pltpu.VMEM((HGRN_W, HGRN_W), F32), blk, blk, blk, blk, blk, blk, blk],
        compiler_params=_cparams(("parallel", "arbitrary")),
        name="hgrn2",
    )(hproj, lb_logits, nw_tiled)


def _s5_param_kernel(prow_ref, pcol_ref, bt_ref, btile_ref, ctile_ref,
                     m_ref, gr_ref, gi_ref, hr_ref, hi_ref, d_ref):
    are, aim = prow_ref[0:1, :], prow_ref[1:2, :]
    dt = jnp.exp(prow_ref[2:3, :])
    lr, li = dt * are, dt * aim
    mag = jnp.exp(lr)
    abr, abi = mag * jnp.cos(li), mag * jnp.sin(li)
    den = are * are + aim * aim
    nr, ni = abr - 1.0, abi
    zr = (nr * are + ni * aim) / den
    zi = (ni * are - nr * aim) / den

    arec, aimc = pcol_ref[:, 0:1], pcol_ref[:, 1:2]
    dtc = jnp.exp(pcol_ref[:, 2:3])
    lrc, lic = dtc * arec, dtc * aimc
    cr, ci = ctile_ref[0], ctile_ref[1]
    tl = (lax.broadcasted_iota(jnp.int32, (S5_P, S5_L * S5_C), 1) // S5_C).astype(F32)

    def powers(n):
        mg = jnp.exp(n * lrc)
        return mg * jnp.cos(n * lic), mg * jnp.sin(n * lic)

    er, ei = powers(tl)
    w1 = er * cr - ei * ci
    w2 = -(ei * cr + er * ci)
    bbr = zr * bt_ref[0] - zi * bt_ref[1]
    bbi = zr * bt_ref[1] + zi * bt_ref[0]
    r0 = _dot_split(bbr, w1) + _dot_split(bbi, w2)
    lane = lax.broadcasted_iota(jnp.int32, (S5_C, S5_L * S5_C), 1)
    m_ref[0:S5_C, :] = r0.astype(m_ref.dtype)
    for s in range(1, S5_L):
        blk = jnp.where(lane >= s * S5_C, pltpu.roll(r0, s * S5_C, axis=1), 0.0)
        m_ref[s * S5_C:(s + 1) * S5_C, :] = blk.astype(m_ref.dtype)

    e = (S5_L - 1 - lax.broadcasted_iota(jnp.int32, (S5_L * S5_C, S5_P), 0) // S5_C).astype(F32)
    mg = jnp.exp(e * lr)
    fr, fi = mg * jnp.cos(e * li), mg * jnp.sin(e * li)
    btr = zr * btile_ref[0] - zi * btile_ref[1]
    bti = zr * btile_ref[1] + zi * btile_ref[0]
    gr_ref[...] = (fr * btr - fi * bti).astype(gr_ref.dtype)
    gi_ref[...] = (fr * bti + fi * btr).astype(gi_ref.dtype)

    pr, pi = powers(tl + 1.0)
    hr_ref[...] = (cr * pr - ci * pi).astype(hr_ref.dtype)
    hi_ref[...] = (-(cr * pi + ci * pr)).astype(hi_ref.dtype)

    mg = jnp.exp(float(S5_L) * lr)
    d_ref[0:1, :] = mg * jnp.cos(float(S5_L) * li)
    d_ref[1:2, :] = mg * jnp.sin(float(S5_L) * li)


def _s5_params(prow, pcol, bt, btile, ctile):
    g = prow.shape[0]
    wide = S5_L * S5_C
    m3 = lambda i: (i, 0, 0)
    m4 = lambda i: (i, 0, 0, 0)
    return pl.pallas_call(
        _s5_param_kernel,
        grid=(g,),
        in_specs=[pl.BlockSpec((None, 3, S5_P), m3), pl.BlockSpec((None, S5_P, 3), m3),
                  pl.BlockSpec((None, 2, S5_C, S5_P), m4), pl.BlockSpec((None, 2, wide, S5_P), m4),
                  pl.BlockSpec((None, 2, S5_P, wide), m4)],
        out_specs=[pl.BlockSpec((None, wide, wide), m3), pl.BlockSpec((None, wide, S5_P), m3),
                   pl.BlockSpec((None, wide, S5_P), m3), pl.BlockSpec((None, S5_P, wide), m3),
                   pl.BlockSpec((None, S5_P, wide), m3), pl.BlockSpec((None, 2, S5_P), m3)],
        out_shape=[jax.ShapeDtypeStruct((g, wide, wide), BF16), jax.ShapeDtypeStruct((g, wide, S5_P), BF16),
                   jax.ShapeDtypeStruct((g, wide, S5_P), BF16), jax.ShapeDtypeStruct((g, S5_P, wide), BF16),
                   jax.ShapeDtypeStruct((g, S5_P, wide), BF16), jax.ShapeDtypeStruct((g, 2, S5_P), F32)],
        compiler_params=_cparams(("parallel",)),
        name="s5_params",
    )(prow, pcol, bt, btile, ctile)


def _s5_scan_kernel(u_ref, m_ref, gr_ref, gi_ref, hr_ref, hi_ref, d_ref, y_ref, *, nchunk):
    u = u_ref[...]
    rows = u.shape[0]
    y = _dot(u, m_ref[...])
    xr = _dot(u, gr_ref[...])
    xi = _dot(u, gi_ref[...])
    rn = lax.broadcasted_iota(jnp.int32, (rows, S5_P), 0) % nchunk
    pr, pi = d_ref[0:1, :], d_ref[1:2, :]
    sh = 1
    while sh < nchunk:
        keep = rn >= sh
        sr = jnp.where(keep, pltpu.roll(xr, sh, axis=0), 0.0)
        si = jnp.where(keep, pltpu.roll(xi, sh, axis=0), 0.0)
        xr, xi = xr + pr * sr - pi * si, xi + pr * si + pi * sr
        pr, pi = pr * pr - pi * pi, 2.0 * pr * pi
        sh *= 2
    first = rn >= 1
    xr = jnp.where(first, pltpu.roll(xr, 1, axis=0), 0.0)
    xi = jnp.where(first, pltpu.roll(xi, 1, axis=0), 0.0)
    y = y + _dot(xr.astype(BF16), hr_ref[...]) + _dot(xi.astype(BF16), hi_ref[...])
    y_ref[...] = y


def _s5_scan(ug, m, gr, gi, hr, hi, d, nchunk):
    g, rows, wide = ug.shape
    m3 = lambda i: (i, 0, 0)
    return pl.pallas_call(
        functools.partial(_s5_scan_kernel, nchunk=nchunk),
        grid=(g,),
        in_specs=[pl.BlockSpec((None, rows, wide), m3), pl.BlockSpec((None, wide, wide), m3),
                  pl.BlockSpec((None, wide, S5_P), m3), pl.BlockSpec((None, wide, S5_P), m3),
                  pl.BlockSpec((None, S5_P, wide), m3), pl.BlockSpec((None, S5_P, wide), m3),
                  pl.BlockSpec((None, 2, S5_P), m3)],
        out_specs=pl.BlockSpec((None, rows, wide), m3),
        out_shape=jax.ShapeDtypeStruct((g, rows, wide), F32),
        compiler_params=_cparams(("parallel",)),
        name="s5_scan",
    )(ug, m, gr, gi, hr, hi, d)


def _s5_post_kernel(y_ref, s_ref, d_ref, w_ref, b_ref, o_ref):
    u = s_ref[:, 0:S5_W]
    sg = s_ref[:, S5_W:2 * S5_W]
    y = y_ref[...] + d_ref[...] * u
    cdf = 0.5 * (1.0 + jnp.tanh(math.sqrt(2.0 / math.pi) * (y + 0.044715 * (y * y * y))))
    y = y * cdf
    z = _dot(y.astype(BF16), w_ref[...]) + b_ref[...]
    o_ref[...] = (y * jax.nn.sigmoid(z) * _silu(sg)).astype(o_ref.dtype)


def _s5_post(y, sproj, d, w_bf, b, tm):
    t = y.shape[0]
    row = lambda i: (i, 0)
    cst = lambda i: (0, 0)
    return pl.pallas_call(
        _s5_post_kernel,
        grid=(t // tm,),
        in_specs=[pl.BlockSpec((tm, S5_W), row), pl.BlockSpec((tm, 2 * S5_W), row),
                  pl.BlockSpec((1, S5_W), cst), pl.BlockSpec((S5_W, S5_W), cst), pl.BlockSpec((1, S5_W), cst)],
        out_specs=pl.BlockSpec((tm, S5_W), row),
        out_shape=jax.ShapeDtypeStruct((t, S5_W), F32),
        compiler_params=_cparams(("parallel",)),
        name="s5_post",
    )(y, sproj, d, w_bf, b)


def _s5(sproj, a_re, a_im, b_re, b_im, c_re, c_im, log_dt, d_skip, glu_w, glu_b, batch, seq, tm):
    t = sproj.shape[0]
    nchunk = seq // S5_L
    ldt = jnp.broadcast_to(log_dt[:, None], a_re.shape)
    prow = jnp.stack([a_re, a_im, ldt], axis=1)
    pcol = jnp.stack([a_re, a_im, ldt], axis=2)
    bt = jnp.stack([b_re, b_im], axis=1).transpose(0, 1, 3, 2)
    btile = jnp.tile(bt, (1, 1, S5_L, 1))
    ctile = jnp.tile(jnp.stack([c_re, c_im], axis=1).transpose(0, 1, 3, 2), (1, 1, 1, S5_L))
    m, gr, gi, hr, hi, d = _s5_params(prow, pcol, bt, btile, ctile)
    u = sproj[:, 0:S5_W].astype(BF16)
    ug = u.reshape(batch * nchunk, S5_L, S5_G, S5_C).transpose(2, 0, 1, 3).reshape(S5_G, batch * nchunk, S5_L * S5_C)
    yg = _s5_scan(ug, m, gr, gi, hr, hi, d, nchunk)
    y = yg.reshape(S5_G, batch * nchunk, S5_L, S5_C).transpose(1, 2, 0, 3).reshape(t, S5_W)
    return _s5_post(y, sproj, d_skip[None, :], glu_w.astype(BF16), glu_b[None, :], tm)


def _pick(n, pref):
    b = min(n, pref)
    while n % b:
        b //= 2
    return b


def kernel(x, norm_w, w_in, w_out, hgrn_lb_logits, hgrn_norm_w, s5_a_re, s5_a_im, s5_b_re, s5_b_im, s5_c_re, s5_c_im, s5_log_dt, s5_d, s5_glu_w, s5_glu_b, diff_lq1, diff_lk1, diff_lq2, diff_lk2, diff_subln_w, final_norm_w):
    batch, seq, _ = x.shape
    depth = norm_w.shape[0]
    t = batch * seq
    tm = _pick(seq, 512)
    tq = _pick(seq, 1024)
    ts = _pick(seq, 512)
    rope = _rope_tables(seq)
    h_res = x.astype(F32).reshape(t, D_MODEL)
    for l in range(depth):
        hp, sp, aq, ak, av, ag = _inproj(h_res, norm_w[l][None, :].astype(F32), w_in[l].astype(BF16), rope, seq, tm)
        mix_h = _hgrn(hp, hgrn_lb_logits.astype(F32), jnp.tile(hgrn_norm_w[l].astype(F32), HGRN_W // HGRN_HD)[None, :],
                      l, batch, seq, ts)
        mix_s = _s5(sp, s5_a_re[l], s5_a_im[l], s5_b_re[l], s5_b_im[l], s5_c_re[l], s5_c_im[l], s5_log_dt[l],
                    s5_d[l], s5_glu_w[l], s5_glu_b[l], batch, seq, tm)
        lam_init = 0.8 - 0.6 * math.exp(-0.3 * l)
        mix_a = _attention(aq, ak, av, ag, diff_lq1[l][None, :], diff_lk1[l][None, :], diff_lq2[l][None, :],
                           diff_lk2[l][None, :], diff_subln_w[l][None, :], lam_init, batch, seq, tq)
        h_res = _outproj(h_res, mix_h, mix_s, mix_a, w_out[l].astype(BF16), final_norm_w[None, :].astype(F32),
                         l == depth - 1, tm)
    return h_res.reshape(batch, seq, D_MODEL).astype(x.dtype)
```

```python
import functools
import math

import jax
import jax.numpy as jnp
from jax import lax
from jax.experimental import pallas as pl
from jax.experimental.pallas import tpu as pltpu

F32 = jnp.float32
BF16 = jnp.bfloat16

D_MODEL = 1024
HGRN_W = 256
HGRN_HD = 64
CHUNK = 16
HGRN_WR = 8 * 16 + 8 * 8
HGRN_PW = 2 * HGRN_HD
HGRN_GR = 128
LB_FLOOR = 1e-30
S5_W = 256
S5_C = 16
S5_G = S5_W // S5_C
S5_P = 64
S5_L = 16
ATT_W = 512
ATT_DH = 64
ATT_DV = 128
ATT_H = ATT_W // ATT_DV
ATT_CT = 256
ATT_VA = ATT_DV + 16
ROPE_DIM = 16
ROPE_THETA = 500000.0
MASK_VALUE = -1e30
Q_SCALE = ATT_DH ** -0.5 * math.log2(math.e)
EPS = 1e-6
D_IN = 4 * HGRN_W + 2 * S5_W + 4 * ATT_W
COL_S5 = 4 * HGRN_W
COL_AQ = COL_S5 + 2 * S5_W
COL_AK = COL_AQ + ATT_W
COL_AV = COL_AK + ATT_W
COL_AG = COL_AV + ATT_W

VMEM_LIMIT = 56 * 1024 * 1024


def _cparams(sem):
    return pltpu.CompilerParams(dimension_semantics=sem, vmem_limit_bytes=VMEM_LIMIT)


def _sigmoid(x):
    return 1.0 / (1.0 + jnp.exp(-x))


def _softplus_neg_abs(x):
    return jnp.log(1.0 + jnp.exp(-jnp.abs(x)))


def _silu(x):
    return x * _sigmoid(x)


def _dot(a, b):
    return jnp.dot(a, b, preferred_element_type=F32)


def _dot_split(a, b):
    ah = a.astype(BF16)
    al = (a - ah.astype(F32)).astype(BF16)
    bh = b.astype(BF16)
    bl = (b - bh.astype(F32)).astype(BF16)
    return _dot(ah, bh) + _dot(ah, bl) + _dot(al, bh)


def _rope(t, rc, ra, rb):
    return t * rc + pltpu.roll(t, 128 - ROPE_DIM // 2, axis=1) * ra + pltpu.roll(t, ROPE_DIM // 2, axis=1) * rb


def _inproj_kernel(x_ref, nw_ref, w_ref, rope_ref, h_ref, s_ref, q_ref, k_ref, v_ref, g_ref):
    x = x_ref[...]
    hn = x * lax.rsqrt(jnp.mean(x * x, axis=-1, keepdims=True) + EPS) * nw_ref[...]
    proj = _dot(hn.astype(BF16), w_ref[...])
    h_ref[...] = proj[:, :COL_S5]
    s_ref[...] = proj[:, COL_S5:COL_AQ]
    rc = rope_ref[:, 0:128]
    ra = rope_ref[:, 128:256]
    rb = rope_ref[:, 256:384]
    for h in range(ATT_H):
        lo = h * ATT_DV
        q = proj[:, COL_AQ + lo:COL_AQ + lo + ATT_DV]
        k = proj[:, COL_AK + lo:COL_AK + lo + ATT_DV]
        q_ref[:, lo:lo + ATT_DV] = (_rope(q, rc, ra, rb) * Q_SCALE).astype(BF16)
        k_ref[:, lo:lo + ATT_DV] = _rope(k, rc, ra, rb).astype(BF16)
    v_ref[...] = proj[:, COL_AV:COL_AG].astype(BF16)
    g_ref[...] = proj[:, COL_AG:]


def _rope_tables(seq):
    pos = jnp.arange(seq, dtype=F32)
    inv_freq = ROPE_THETA ** (-jnp.arange(0, ROPE_DIM, 2, dtype=F32) / ROPE_DIM)
    ang = pos[:, None] * inv_freq[None, :]
    cos, sin = jnp.cos(ang), jnp.sin(ang)
    half = ROPE_DIM // 2
    zeros = jnp.zeros((seq, ATT_DH - ROPE_DIM), F32)
    zh = jnp.zeros((seq, half), F32)
    rc = jnp.concatenate([cos, cos, jnp.ones((seq, ATT_DH - ROPE_DIM), F32)], axis=1)
    ra = jnp.concatenate([-sin, zh, zeros], axis=1)
    rb = jnp.concatenate([zh, sin, zeros], axis=1)
    return jnp.concatenate([jnp.tile(t, (1, 2)) for t in (rc, ra, rb)], axis=1)


def _inproj(x2, norm_w, w_bf, rope, seq, tm):
    t = x2.shape[0]
    nblk = seq // tm
    row = lambda i: (i, 0)
    return pl.pallas_call(
        _inproj_kernel,
        grid=(t // tm,),
        in_specs=[pl.BlockSpec((tm, D_MODEL), row),
                  pl.BlockSpec((1, D_MODEL), lambda i: (0, 0)),
                  pl.BlockSpec((D_MODEL, D_IN), lambda i: (0, 0)),
                  pl.BlockSpec((tm, 384), lambda i: (i % nblk, 0))],
        out_specs=[pl.BlockSpec((tm, COL_S5), row), pl.BlockSpec((tm, 2 * S5_W), row),
                   pl.BlockSpec((tm, ATT_W), row), pl.BlockSpec((tm, ATT_W), row),
                   pl.BlockSpec((tm, ATT_W), row), pl.BlockSpec((tm, ATT_W), row)],
        out_shape=[jax.ShapeDtypeStruct((t, COL_S5), F32), jax.ShapeDtypeStruct((t, 2 * S5_W), F32),
                   jax.ShapeDtypeStruct((t, ATT_W), BF16), jax.ShapeDtypeStruct((t, ATT_W), BF16),
                   jax.ShapeDtypeStruct((t, ATT_W), BF16), jax.ShapeDtypeStruct((t, ATT_W), F32)],
        compiler_params=_cparams(("parallel",)),
        name="inproj",
    )(x2, norm_w, w_bf, rope)


def _outproj_kernel(res_ref, mh_ref, ms_ref, ma_ref, w_ref, fw_ref, o_ref, *, final):
    acc = res_ref[...]
    acc = acc + _dot(mh_ref[...].astype(BF16), w_ref[0:HGRN_W, :])
    acc = acc + _dot(ms_ref[...].astype(BF16), w_ref[HGRN_W:HGRN_W + S5_W, :])
    acc = acc + _dot(ma_ref[...].astype(BF16), w_ref[HGRN_W + S5_W:, :])
    if final:
        acc = acc * lax.rsqrt(jnp.mean(acc * acc, axis=-1, keepdims=True) + EPS) * fw_ref[...]
    o_ref[...] = acc


def _outproj(res, mh, ms, ma, w_bf, fw, final, tm):
    t = res.shape[0]
    row = lambda i: (i, 0)
    return pl.pallas_call(
        functools.partial(_outproj_kernel, final=final),
        grid=(t // tm,),
        in_specs=[pl.BlockSpec((tm, D_MODEL), row), pl.BlockSpec((tm, HGRN_W), row),
                  pl.BlockSpec((tm, S5_W), row), pl.BlockSpec((tm, ATT_W), row),
                  pl.BlockSpec((D_MODEL, D_MODEL), lambda i: (0, 0)),
                  pl.BlockSpec((1, D_MODEL), lambda i: (0, 0))],
        out_specs=pl.BlockSpec((tm, D_MODEL), row),
        out_shape=jax.ShapeDtypeStruct((t, D_MODEL), F32),
        compiler_params=_cparams(("parallel",)),
        name="outproj",
    )(res, mh, ms, ma, w_bf, fw)


def _attn_kernel(q_ref, k_ref, v_ref, g_ref, lq1_ref, lk1_ref, lq2_ref, lk2_ref, sw_ref, o_ref,
                 vt_s, qq_s, acc_s, s0_s, s1_s, *, lam_init, tq, tk):
    i = pl.program_id(2)
    nkb = vt_s.shape[0]
    ntile = 2 * tq // ATT_CT
    per_map = tq // ATT_CT
    nt = (((1,), (1,)), ((), ()))

    @pl.when(i == 0)
    def _():
        for j in range(nkb):
            vt_s[j, 0:ATT_DV, :] = v_ref[j * tk:(j + 1) * tk, :].T
            vt_s[j, ATT_DV:, :] = jnp.ones((ATT_VA - ATT_DV, tk), BF16)

    q = q_ref[...]
    lane = lax.broadcasted_iota(jnp.int32, (tq, ATT_DV), 1)
    qq_s[0:tq, :] = jnp.where(lane < ATT_DH, q, jnp.zeros_like(q))
    qq_s[tq:2 * tq, :] = jnp.where(lane >= ATT_DH, q, jnp.zeros_like(q))
    acc_s[...] = jnp.zeros(acc_s.shape, F32)

    def tile_mode(c, koff):
        qs = (c % per_map) * ATT_CT
        if koff is None or qs >= koff + tk - 1:
            return "full"
        return "skip" if qs + ATT_CT - 1 < koff else "causal"

    def stage(kq, sq_s, koff_q, ssm_s, ksm, koff_sm, m_run, cmax):
        new_m, new_cmax = list(m_run), [None] * ntile
        if kq is not None:
            kb = k_ref[pl.ds(kq * tk if isinstance(kq, int) else pl.multiple_of(kq * tk, tk), tk), :]
        if ksm is not None:
            vtb = vt_s[ksm]
        for c in range(ntile):
            cols = slice(c * ATT_CT, (c + 1) * ATT_CT)
            if kq is not None and tile_mode(c, koff_q) != "skip":
                s = lax.dot_general(kb, qq_s[cols, :], nt, preferred_element_type=F32)
                if tile_mode(c, koff_q) == "causal":
                    kpos = lax.broadcasted_iota(jnp.int32, (tk, ATT_CT), 0) + koff_q
                    qpos = lax.broadcasted_iota(jnp.int32, (tk, ATT_CT), 1) + (c % per_map) * ATT_CT
                    s = jnp.where(kpos <= qpos, s, MASK_VALUE)
                sq_s[:, cols] = s
                new_cmax[c] = jnp.max(s, axis=0, keepdims=True)
            if ksm is not None and tile_mode(c, koff_sm) != "skip":
                m_new = jnp.maximum(m_run[c], cmax[c])
                alpha = jnp.exp2(m_run[c] - m_new)
                p = jnp.exp2(ssm_s[:, cols] - m_new)
                acc_s[:, cols] = alpha * acc_s[:, cols] + _dot(vtb, p.astype(BF16))
                new_m[c] = m_new
        return tuple(new_m), tuple(new_cmax)

    neg = tuple(jnp.full((1, ATT_CT), MASK_VALUE, F32) for _ in range(ntile))
    d0 = 2 * i
    _, cm = stage(d0, s0_s, 0, None, None, None, neg, None)
    m_run, cm1 = stage(d0 + 1, s1_s, tk, s0_s, d0, 0, neg, cm)
    cm1 = tuple(neg[c] if x is None else x for c, x in enumerate(cm1))
    m_run, cm0 = stage(0, s0_s, None, s1_s, d0 + 1, tk, m_run, cm1)

    def pair(t, carry):
        m_run, cm0 = carry
        m_run, cm1 = stage(2 * t + 1, s1_s, None, s0_s, 2 * t, None, m_run, cm0)
        m_run, cm0 = stage(jnp.minimum(2 * t + 2, d0 - 1), s0_s, None, s1_s, 2 * t + 1, None, m_run, cm1)
        return m_run, cm0

    lax.fori_loop(0, i, pair, (m_run, cm0))

    lam = (jnp.exp(jnp.sum(lq1_ref[...] * lk1_ref[...], axis=-1, keepdims=True))
           - jnp.exp(jnp.sum(lq2_ref[...] * lk2_ref[...], axis=-1, keepdims=True)) + lam_init)
    on = acc_s[0:ATT_DV, :] / acc_s[ATT_DV:ATT_DV + 1, :]
    ot = on[:, 0:tq] - lam * on[:, tq:2 * tq]
    o = ot.T
    o = o * lax.rsqrt(jnp.mean(o * o, axis=-1, keepdims=True) + EPS) * sw_ref[...] * (1.0 - lam_init)
    o_ref[...] = (o * _silu(g_ref[...])).astype(o_ref.dtype)


def _attention(q, k, v, g, lq1, lk1, lq2, lk2, sw, lam_init, batch, seq, tq):
    t = q.shape[0]
    tk = tq // 2
    nq = seq // tq
    qmap = lambda b, h, i: (b * nq + i, h)
    kvmap = lambda b, h, i: (b, h)
    cmap = lambda b, h, i: (0, 0)
    return pl.pallas_call(
        functools.partial(_attn_kernel, lam_init=lam_init, tq=tq, tk=tk),
        grid=(batch, ATT_H, nq),
        in_specs=[pl.BlockSpec((tq, ATT_DV), qmap), pl.BlockSpec((seq, ATT_DV), kvmap),
                  pl.BlockSpec((seq, ATT_DV), kvmap), pl.BlockSpec((tq, ATT_DV), qmap),
                  pl.BlockSpec((1, ATT_DH), cmap), pl.BlockSpec((1, ATT_DH), cmap),
                  pl.BlockSpec((1, ATT_DH), cmap), pl.BlockSpec((1, ATT_DH), cmap),
                  pl.BlockSpec((1, ATT_DV), cmap)],
        out_specs=pl.BlockSpec((tq, ATT_DV), qmap),
        out_shape=jax.ShapeDtypeStruct((t, ATT_W), BF16),
        scratch_shapes=[pltpu.VMEM((seq // tk, ATT_VA, tk), BF16), pltpu.VMEM((2 * tq, ATT_DV), BF16),
                        pltpu.VMEM((ATT_VA, 2 * tq), F32),
                        pltpu.VMEM((tk, 2 * tq), F32), pltpu.VMEM((tk, 2 * tq), F32)],
        compiler_params=_cparams(("parallel", "parallel", "arbitrary")),
        name="diffattn",
    )(q, k, v, g, lq1, lk1, lq2, lk2, sw)


def _head_ones(dtype):
    r = lax.broadcasted_iota(jnp.int32, (HGRN_W, HGRN_W), 0) // HGRN_HD
    c = lax.broadcasted_iota(jnp.int32, (HGRN_W, HGRN_W), 1) // HGRN_HD
    return (r == c).astype(dtype)


def _hgrn_kernel(h_ref, lbl_ref, nw_ref, o_ref, st_ref, q_s, v_s, c_s, cs_s, dec_s, oo_s, qt_s, kh_s, vt_s,
                 w_s, sc_s, *, layer, ts):
    nchunk = ts // CHUNK
    ngroup = ts // HGRN_GR
    npair = HGRN_W // HGRN_PW
    half = CHUNK // 2

    @pl.when(pl.program_id(1) == 0)
    def _():
        st_ref[...] = jnp.zeros(st_ref.shape, F32)

    lg = lbl_ref[...]
    e = jnp.exp(lg - jnp.max(lg, axis=0, keepdims=True))
    p = e / jnp.sum(e, axis=0, keepdims=True)
    lb = jnp.sum(p[0:layer + 1, :], axis=0, keepdims=True) - p[0:1, :]
    log2e = math.log2(math.e)
    log_lb = jnp.log(jnp.maximum(lb, LB_FLOOR)) * log2e
    log1m_lb = jnp.log1p(-lb) * log2e
    r16 = lax.broadcasted_iota(jnp.int32, (HGRN_GR, HGRN_W), 0) % CHUNK
    cpg = HGRN_GR // CHUNK
    by_chunk = lambda a: a.reshape(cpg, CHUNK, HGRN_W)

    def prepare(g, carry):
        rows = pl.ds(pl.multiple_of(g * HGRN_GR, HGRN_GR), HGRN_GR)
        x2 = h_ref[rows, HGRN_W:2 * HGRN_W] * log2e
        hi = h_ref[rows, 2 * HGRN_W:3 * HGRN_W]
        b = log1m_lb - (jnp.maximum(-x2, 0.0) + jnp.log2(1.0 + jnp.exp2(-jnp.abs(x2))))
        log_f = jnp.maximum(log_lb, b) + jnp.log2(1.0 + jnp.exp2(-jnp.abs(log_lb - b)))
        k = 1.0 - jnp.exp2(log_f)
        q = _silu(h_ref[rows, 0:HGRN_W])
        cum = log_f
        sh = 1
        while sh < CHUNK:
            cum = cum + jnp.where(r16 >= sh, pltpu.roll(cum, sh, axis=0), 0.0)
            sh *= 2
        cum = by_chunk(cum)
        last = jnp.broadcast_to(cum[:, CHUNK - 1:CHUNK, :], cum.shape)
        rem = last - cum
        k, q = by_chunk(k), by_chunk(q)
        chunks = pl.ds(pl.multiple_of(g * cpg, cpg), cpg)
        q_s[chunks] = q
        v_s[chunks] = by_chunk(hi)
        c_s[chunks] = cum
        cs_s[chunks] = cum - jnp.log2(k)
        qt_s[chunks] = (q * jnp.exp2(cum)).astype(BF16)
        dec_s[chunks] = jnp.exp2(last)
        kh_s[g] = (k * jnp.exp2(rem)).reshape(HGRN_GR, HGRN_W).astype(BF16)
        vt = hi.T.astype(BF16)
        for pr in range(npair):
            vt_s[pr, g] = vt[pr * HGRN_PW:(pr + 1) * HGRN_PW, :]
        return carry

    lax.fori_loop(0, ngroup, prepare, 0)

    t8 = lax.broadcasted_iota(jnp.int32, (half, HGRN_W), 0)

    def weights(n, carry):
        c0, c1 = c_s[n, 0:half, :], c_s[n, half:, :]
        q0, q1 = q_s[n, 0:half, :], q_s[n, half:, :]
        tiles = []
        for s in range(CHUNK):
            bs = jnp.broadcast_to(cs_s[n, s:s + 1, :], (half, HGRN_W))
            if s < half:
                d0 = jnp.exp2(c0 - bs)
                tiles.append(q0 * (jnp.where(t8 >= s, d0, 0.0) if s else d0))
                tiles.append(q1 * jnp.exp2(c1 - bs))
            else:
                d1 = jnp.exp2(c1 - bs)
                tiles.append(q1 * (jnp.where(t8 >= s - half, d1, 0.0) if s > half else d1))
        w_s[n] = jnp.concatenate(tiles, axis=0).astype(BF16)
        return carry

    lax.fori_loop(0, nchunk, weights, 0)
    sc_s[...] = _dot(w_s[...].reshape(nchunk * HGRN_WR, HGRN_W), _head_ones(BF16)).reshape(nchunk, HGRN_WR, HGRN_W)

    lane_chunk = lax.broadcasted_iota(jnp.int32, (HGRN_PW, HGRN_GR), 1) // CHUNK
    same_head = (lax.broadcasted_iota(jnp.int32, (HGRN_PW, HGRN_PW), 0) // HGRN_HD
                 == lax.broadcasted_iota(jnp.int32, (HGRN_PW, HGRN_PW), 1) // HGRN_HD)
    nt = (((1,), (1,)), ((), ()))

    def group(g, carry):
        state = list(carry)
        pcols = [slice(pr * HGRN_PW, (pr + 1) * HGRN_PW) for pr in range(npair)]
        upd = []
        for j in range(cpg):
            row = []
            for pr in range(npair):
                vt = vt_s[pr, g]
                lhs = jnp.where(lane_chunk == j, vt, jnp.zeros_like(vt))
                row.append(jnp.where(same_head, _dot(lhs, kh_s[g, :, pcols[pr]]), 0.0))
            upd.append(row)
        entering = []
        for j in range(cpg):
            n = g * cpg + j
            entering.append([st.astype(BF16) for st in state])
            state = [state[pr] * dec_s[n, 0:1, pcols[pr]] + upd[j][pr] for pr in range(npair)]
        for j in range(cpg):
            n = g * cpg + j
            o0 = jnp.zeros((half, HGRN_W), F32)
            o1 = jnp.zeros((half, HGRN_W), F32)
            for s in range(CHUNK):
                vb = jnp.broadcast_to(v_s[n, s:s + 1, :], (half, HGRN_W))
                if s < half:
                    o0 = o0 + sc_s[n, s * CHUNK:s * CHUNK + half, :] * vb
                    o1 = o1 + sc_s[n, s * CHUNK + half:(s + 1) * CHUNK, :] * vb
                else:
                    o1 = o1 + sc_s[n, half * CHUNK + (s - half) * half:half * CHUNK + (s - half + 1) * half, :] * vb
            oi = jnp.concatenate([lax.dot_general(qt_s[n, :, pcols[pr]], entering[j][pr], nt,
                                                  preferred_element_type=F32) for pr in range(npair)], axis=1)
            oo_s[n, 0:half, :] = o0 + oi[0:half]
            oo_s[n, half:, :] = o1 + oi[half:]
        return tuple(state)

    state = lax.fori_loop(0, ngroup, group, tuple(st_ref[pr] for pr in range(npair)))
    for pr in range(npair):
        st_ref[pr] = state[pr]

    def finish(g, carry):
        rows = pl.ds(pl.multiple_of(g * HGRN_GR, HGRN_GR), HGRN_GR)
        o = oo_s[pl.ds(pl.multiple_of(g * cpg, cpg), cpg)].reshape(HGRN_GR, HGRN_W)
        ms = _dot_split(o * o, _head_ones(F32)) * (1.0 / HGRN_HD)
        hg = h_ref[rows, 3 * HGRN_W:4 * HGRN_W]
        o_ref[rows, :] = (o * lax.rsqrt(ms + EPS) * nw_ref[...] * _silu(hg)).astype(o_ref.dtype)
        return carry

    lax.fori_loop(0, ngroup, finish, 0)


def _hgrn(hproj, lb_logits, nw_tiled, layer, batch, seq, ts):
    t = hproj.shape[0]
    ns = seq // ts
    depth = lb_logits.shape[0]
    nchunk = ts // CHUNK
    npair = HGRN_W // HGRN_PW
    blk = pltpu.VMEM((nchunk, CHUNK, HGRN_W), F32)
    return pl.pallas_call(
        functools.partial(_hgrn_kernel, layer=layer, ts=ts),
        grid=(batch, ns),
        in_specs=[pl.BlockSpec((ts, 4 * HGRN_W), lambda b, i: (b * ns + i, 0)),
                  pl.BlockSpec((depth, HGRN_W), lambda b, i: (0, 0)),
                  pl.BlockSpec((1, HGRN_W), lambda b, i: (0, 0))],
        out_specs=pl.BlockSpec((ts, HGRN_W), lambda b, i: (b * ns + i, 0)),
        out_shape=jax.ShapeDtypeStruct((t, HGRN_W), BF16),
        scratch_shapes=[pltpu.VMEM((npair, HGRN_PW, HGRN_PW), F32),
                        blk, blk, blk, blk, blk, blk,
                        pltpu.VMEM((nchunk, CHUNK, HGRN_W), BF16),
                        pltpu.VMEM((ts // HGRN_GR, HGRN_GR, HGRN_W), BF16),
                        pltpu.VMEM((npair, ts // HGRN_GR, HGRN_PW, HGRN_GR), BF16),
                        pltpu.VMEM((nchunk, HGRN_WR, HGRN_W), BF16), pltpu.VMEM((nchunk, HGRN_WR, HGRN_W), F32)],
        compiler_params=_cparams(("parallel", "arbitrary")),
        name="hgrn2",
    )(hproj, lb_logits, nw_tiled)


def _s5_param_kernel(prow_ref, pcol_ref, bt_ref, btile_ref, ctile_ref,
                     m_ref, gr_ref, gi_ref, hr_ref, hi_ref, d_ref):
    are, aim = prow_ref[0:1, :], prow_ref[1:2, :]
    dt = jnp.exp(prow_ref[2:3, :])
    lr, li = dt * are, dt * aim
    mag = jnp.exp(lr)
    abr, abi = mag * jnp.cos(li), mag * jnp.sin(li)
    den = are * are + aim * aim
    nr, ni = abr - 1.0, abi
    zr = (nr * are + ni * aim) / den
    zi = (ni * are - nr * aim) / den

    arec, aimc = pcol_ref[:, 0:1], pcol_ref[:, 1:2]
    dtc = jnp.exp(pcol_ref[:, 2:3])
    lrc, lic = dtc * arec, dtc * aimc
    cr, ci = ctile_ref[0], ctile_ref[1]
    tl = (lax.broadcasted_iota(jnp.int32, (S5_P, S5_L * S5_C), 1) // S5_C).astype(F32)

    def powers(n):
        mg = jnp.exp(n * lrc)
        return mg * jnp.cos(n * lic), mg * jnp.sin(n * lic)

    er, ei = powers(tl)
    w1 = er * cr - ei * ci
    w2 = -(ei * cr + er * ci)
    bbr = zr * bt_ref[0] - zi * bt_ref[1]
    bbi = zr * bt_ref[1] + zi * bt_ref[0]
    r0 = _dot_split(bbr, w1) + _dot_split(bbi, w2)
    lane = lax.broadcasted_iota(jnp.int32, (S5_C, S5_L * S5_C), 1)
    m_ref[0:S5_C, :] = r0.astype(m_ref.dtype)
    for s in range(1, S5_L):
        blk = jnp.where(lane >= s * S5_C, pltpu.roll(r0, s * S5_C, axis=1), 0.0)
        m_ref[s * S5_C:(s + 1) * S5_C, :] = blk.astype(m_ref.dtype)

    e = (S5_L - 1 - lax.broadcasted_iota(jnp.int32, (S5_L * S5_C, S5_P), 0) // S5_C).astype(F32)
    mg = jnp.exp(e * lr)
    fr, fi = mg * jnp.cos(e * li), mg * jnp.sin(e * li)
    btr = zr * btile_ref[0] - zi * btile_ref[1]
    bti = zr * btile_ref[1] + zi * btile_ref[0]
    gr_ref[...] = (fr * btr - fi * bti).astype(gr_ref.dtype)
    gi_ref[...] = (fr * bti + fi * btr).astype(gi_ref.dtype)

    pr, pi = powers(tl + 1.0)
    hr_ref[...] = (cr * pr - ci * pi).astype(hr_ref.dtype)
    hi_ref[...] = (-(cr * pi + ci * pr)).astype(hi_ref.dtype)

    mg = jnp.exp(float(S5_L) * lr)
    d_ref[0:1, :] = mg * jnp.cos(float(S5_L) * li)
    d_ref[1:2, :] = mg * jnp.sin(float(S5_L) * li)


def _s5_params(prow, pcol, bt, btile, ctile):
    g = prow.shape[0]
    wide = S5_L * S5_C
    m3 = lambda i: (i, 0, 0)
    m4 = lambda i: (i, 0, 0, 0)
    return pl.pallas_call(
        _s5_param_kernel,
        grid=(g,),
        in_specs=[pl.BlockSpec((None, 3, S5_P), m3), pl.BlockSpec((None, S5_P, 3), m3),
                  pl.BlockSpec((None, 2, S5_C, S5_P), m4), pl.BlockSpec((None, 2, wide, S5_P), m4),
                  pl.BlockSpec((None, 2, S5_P, wide), m4)],
        out_specs=[pl.BlockSpec((None, wide, wide), m3), pl.BlockSpec((None, wide, S5_P), m3),
                   pl.BlockSpec((None, wide, S5_P), m3), pl.BlockSpec((None, S5_P, wide), m3),
                   pl.BlockSpec((None, S5_P, wide), m3), pl.BlockSpec((None, 2, S5_P), m3)],
        out_shape=[jax.ShapeDtypeStruct((g, wide, wide), BF16), jax.ShapeDtypeStruct((g, wide, S5_P), BF16),
                   jax.ShapeDtypeStruct((g, wide, S5_P), BF16), jax.ShapeDtypeStruct((g, S5_P, wide), BF16),
                   jax.ShapeDtypeStruct((g, S5_P, wide), BF16), jax.ShapeDtypeStruct((g, 2, S5_P), F32)],
        compiler_params=_cparams(("parallel",)),
        name="s5_params",
    )(prow, pcol, bt, btile, ctile)


def _s5_scan_kernel(u_ref, m_ref, gr_ref, gi_ref, hr_ref, hi_ref, d_ref, y_ref, *, nchunk):
    u = u_ref[...]
    rows = u.shape[0]
    y = _dot(u, m_ref[...])
    xr = _dot(u, gr_ref[...])
    xi = _dot(u, gi_ref[...])
    rn = lax.broadcasted_iota(jnp.int32, (rows, S5_P), 0) % nchunk
    pr, pi = d_ref[0:1, :], d_ref[1:2, :]
    sh = 1
    while sh < nchunk:
        keep = rn >= sh
        sr = jnp.where(keep, pltpu.roll(xr, sh, axis=0), 0.0)
        si = jnp.where(keep, pltpu.roll(xi, sh, axis=0), 0.0)
        xr, xi = xr + pr * sr - pi * si, xi + pr * si + pi * sr
        pr, pi = pr * pr - pi * pi, 2.0 * pr * pi
        sh *= 2
    first = rn >= 1
    xr = jnp.where(first, pltpu.roll(xr, 1, axis=0), 0.0)
    xi = jnp.where(first, pltpu.roll(xi, 1, axis=0), 0.0)
    y = y + _dot(xr.astype(BF16), hr_ref[...]) + _dot(xi.astype(BF16), hi_ref[...])
    y_ref[...] = y


def _s5_scan(ug, m, gr, gi, hr, hi, d, nchunk):
    g, rows, wide = ug.shape
    m3 = lambda i: (i, 0, 0)
    return pl.pallas_call(
        functools.partial(_s5_scan_kernel, nchunk=nchunk),
        grid=(g,),
        in_specs=[pl.BlockSpec((None, rows, wide), m3), pl.BlockSpec((None, wide, wide), m3),
                  pl.BlockSpec((None, wide, S5_P), m3), pl.BlockSpec((None, wide, S5_P), m3),
                  pl.BlockSpec((None, S5_P, wide), m3), pl.BlockSpec((None, S5_P, wide), m3),
                  pl.BlockSpec((None, 2, S5_P), m3)],
        out_specs=pl.BlockSpec((None, rows, wide), m3),
        out_shape=jax.ShapeDtypeStruct((g, rows, wide), F32),
        compiler_params=_cparams(("parallel",)),
        name="s5_scan",
    )(ug, m, gr, gi, hr, hi, d)


def _s5_post_kernel(y_ref, s_ref, d_ref, w_ref, b_ref, o_ref):
    u = s_ref[:, 0:S5_W]
    sg = s_ref[:, S5_W:2 * S5_W]
    y = y_ref[...] + d_ref[...] * u
    cdf = 0.5 * (1.0 + jnp.tanh(math.sqrt(2.0 / math.pi) * (y + 0.044715 * (y * y * y))))
    y = y * cdf
    z = _dot(y.astype(BF16), w_ref[...]) + b_ref[...]
    o_ref[...] = (y * _sigmoid(z) * _silu(sg)).astype(o_ref.dtype)


def _s5_post(y, sproj, d, w_bf, b, tm):
    t = y.shape[0]
    row = lambda i: (i, 0)
    cst = lambda i: (0, 0)
    return pl.pallas_call(
        _s5_post_kernel,
        grid=(t // tm,),
        in_specs=[pl.BlockSpec((tm, S5_W), row), pl.BlockSpec((tm, 2 * S5_W), row),
                  pl.BlockSpec((1, S5_W), cst), pl.BlockSpec((S5_W, S5_W), cst), pl.BlockSpec((1, S5_W), cst)],
        out_specs=pl.BlockSpec((tm, S5_W), row),
        out_shape=jax.ShapeDtypeStruct((t, S5_W), BF16),
        compiler_params=_cparams(("parallel",)),
        name="s5_post",
    )(y, sproj, d, w_bf, b)


def _s5(sproj, a_re, a_im, b_re, b_im, c_re, c_im, log_dt, d_skip, glu_w, glu_b, batch, seq, tm):
    t = sproj.shape[0]
    nchunk = seq // S5_L
    ldt = jnp.broadcast_to(log_dt[:, None], a_re.shape)
    prow = jnp.stack([a_re, a_im, ldt], axis=1)
    pcol = jnp.stack([a_re, a_im, ldt], axis=2)
    bt = jnp.stack([b_re, b_im], axis=1).transpose(0, 1, 3, 2)
    btile = jnp.tile(bt, (1, 1, S5_L, 1))
    ctile = jnp.tile(jnp.stack([c_re, c_im], axis=1).transpose(0, 1, 3, 2), (1, 1, 1, S5_L))
    m, gr, gi, hr, hi, d = _s5_params(prow, pcol, bt, btile, ctile)
    u = sproj[:, 0:S5_W].astype(BF16)
    ug = u.reshape(batch * nchunk, S5_L, S5_G, S5_C).transpose(2, 0, 1, 3).reshape(S5_G, batch * nchunk, S5_L * S5_C)
    yg = _s5_scan(ug, m, gr, gi, hr, hi, d, nchunk)
    y = yg.reshape(S5_G, batch * nchunk, S5_L, S5_C).transpose(1, 2, 0, 3).reshape(t, S5_W)
    return _s5_post(y, sproj, d_skip[None, :], glu_w.astype(BF16), glu_b[None, :], tm)


def _pick(n, pref):
    b = min(n, pref)
    while n % b:
        b //= 2
    return b


def kernel(x, norm_w, w_in, w_out, hgrn_lb_logits, hgrn_norm_w, s5_a_re, s5_a_im, s5_b_re, s5_b_im, s5_c_re, s5_c_im, s5_log_dt, s5_d, s5_glu_w, s5_glu_b, diff_lq1, diff_lk1, diff_lq2, diff_lk2, diff_subln_w, final_norm_w):
    batch, seq, _ = x.shape
    depth = norm_w.shape[0]
    t = batch * seq
    tm = _pick(seq, 512)
    tq = _pick(seq, 1024)
    ts = _pick(seq, 512)
    rope = _rope_tables(seq)
    h_res = x.astype(F32).reshape(t, D_MODEL)
    for l in range(depth):
        hp, sp, aq, ak, av, ag = _inproj(h_res, norm_w[l][None, :].astype(F32), w_in[l].astype(BF16), rope, seq, tm)
        mix_h = _hgrn(hp, hgrn_lb_logits.astype(F32), jnp.tile(hgrn_norm_w[l].astype(F32), HGRN_W // HGRN_HD)[None, :],
                      l, batch, seq, ts)
        mix_s = _s5(sp, s5_a_re[l], s5_a_im[l], s5_b_re[l], s5_b_im[l], s5_c_re[l], s5_c_im[l], s5_log_dt[l],
                    s5_d[l], s5_glu_w[l], s5_glu_b[l], batch, seq, tm)
        lam_init = 0.8 - 0.6 * math.exp(-0.3 * l)
        mix_a = _attention(aq, ak, av, ag, diff_lq1[l][None, :], diff_lk1[l][None, :], diff_lq2[l][None, :],
                           diff_lk2[l][None, :], diff_subln_w[l][None, :], lam_init, batch, seq, tq)
        h_res = _outproj(h_res, mix_h, mix_s, mix_a, w_out[l].astype(BF16), final_norm_w[None, :].astype(F32),
                         l == depth - 1, tm)
    return h_res.reshape(batch, seq, D_MODEL).astype(x.dtype)
```

```python
import functools
import math

import jax
import jax.numpy as jnp
from jax import lax
from jax.experimental import pallas as pl
from jax.experimental.pallas import tpu as pltpu

F32 = jnp.float32
BF16 = jnp.bfloat16

D_MODEL = 1024
HGRN_W = 256
HGRN_HD = 64
CHUNK = 16
HGRN_WR = 8 * 16 + 8 * 8
HGRN_PW = 2 * HGRN_HD
HGRN_GR = 128
LB_FLOOR = 1e-30
S5_W = 256
S5_C = 16
S5_G = S5_W // S5_C
S5_P = 64
S5_L = 16
S5_ST = 2 * S5_P
S5_SW = S5_G * S5_ST
S5_CW = S5_L * S5_W
S5_HW = 128
ATT_W = 512
ATT_DH = 64
ATT_DV = 128
ATT_H = ATT_W // ATT_DV
ATT_CT = 256
ATT_VA = ATT_DV + 16
ROPE_DIM = 16
ROPE_THETA = 500000.0
MASK_VALUE = -1e30
Q_SCALE = ATT_DH ** -0.5 * math.log2(math.e)
EPS = 1e-6
D_IN = 4 * HGRN_W + 2 * S5_W + 4 * ATT_W
COL_S5 = 4 * HGRN_W
COL_AQ = COL_S5 + 2 * S5_W
COL_AK = COL_AQ + ATT_W
COL_AV = COL_AK + ATT_W
COL_AG = COL_AV + ATT_W

VMEM_LIMIT = 56 * 1024 * 1024


def _cparams(sem):
    return pltpu.CompilerParams(dimension_semantics=sem, vmem_limit_bytes=VMEM_LIMIT)


def _sigmoid(x):
    return 1.0 / (1.0 + jnp.exp(-x))


def _softplus_neg_abs(x):
    return jnp.log(1.0 + jnp.exp(-jnp.abs(x)))


def _silu(x):
    return x * _sigmoid(x)


def _dot(a, b):
    return jnp.dot(a, b, preferred_element_type=F32)


def _dot_split(a, b):
    ah = a.astype(BF16)
    al = (a - ah.astype(F32)).astype(BF16)
    bh = b.astype(BF16)
    bl = (b - bh.astype(F32)).astype(BF16)
    return _dot(ah, bh) + _dot(ah, bl) + _dot(al, bh)


def _rope(t, rc, ra, rb):
    return t * rc + pltpu.roll(t, 128 - ROPE_DIM // 2, axis=1) * ra + pltpu.roll(t, ROPE_DIM // 2, axis=1) * rb


def _inproj_kernel(x_ref, nw_ref, w_ref, rope_ref, h_ref, s_ref, q_ref, k_ref, v_ref, g_ref):
    x = x_ref[...]
    hn = x * lax.rsqrt(jnp.mean(x * x, axis=-1, keepdims=True) + EPS) * nw_ref[...]
    proj = _dot(hn.astype(BF16), w_ref[...])
    h_ref[...] = proj[:, :COL_S5]
    s_ref[...] = proj[:, COL_S5:COL_AQ]
    rc = rope_ref[:, 0:128]
    ra = rope_ref[:, 128:256]
    rb = rope_ref[:, 256:384]
    for h in range(ATT_H):
        lo = h * ATT_DV
        q = proj[:, COL_AQ + lo:COL_AQ + lo + ATT_DV]
        k = proj[:, COL_AK + lo:COL_AK + lo + ATT_DV]
        q_ref[:, lo:lo + ATT_DV] = (_rope(q, rc, ra, rb) * Q_SCALE).astype(BF16)
        k_ref[:, lo:lo + ATT_DV] = _rope(k, rc, ra, rb).astype(BF16)
    v_ref[...] = proj[:, COL_AV:COL_AG].astype(BF16)
    g_ref[...] = proj[:, COL_AG:]


def _rope_tables(seq):
    pos = jnp.arange(seq, dtype=F32)
    inv_freq = ROPE_THETA ** (-jnp.arange(0, ROPE_DIM, 2, dtype=F32) / ROPE_DIM)
    ang = pos[:, None] * inv_freq[None, :]
    cos, sin = jnp.cos(ang), jnp.sin(ang)
    half = ROPE_DIM // 2
    zeros = jnp.zeros((seq, ATT_DH - ROPE_DIM), F32)
    zh = jnp.zeros((seq, half), F32)
    rc = jnp.concatenate([cos, cos, jnp.ones((seq, ATT_DH - ROPE_DIM), F32)], axis=1)
    ra = jnp.concatenate([-sin, zh, zeros], axis=1)
    rb = jnp.concatenate([zh, sin, zeros], axis=1)
    return jnp.concatenate([jnp.tile(t, (1, 2)) for t in (rc, ra, rb)], axis=1)


def _inproj(x2, norm_w, w_bf, rope, seq, tm):
    t = x2.shape[0]
    nblk = seq // tm
    row = lambda i: (i, 0)
    return pl.pallas_call(
        _inproj_kernel,
        grid=(t // tm,),
        in_specs=[pl.BlockSpec((tm, D_MODEL), row),
                  pl.BlockSpec((1, D_MODEL), lambda i: (0, 0)),
                  pl.BlockSpec((D_MODEL, D_IN), lambda i: (0, 0)),
                  pl.BlockSpec((tm, 384), lambda i: (i % nblk, 0))],
        out_specs=[pl.BlockSpec((tm, COL_S5), row), pl.BlockSpec((tm, 2 * S5_W), row),
                   pl.BlockSpec((tm, ATT_W), row), pl.BlockSpec((tm, ATT_W), row),
                   pl.BlockSpec((tm, ATT_W), row), pl.BlockSpec((tm, ATT_W), row)],
        out_shape=[jax.ShapeDtypeStruct((t, COL_S5), F32), jax.ShapeDtypeStruct((t, 2 * S5_W), F32),
                   jax.ShapeDtypeStruct((t, ATT_W), BF16), jax.ShapeDtypeStruct((t, ATT_W), BF16),
                   jax.ShapeDtypeStruct((t, ATT_W), BF16), jax.ShapeDtypeStruct((t, ATT_W), F32)],
        compiler_params=_cparams(("parallel",)),
        name="inproj",
    )(x2, norm_w, w_bf, rope)


def _outproj_kernel(res_ref, mh_ref, ms0_ref, ms1_ref, ma_ref, w_ref, fw_ref, o_ref, *, final):
    acc = res_ref[...]
    acc = acc + _dot(mh_ref[...].astype(BF16), w_ref[0:HGRN_W, :])
    acc = acc + _dot(ms0_ref[...].astype(BF16), w_ref[HGRN_W:HGRN_W + S5_HW, :])
    acc = acc + _dot(ms1_ref[...].astype(BF16), w_ref[HGRN_W + S5_HW:HGRN_W + S5_W, :])
    acc = acc + _dot(ma_ref[...].astype(BF16), w_ref[HGRN_W + S5_W:, :])
    if final:
        acc = acc * lax.rsqrt(jnp.mean(acc * acc, axis=-1, keepdims=True) + EPS) * fw_ref[...]
    o_ref[...] = acc


def _outproj(res, mh, ms0, ms1, ma, w_bf, fw, final, tm):
    t = res.shape[0]
    row = lambda i: (i, 0)
    return pl.pallas_call(
        functools.partial(_outproj_kernel, final=final),
        grid=(t // tm,),
        in_specs=[pl.BlockSpec((tm, D_MODEL), row), pl.BlockSpec((tm, HGRN_W), row),
                  pl.BlockSpec((tm, S5_HW), row), pl.BlockSpec((tm, S5_HW), row), pl.BlockSpec((tm, ATT_W), row),
                  pl.BlockSpec((D_MODEL, D_MODEL), lambda i: (0, 0)),
                  pl.BlockSpec((1, D_MODEL), lambda i: (0, 0))],
        out_specs=pl.BlockSpec((tm, D_MODEL), row),
        out_shape=jax.ShapeDtypeStruct((t, D_MODEL), F32),
        compiler_params=_cparams(("parallel",)),
        name="outproj",
    )(res, mh, ms0, ms1, ma, w_bf, fw)


def _attn_kernel(q_ref, k_ref, v_ref, g_ref, lq1_ref, lk1_ref, lq2_ref, lk2_ref, sw_ref, o_ref,
                 vt_s, qq_s, acc_s, s0_s, s1_s, *, lam_init, tq, tk):
    i = pl.program_id(2)
    nkb = vt_s.shape[0]
    ntile = 2 * tq // ATT_CT
    per_map = tq // ATT_CT
    nt = (((1,), (1,)), ((), ()))

    @pl.when(i == 0)
    def _():
        for j in range(nkb):
            vt_s[j, 0:ATT_DV, :] = v_ref[j * tk:(j + 1) * tk, :].T
            vt_s[j, ATT_DV:, :] = jnp.ones((ATT_VA - ATT_DV, tk), BF16)

    q = q_ref[...]
    lane = lax.broadcasted_iota(jnp.int32, (tq, ATT_DV), 1)
    qq_s[0:tq, :] = jnp.where(lane < ATT_DH, q, jnp.zeros_like(q))
    qq_s[tq:2 * tq, :] = jnp.where(lane >= ATT_DH, q, jnp.zeros_like(q))
    acc_s[...] = jnp.zeros(acc_s.shape, F32)

    def tile_mode(c, koff):
        qs = (c % per_map) * ATT_CT
        if koff is None or qs >= koff + tk - 1:
            return "full"
        return "skip" if qs + ATT_CT - 1 < koff else "causal"

    def stage(kq, sq_s, koff_q, ssm_s, ksm, koff_sm, m_run, cmax):
        new_m, new_cmax = list(m_run), [None] * ntile
        if kq is not None:
            kb = k_ref[pl.ds(kq * tk if isinstance(kq, int) else pl.multiple_of(kq * tk, tk), tk), :]
        if ksm is not None:
            vtb = vt_s[ksm]
        for c in range(ntile):
            cols = slice(c * ATT_CT, (c + 1) * ATT_CT)
            if kq is not None and tile_mode(c, koff_q) != "skip":
                s = lax.dot_general(kb, qq_s[cols, :], nt, preferred_element_type=F32)
                if tile_mode(c, koff_q) == "causal":
                    kpos = lax.broadcasted_iota(jnp.int32, (tk, ATT_CT), 0) + koff_q
                    qpos = lax.broadcasted_iota(jnp.int32, (tk, ATT_CT), 1) + (c % per_map) * ATT_CT
                    s = jnp.where(kpos <= qpos, s, MASK_VALUE)
                sq_s[:, cols] = s
                new_cmax[c] = jnp.max(s, axis=0, keepdims=True)
            if ksm is not None and tile_mode(c, koff_sm) != "skip":
                m_new = jnp.maximum(m_run[c], cmax[c])
                alpha = jnp.exp2(m_run[c] - m_new)
                p = jnp.exp2(ssm_s[:, cols] - m_new)
                acc_s[:, cols] = alpha * acc_s[:, cols] + _dot(vtb, p.astype(BF16))
                new_m[c] = m_new
        return tuple(new_m), tuple(new_cmax)

    neg = tuple(jnp.full((1, ATT_CT), MASK_VALUE, F32) for _ in range(ntile))
    d0 = 2 * i
    _, cm = stage(d0, s0_s, 0, None, None, None, neg, None)
    m_run, cm1 = stage(d0 + 1, s1_s, tk, s0_s, d0, 0, neg, cm)
    cm1 = tuple(neg[c] if x is None else x for c, x in enumerate(cm1))
    m_run, cm0 = stage(0, s0_s, None, s1_s, d0 + 1, tk, m_run, cm1)

    def pair(t, carry):
        m_run, cm0 = carry
        m_run, cm1 = stage(2 * t + 1, s1_s, None, s0_s, 2 * t, None, m_run, cm0)
        m_run, cm0 = stage(jnp.minimum(2 * t + 2, d0 - 1), s0_s, None, s1_s, 2 * t + 1, None, m_run, cm1)
        return m_run, cm0

    lax.fori_loop(0, i, pair, (m_run, cm0))

    lam = (jnp.exp(jnp.sum(lq1_ref[...] * lk1_ref[...], axis=-1, keepdims=True))
           - jnp.exp(jnp.sum(lq2_ref[...] * lk2_ref[...], axis=-1, keepdims=True)) + lam_init)
    on = acc_s[0:ATT_DV, :] / acc_s[ATT_DV:ATT_DV + 1, :]
    ot = on[:, 0:tq] - lam * on[:, tq:2 * tq]
    o = ot.T
    o = o * lax.rsqrt(jnp.mean(o * o, axis=-1, keepdims=True) + EPS) * sw_ref[...] * (1.0 - lam_init)
    o_ref[...] = (o * _silu(g_ref[...])).astype(o_ref.dtype)


def _attention(q, k, v, g, lq1, lk1, lq2, lk2, sw, lam_init, batch, seq, tq):
    t = q.shape[0]
    tk = tq // 2
    nq = seq // tq
    qmap = lambda b, h, i: (b * nq + i, h)
    kvmap = lambda b, h, i: (b, h)
    cmap = lambda b, h, i: (0, 0)
    return pl.pallas_call(
        functools.partial(_attn_kernel, lam_init=lam_init, tq=tq, tk=tk),
        grid=(batch, ATT_H, nq),
        in_specs=[pl.BlockSpec((tq, ATT_DV), qmap), pl.BlockSpec((seq, ATT_DV), kvmap),
                  pl.BlockSpec((seq, ATT_DV), kvmap), pl.BlockSpec((tq, ATT_DV), qmap),
                  pl.BlockSpec((1, ATT_DH), cmap), pl.BlockSpec((1, ATT_DH), cmap),
                  pl.BlockSpec((1, ATT_DH), cmap), pl.BlockSpec((1, ATT_DH), cmap),
                  pl.BlockSpec((1, ATT_DV), cmap)],
        out_specs=pl.BlockSpec((tq, ATT_DV), qmap),
        out_shape=jax.ShapeDtypeStruct((t, ATT_W), BF16),
        scratch_shapes=[pltpu.VMEM((seq // tk, ATT_VA, tk), BF16), pltpu.VMEM((2 * tq, ATT_DV), BF16),
                        pltpu.VMEM((ATT_VA, 2 * tq), F32),
                        pltpu.VMEM((tk, 2 * tq), F32), pltpu.VMEM((tk, 2 * tq), F32)],
        compiler_params=_cparams(("parallel", "parallel", "arbitrary")),
        name="diffattn",
    )(q, k, v, g, lq1, lk1, lq2, lk2, sw)


def _head_ones(dtype):
    r = lax.broadcasted_iota(jnp.int32, (HGRN_W, HGRN_W), 0) // HGRN_HD
    c = lax.broadcasted_iota(jnp.int32, (HGRN_W, HGRN_W), 1) // HGRN_HD
    return (r == c).astype(dtype)


def _hgrn_kernel(h_ref, lbl_ref, nw_ref, o_ref, st_ref, q_s, v_s, c_s, cs_s, dec_s, oo_s, qt_s, kh_s, vt_s,
                 w_s, sc_s, *, layer, ts):
    nchunk = ts // CHUNK
    ngroup = ts // HGRN_GR
    npair = HGRN_W // HGRN_PW
    half = CHUNK // 2

    @pl.when(pl.program_id(1) == 0)
    def _():
        st_ref[...] = jnp.zeros(st_ref.shape, F32)

    lg = lbl_ref[...]
    e = jnp.exp(lg - jnp.max(lg, axis=0, keepdims=True))
    p = e / jnp.sum(e, axis=0, keepdims=True)
    lb = jnp.sum(p[0:layer + 1, :], axis=0, keepdims=True) - p[0:1, :]
    log2e = math.log2(math.e)
    log_lb = jnp.log(jnp.maximum(lb, LB_FLOOR)) * log2e
    log1m_lb = jnp.log1p(-lb) * log2e
    r16 = lax.broadcasted_iota(jnp.int32, (HGRN_GR, HGRN_W), 0) % CHUNK
    cpg = HGRN_GR // CHUNK
    by_chunk = lambda a: a.reshape(cpg, CHUNK, HGRN_W)

    def prepare(g, carry):
        rows = pl.ds(pl.multiple_of(g * HGRN_GR, HGRN_GR), HGRN_GR)
        x2 = h_ref[rows, HGRN_W:2 * HGRN_W] * log2e
        hi = h_ref[rows, 2 * HGRN_W:3 * HGRN_W]
        b = log1m_lb - (jnp.maximum(-x2, 0.0) + jnp.log2(1.0 + jnp.exp2(-jnp.abs(x2))))
        log_f = jnp.maximum(log_lb, b) + jnp.log2(1.0 + jnp.exp2(-jnp.abs(log_lb - b)))
        k = 1.0 - jnp.exp2(log_f)
        q = _silu(h_ref[rows, 0:HGRN_W])
        cum = log_f
        sh = 1
        while sh < CHUNK:
            cum = cum + jnp.where(r16 >= sh, pltpu.roll(cum, sh, axis=0), 0.0)
            sh *= 2
        cum = by_chunk(cum)
        last = jnp.broadcast_to(cum[:, CHUNK - 1:CHUNK, :], cum.shape)
        rem = last - cum
        k, q = by_chunk(k), by_chunk(q)
        chunks = pl.ds(pl.multiple_of(g * cpg, cpg), cpg)
        q_s[chunks] = q
        v_s[chunks] = by_chunk(hi)
        c_s[chunks] = cum
        cs_s[chunks] = cum - jnp.log2(k)
        qt_s[chunks] = (q * jnp.exp2(cum)).astype(BF16)
        dec_s[chunks] = jnp.exp2(last)
        kh_s[g] = (k * jnp.exp2(rem)).reshape(HGRN_GR, HGRN_W).astype(BF16)
        vt = hi.T.astype(BF16)
        for pr in range(npair):
            vt_s[pr, g] = vt[pr * HGRN_PW:(pr + 1) * HGRN_PW, :]
        return carry

    lax.fori_loop(0, ngroup, prepare, 0)

    t8 = lax.broadcasted_iota(jnp.int32, (half, HGRN_W), 0)

    def weights(n, carry):
        c0, c1 = c_s[n, 0:half, :], c_s[n, half:, :]
        q0, q1 = q_s[n, 0:half, :], q_s[n, half:, :]
        tiles = []
        for s in range(CHUNK):
            bs = jnp.broadcast_to(cs_s[n, s:s + 1, :], (half, HGRN_W))
            if s < half:
                d0 = jnp.exp2(c0 - bs)
                tiles.append(q0 * (jnp.where(t8 >= s, d0, 0.0) if s else d0))
                tiles.append(q1 * jnp.exp2(c1 - bs))
            else:
                d1 = jnp.exp2(c1 - bs)
                tiles.append(q1 * (jnp.where(t8 >= s - half, d1, 0.0) if s > half else d1))
        w_s[n] = jnp.concatenate(tiles, axis=0).astype(BF16)
        return carry

    lax.fori_loop(0, nchunk, weights, 0)
    sc_s[...] = _dot(w_s[...].reshape(nchunk * HGRN_WR, HGRN_W), _head_ones(BF16)).reshape(nchunk, HGRN_WR, HGRN_W)

    lane_chunk = lax.broadcasted_iota(jnp.int32, (HGRN_PW, HGRN_GR), 1) // CHUNK
    same_head = (lax.broadcasted_iota(jnp.int32, (HGRN_PW, HGRN_PW), 0) // HGRN_HD
                 == lax.broadcasted_iota(jnp.int32, (HGRN_PW, HGRN_PW), 1) // HGRN_HD)
    nt = (((1,), (1,)), ((), ()))

    def group(g, carry):
        state = list(carry)
        pcols = [slice(pr * HGRN_PW, (pr + 1) * HGRN_PW) for pr in range(npair)]
        upd = []
        for j in range(cpg):
            row = []
            for pr in range(npair):
                vt = vt_s[pr, g]
                lhs = jnp.where(lane_chunk == j, vt, jnp.zeros_like(vt))
                row.append(jnp.where(same_head, _dot(lhs, kh_s[g, :, pcols[pr]]), 0.0))
            upd.append(row)
        entering = []
        for j in range(cpg):
            n = g * cpg + j
            entering.append([st.astype(BF16) for st in state])
            state = [state[pr] * dec_s[n, 0:1, pcols[pr]] + upd[j][pr] for pr in range(npair)]
        for j in range(cpg):
            n = g * cpg + j
            o0 = jnp.zeros((half, HGRN_W), F32)
            o1 = jnp.zeros((half, HGRN_W), F32)
            for s in range(CHUNK):
                vb = jnp.broadcast_to(v_s[n, s:s + 1, :], (half, HGRN_W))
                if s < half:
                    o0 = o0 + sc_s[n, s * CHUNK:s * CHUNK + half, :] * vb
                    o1 = o1 + sc_s[n, s * CHUNK + half:(s + 1) * CHUNK, :] * vb
                else:
                    o1 = o1 + sc_s[n, half * CHUNK + (s - half) * half:half * CHUNK + (s - half + 1) * half, :] * vb
            oi = jnp.concatenate([lax.dot_general(qt_s[n, :, pcols[pr]], entering[j][pr], nt,
                                                  preferred_element_type=F32) for pr in range(npair)], axis=1)
            oo_s[n, 0:half, :] = o0 + oi[0:half]
            oo_s[n, half:, :] = o1 + oi[half:]
        return tuple(state)

    state = lax.fori_loop(0, ngroup, group, tuple(st_ref[pr] for pr in range(npair)))
    for pr in range(npair):
        st_ref[pr] = state[pr]

    def finish(g, carry):
        rows = pl.ds(pl.multiple_of(g * HGRN_GR, HGRN_GR), HGRN_GR)
        o = oo_s[pl.ds(pl.multiple_of(g * cpg, cpg), cpg)].reshape(HGRN_GR, HGRN_W)
        ms = _dot_split(o * o, _head_ones(F32)) * (1.0 / HGRN_HD)
        hg = h_ref[rows, 3 * HGRN_W:4 * HGRN_W]
        o_ref[rows, :] = (o * lax.rsqrt(ms + EPS) * nw_ref[...] * _silu(hg)).astype(o_ref.dtype)
        return carry

    lax.fori_loop(0, ngroup, finish, 0)


def _hgrn(hproj, lb_logits, nw_tiled, layer, batch, seq, ts):
    t = hproj.shape[0]
    ns = seq // ts
    depth = lb_logits.shape[0]
    nchunk = ts // CHUNK
    npair = HGRN_W // HGRN_PW
    blk = pltpu.VMEM((nchunk, CHUNK, HGRN_W), F32)
    return pl.pallas_call(
        functools.partial(_hgrn_kernel, layer=layer, ts=ts),
        grid=(batch, ns),
        in_specs=[pl.BlockSpec((ts, 4 * HGRN_W), lambda b, i: (b * ns + i, 0)),
                  pl.BlockSpec((depth, HGRN_W), lambda b, i: (0, 0)),
                  pl.BlockSpec((1, HGRN_W), lambda b, i: (0, 0))],
        out_specs=pl.BlockSpec((ts, HGRN_W), lambda b, i: (b * ns + i, 0)),
        out_shape=jax.ShapeDtypeStruct((t, HGRN_W), BF16),
        scratch_shapes=[pltpu.VMEM((npair, HGRN_PW, HGRN_PW), F32),
                        blk, blk, blk, blk, blk, blk,
                        pltpu.VMEM((nchunk, CHUNK, HGRN_W), BF16),
                        pltpu.VMEM((ts // HGRN_GR, HGRN_GR, HGRN_W), BF16),
                        pltpu.VMEM((npair, ts // HGRN_GR, HGRN_PW, HGRN_GR), BF16),
                        pltpu.VMEM((nchunk, HGRN_WR, HGRN_W), BF16), pltpu.VMEM((nchunk, HGRN_WR, HGRN_W), F32)],
        compiler_params=_cparams(("parallel", "arbitrary")),
        name="hgrn2",
    )(hproj, lb_logits, nw_tiled)


def _s5_param_kernel(prow_ref, pcol_ref, bt_ref, btile_ref, ctile_ref, ks_ref, gs_ref, hs_ref, d_ref):
    g = pl.program_id(0)
    are, aim = prow_ref[0:1, :], prow_ref[1:2, :]
    dt = jnp.exp(prow_ref[2:3, :])
    lr, li = dt * are, dt * aim
    mag = jnp.exp(lr)
    abr, abi = mag * jnp.cos(li), mag * jnp.sin(li)
    den = are * are + aim * aim
    nr, ni = abr - 1.0, abi
    zr = (nr * are + ni * aim) / den
    zi = (ni * are - nr * aim) / den

    arec, aimc = pcol_ref[:, 0:1], pcol_ref[:, 1:2]
    dtc = jnp.exp(pcol_ref[:, 2:3])
    lrc, lic = dtc * arec, dtc * aimc
    cr, ci = ctile_ref[0], ctile_ref[1]
    wide = S5_L * S5_C
    tl = (lax.broadcasted_iota(jnp.int32, (S5_P, wide), 1) // S5_C).astype(F32)

    def powers(n):
        mg = jnp.exp(n * lrc)
        return mg * jnp.cos(n * lic), mg * jnp.sin(n * lic)

    er, ei = powers(tl)
    w1 = er * cr - ei * ci
    w2 = -(ei * cr + er * ci)
    zr1, zi1 = zr[:, 0:S5_P], zi[:, 0:S5_P]
    bbr = zr1 * bt_ref[0] - zi1 * bt_ref[1]
    bbi = zr1 * bt_ref[1] + zi1 * bt_ref[0]
    r0 = _dot_split(bbr, w1) + _dot_split(bbi, w2)
    mine = lax.broadcasted_iota(jnp.int32, (S5_C, wide), 1) // S5_C == g
    for j in range(S5_L):
        shift = (g * S5_C + (j + 1) * S5_C) % wide
        ks_ref[j] = jnp.where(mine, pltpu.roll(r0, shift, axis=1), 0.0).astype(ks_ref.dtype)

    e = (S5_L - 1 - lax.broadcasted_iota(jnp.int32, (wide, S5_ST), 0) // S5_C).astype(F32)
    mg = jnp.exp(e * lr)
    fr, fi = mg * jnp.cos(e * li), mg * jnp.sin(e * li)
    btr = zr * btile_ref[0] - zi * btile_ref[1]
    bti = zr * btile_ref[1] + zi * btile_ref[0]
    re_half = lax.broadcasted_iota(jnp.int32, (wide, S5_ST), 1) < S5_P
    g128 = jnp.where(re_half, fr * btr - fi * bti, fr * bti + fi * btr)
    tile = lax.broadcasted_iota(jnp.int32, (wide, S5_SW), 1) // S5_ST
    gfull = jnp.where(tile == g, jnp.concatenate([g128] * S5_G, axis=1), 0.0)
    gs_ref[...] = gfull.reshape(S5_L, S5_C, S5_SW).astype(gs_ref.dtype)

    pr, pi = powers(tl + 1.0)
    hcat = jnp.concatenate([cr * pr - ci * pi, -(cr * pi + ci * pr)], axis=0)
    mine_h = lax.broadcasted_iota(jnp.int32, (S5_ST, wide), 1) // S5_C == g
    for t in range(S5_L):
        shift = (g * S5_C + wide - t * S5_C) % wide
        hs_ref[t] = jnp.where(mine_h, pltpu.roll(hcat, shift, axis=1), 0.0).astype(hs_ref.dtype)

    mg = jnp.exp(float(S5_L) * lr)
    d_ref[0:1, :] = mg * jnp.cos(float(S5_L) * li)
    d_ref[1:2, :] = mg * jnp.sin(float(S5_L) * li)


def _s5_params(prow, pcol, bt, btile, ctile):
    g = prow.shape[0]
    wide = S5_L * S5_C
    m3 = lambda i: (i, 0, 0)
    m4 = lambda i: (i, 0, 0, 0)
    g4 = lambda i: (0, i, 0, 0)
    return pl.pallas_call(
        _s5_param_kernel,
        grid=(g,),
        in_specs=[pl.BlockSpec((None, 3, S5_ST), m3), pl.BlockSpec((None, S5_P, 3), m3),
                  pl.BlockSpec((None, 2, S5_C, S5_P), m4), pl.BlockSpec((None, 2, wide, S5_ST), m4),
                  pl.BlockSpec((None, 2, S5_P, wide), m4)],
        out_specs=[pl.BlockSpec((S5_L, None, S5_C, wide), g4), pl.BlockSpec((S5_L, None, S5_C, S5_SW), g4),
                   pl.BlockSpec((S5_L, None, S5_ST, wide), g4), pl.BlockSpec((None, 2, S5_ST), m3)],
        out_shape=[jax.ShapeDtypeStruct((S5_L, g, S5_C, wide), BF16), jax.ShapeDtypeStruct((S5_L, g, S5_C, S5_SW), BF16),
                   jax.ShapeDtypeStruct((S5_L, g, S5_ST, wide), BF16), jax.ShapeDtypeStruct((g, 2, S5_ST), F32)],
        compiler_params=_cparams(("parallel",)),
        name="s5_params",
    )(prow, pcol, bt, btile, ctile)


def _s5_gather_chunks(u_refs, ucat_s, n):
    for tau in range(S5_L):
        for k, u_ref in enumerate(u_refs):
            lo = tau * S5_W + k * S5_HW
            ucat_s[:, lo:lo + S5_HW] = u_ref[pl.ds(tau, n, stride=S5_L), :].astype(BF16)


def _s5_state_kernel(u0_ref, u1_ref, gs_ref, d_ref, xs_ref, ucat_s, *, nchunk):
    @pl.when(pl.program_id(1) == 0)
    def _():
        _s5_gather_chunks((u0_ref, u1_ref), ucat_s, nchunk)

    width = xs_ref.shape[1]
    x = _dot(ucat_s[...], gs_ref[...])
    sgn = jnp.where(lax.broadcasted_iota(jnp.int32, (1, width), 1) % S5_ST < S5_P, -1.0, 1.0)
    rn = lax.broadcasted_iota(jnp.int32, (nchunk, width), 0)
    pr, pi = d_ref[0:1, :], d_ref[1:2, :]

    def swap(a):
        return jnp.concatenate([pltpu.roll(a[:, k * S5_ST:(k + 1) * S5_ST], S5_P, axis=1)
                                for k in range(width // S5_ST)], axis=1)

    sh = 1
    while sh < nchunk:
        sx = jnp.where(rn >= sh, pltpu.roll(x, sh, axis=0), 0.0)
        x = x + pr * sx + (pi * sgn) * swap(sx)
        pr, pi = pr * pr - pi * pi, 2.0 * pr * pi
        sh *= 2
    xs_ref[...] = jnp.where(rn >= 1, pltpu.roll(x, 1, axis=0), 0.0).astype(xs_ref.dtype)


def _s5_state(sproj, gs, d, batch, seq):
    nchunk = seq // S5_L
    ntile = S5_SW // S5_W
    return pl.pallas_call(
        functools.partial(_s5_state_kernel, nchunk=nchunk),
        grid=(batch, ntile),
        in_specs=[pl.BlockSpec((seq, S5_HW), lambda b, j: (b, 0)), pl.BlockSpec((seq, S5_HW), lambda b, j: (b, 1)),
                  pl.BlockSpec((S5_CW, S5_W), lambda b, j: (0, j)),
                  pl.BlockSpec((2, S5_W), lambda b, j: (0, j))],
        out_specs=pl.BlockSpec((nchunk, S5_W), lambda b, j: (b, j)),
        out_shape=jax.ShapeDtypeStruct((batch * nchunk, S5_SW), BF16),
        scratch_shapes=[pltpu.VMEM((nchunk, S5_CW), BF16)],
        compiler_params=_cparams(("parallel", "arbitrary")),
        name="s5_state",
    )(sproj, sproj, gs, d)


def _s5_out_kernel(u0_ref, u1_ref, g0_ref, g1_ref, xs_ref, ks_ref, hs_ref, dsk_ref, w_ref, b_ref,
                   o0_ref, o1_ref, ucat_s, *, nb):
    t = pl.program_id(1)

    @pl.when(t == 0)
    def _():
        _s5_gather_chunks((u0_ref, u1_ref), ucat_s, nb)

    for tt in range(S5_L):
        @pl.when(t == tt)
        def _():
            rows = pl.ds(tt, nb, stride=S5_L)
            cat = lambda a, b: jnp.concatenate([a[rows, :], b[rows, :]], axis=1)
            y = _dot(ucat_s[:, 0:(tt + 1) * S5_W], ks_ref[(S5_L - 1 - tt) * S5_W:, :]) + _dot(xs_ref[...], hs_ref[...])
            y = y + dsk_ref[...] * cat(u0_ref, u1_ref)
            y = y * (0.5 * (1.0 + jnp.tanh(math.sqrt(2.0 / math.pi) * (y + 0.044715 * (y * y * y)))))
            z = _dot(y.astype(BF16), w_ref[...]) + b_ref[...]
            out = y * _sigmoid(z) * _silu(cat(g0_ref, g1_ref))
            o0_ref[rows, :] = out[:, 0:S5_HW]
            o1_ref[rows, :] = out[:, S5_HW:]


def _s5_out(sproj, xs, ks, hs, d_skip, w_bf, b, rb):
    t = sproj.shape[0]
    nb = rb // S5_L
    cst = lambda r, i: (0, 0)
    return pl.pallas_call(
        functools.partial(_s5_out_kernel, nb=nb),
        grid=(t // rb, S5_L),
        in_specs=[pl.BlockSpec((rb, S5_HW), lambda r, i: (r, 0)), pl.BlockSpec((rb, S5_HW), lambda r, i: (r, 1)),
                  pl.BlockSpec((rb, S5_HW), lambda r, i: (r, 2)), pl.BlockSpec((rb, S5_HW), lambda r, i: (r, 3)),
                  pl.BlockSpec((nb, S5_SW), lambda r, i: (r, 0)),
                  pl.BlockSpec((S5_CW, S5_W), cst),
                  pl.BlockSpec((None, S5_SW, S5_W), lambda r, i: (i, 0, 0)),
                  pl.BlockSpec((1, S5_W), cst), pl.BlockSpec((S5_W, S5_W), cst), pl.BlockSpec((1, S5_W), cst)],
        out_specs=[pl.BlockSpec((rb, S5_HW), lambda r, i: (r, 0))] * 2,
        out_shape=[jax.ShapeDtypeStruct((t, S5_HW), F32)] * 2,
        scratch_shapes=[pltpu.VMEM((nb, S5_CW), BF16)],
        compiler_params=_cparams(("parallel", "arbitrary")),
        name="s5_out",
    )(sproj, sproj, sproj, sproj, xs, ks, hs, d_skip, w_bf, b)


def _s5(sproj, a_re, a_im, b_re, b_im, c_re, c_im, log_dt, d_skip, glu_w, glu_b, batch, seq, rb):
    ldt = jnp.broadcast_to(log_dt[:, None], a_re.shape)
    twice = lambda a: jnp.concatenate([a, a], axis=-1)
    prow = twice(jnp.stack([a_re, a_im, ldt], axis=1))
    pcol = jnp.stack([a_re, a_im, ldt], axis=2)
    bt = jnp.stack([b_re, b_im], axis=1).transpose(0, 1, 3, 2)
    btile = twice(jnp.tile(bt, (1, 1, S5_L, 1)))
    ctile = jnp.tile(jnp.stack([c_re, c_im], axis=1).transpose(0, 1, 3, 2), (1, 1, 1, S5_L))
    ks, gs, hs, d = _s5_params(prow, pcol, bt, btile, ctile)
    ks = ks.reshape(S5_CW, S5_W)
    gs = gs.reshape(S5_CW, S5_SW)
    hs = hs.reshape(S5_L, S5_SW, S5_W)
    d = d.transpose(1, 0, 2).reshape(2, S5_SW)
    xs = _s5_state(sproj, gs, d, batch, seq)
    return _s5_out(sproj, xs, ks, hs, d_skip[None, :], glu_w.astype(BF16), glu_b[None, :], rb)


def _pick(n, pref):
    b = min(n, pref)
    while n % b:
        b //= 2
    return b


def kernel(x, norm_w, w_in, w_out, hgrn_lb_logits, hgrn_norm_w, s5_a_re, s5_a_im, s5_b_re, s5_b_im, s5_c_re, s5_c_im, s5_log_dt, s5_d, s5_glu_w, s5_glu_b, diff_lq1, diff_lk1, diff_lq2, diff_lk2, diff_subln_w, final_norm_w):
    batch, seq, _ = x.shape
    depth = norm_w.shape[0]
    t = batch * seq
    tm = _pick(seq, 512)
    tq = _pick(seq, 1024)
    ts = _pick(seq, 512)
    rope = _rope_tables(seq)
    h_res = x.astype(F32).reshape(t, D_MODEL)
    for l in range(depth):
        hp, sp, aq, ak, av, ag = _inproj(h_res, norm_w[l][None, :].astype(F32), w_in[l].astype(BF16), rope, seq, tm)
        mix_h = _hgrn(hp, hgrn_lb_logits.astype(F32), jnp.tile(hgrn_norm_w[l].astype(F32), HGRN_W // HGRN_HD)[None, :],
                      l, batch, seq, ts)
        mix_s = _s5(sp, s5_a_re[l], s5_a_im[l], s5_b_re[l], s5_b_im[l], s5_c_re[l], s5_c_im[l], s5_log_dt[l],
                    s5_d[l], s5_glu_w[l], s5_glu_b[l], batch, seq, _pick(seq, 4096))
        lam_init = 0.8 - 0.6 * math.exp(-0.3 * l)
        mix_a = _attention(aq, ak, av, ag, diff_lq1[l][None, :], diff_lk1[l][None, :], diff_lq2[l][None, :],
                           diff_lk2[l][None, :], diff_subln_w[l][None, :], lam_init, batch, seq, tq)
        h_res = _outproj(h_res, mix_h, mix_s[0], mix_s[1], mix_a, w_out[l].astype(BF16), final_norm_w[None, :].astype(F32),
                         l == depth - 1, tm)
    return h_res.reshape(batch, seq, D_MODEL).astype(x.dtype)
```

```python
import functools
import math

import jax
import jax.numpy as jnp
from jax import lax
from jax.experimental import pallas as pl
from jax.experimental.pallas import tpu as pltpu

F32 = jnp.float32
BF16 = jnp.bfloat16

D_MODEL = 1024
HGRN_W = 256
HGRN_HD = 64
CHUNK = 16
HGRN_WR = 8 * 16 + 8 * 8
HGRN_PW = 2 * HGRN_HD
HGRN_GR = 128
LB_FLOOR = 1e-30
S5_W = 256
S5_C = 16
S5_G = S5_W // S5_C
S5_P = 64
S5_L = 16
S5_ST = 2 * S5_P
S5_SW = S5_G * S5_ST
S5_CW = S5_L * S5_W
S5_HW = 128
ATT_W = 512
ATT_DH = 64
ATT_DV = 128
ATT_H = ATT_W // ATT_DV
ATT_CT = 256
ATT_VA = ATT_DV + 16
ROPE_DIM = 16
ROPE_THETA = 500000.0
MASK_VALUE = -1e30
Q_SCALE = ATT_DH ** -0.5 * math.log2(math.e)
EPS = 1e-6
D_IN = 4 * HGRN_W + 2 * S5_W + 4 * ATT_W
COL_S5 = 4 * HGRN_W
COL_AQ = COL_S5 + 2 * S5_W
COL_AK = COL_AQ + ATT_W
COL_AV = COL_AK + ATT_W
COL_AG = COL_AV + ATT_W

VMEM_LIMIT = 56 * 1024 * 1024


def _cparams(sem):
    return pltpu.CompilerParams(dimension_semantics=sem, vmem_limit_bytes=VMEM_LIMIT)


def _sigmoid(x):
    return 1.0 / (1.0 + jnp.exp(-x))


def _softplus_neg_abs(x):
    return jnp.log(1.0 + jnp.exp(-jnp.abs(x)))


def _silu(x):
    return x * _sigmoid(x)


def _dot(a, b):
    return jnp.dot(a, b, preferred_element_type=F32)


def _dot_split(a, b):
    ah = a.astype(BF16)
    al = (a - ah.astype(F32)).astype(BF16)
    bh = b.astype(BF16)
    bl = (b - bh.astype(F32)).astype(BF16)
    return _dot(ah, bh) + _dot(ah, bl) + _dot(al, bh)


def _rope(t, rc, ra, rb):
    return t * rc + pltpu.roll(t, 128 - ROPE_DIM // 2, axis=1) * ra + pltpu.roll(t, ROPE_DIM // 2, axis=1) * rb


def _inproj_kernel(x_ref, nw_ref, w_ref, rope_ref, h_ref, s_ref, q_ref, k_ref, v_ref, g_ref):
    x = x_ref[...]
    hn = x * lax.rsqrt(jnp.mean(x * x, axis=-1, keepdims=True) + EPS) * nw_ref[...]
    proj = _dot(hn.astype(BF16), w_ref[...])
    h_ref[...] = proj[:, :COL_S5]
    s_ref[...] = proj[:, COL_S5:COL_AQ]
    rc = rope_ref[:, 0:128]
    ra = rope_ref[:, 128:256]
    rb = rope_ref[:, 256:384]
    for h in range(ATT_H):
        lo = h * ATT_DV
        q = proj[:, COL_AQ + lo:COL_AQ + lo + ATT_DV]
        k = proj[:, COL_AK + lo:COL_AK + lo + ATT_DV]
        q_ref[:, lo:lo + ATT_DV] = (_rope(q, rc, ra, rb) * Q_SCALE).astype(BF16)
        k_ref[:, lo:lo + ATT_DV] = _rope(k, rc, ra, rb).astype(BF16)
    v_ref[...] = proj[:, COL_AV:COL_AG].astype(BF16)
    g_ref[...] = proj[:, COL_AG:]


def _rope_tables(seq):
    pos = jnp.arange(seq, dtype=F32)
    inv_freq = ROPE_THETA ** (-jnp.arange(0, ROPE_DIM, 2, dtype=F32) / ROPE_DIM)
    ang = pos[:, None] * inv_freq[None, :]
    cos, sin = jnp.cos(ang), jnp.sin(ang)
    half = ROPE_DIM // 2
    zeros = jnp.zeros((seq, ATT_DH - ROPE_DIM), F32)
    zh = jnp.zeros((seq, half), F32)
    rc = jnp.concatenate([cos, cos, jnp.ones((seq, ATT_DH - ROPE_DIM), F32)], axis=1)
    ra = jnp.concatenate([-sin, zh, zeros], axis=1)
    rb = jnp.concatenate([zh, sin, zeros], axis=1)
    return jnp.concatenate([jnp.tile(t, (1, 2)) for t in (rc, ra, rb)], axis=1)


def _inproj(x2, norm_w, w_bf, rope, seq, tm):
    t = x2.shape[0]
    nblk = seq // tm
    row = lambda i: (i, 0)
    return pl.pallas_call(
        _inproj_kernel,
        grid=(t // tm,),
        in_specs=[pl.BlockSpec((tm, D_MODEL), row),
                  pl.BlockSpec((1, D_MODEL), lambda i: (0, 0)),
                  pl.BlockSpec((D_MODEL, D_IN), lambda i: (0, 0)),
                  pl.BlockSpec((tm, 384), lambda i: (i % nblk, 0))],
        out_specs=[pl.BlockSpec((tm, COL_S5), row), pl.BlockSpec((tm, 2 * S5_W), row),
                   pl.BlockSpec((tm, ATT_W), row), pl.BlockSpec((tm, ATT_W), row),
                   pl.BlockSpec((tm, ATT_W), row), pl.BlockSpec((tm, ATT_W), row)],
        out_shape=[jax.ShapeDtypeStruct((t, COL_S5), F32), jax.ShapeDtypeStruct((t, 2 * S5_W), F32),
                   jax.ShapeDtypeStruct((t, ATT_W), BF16), jax.ShapeDtypeStruct((t, ATT_W), BF16),
                   jax.ShapeDtypeStruct((t, ATT_W), BF16), jax.ShapeDtypeStruct((t, ATT_W), F32)],
        compiler_params=_cparams(("parallel",)),
        name="inproj",
    )(x2, norm_w, w_bf, rope)


def _outproj_kernel(res_ref, mh_ref, ms0_ref, ms1_ref, ma_ref, w_ref, fw_ref, o_ref, *, final):
    acc = res_ref[...]
    acc = acc + _dot(mh_ref[...].astype(BF16), w_ref[0:HGRN_W, :])
    acc = acc + _dot(ms0_ref[...].astype(BF16), w_ref[HGRN_W:HGRN_W + S5_HW, :])
    acc = acc + _dot(ms1_ref[...].astype(BF16), w_ref[HGRN_W + S5_HW:HGRN_W + S5_W, :])
    acc = acc + _dot(ma_ref[...].astype(BF16), w_ref[HGRN_W + S5_W:, :])
    if final:
        acc = acc * lax.rsqrt(jnp.mean(acc * acc, axis=-1, keepdims=True) + EPS) * fw_ref[...]
    o_ref[...] = acc


def _outproj(res, mh, ms0, ms1, ma, w_bf, fw, final, tm):
    t = res.shape[0]
    row = lambda i: (i, 0)
    return pl.pallas_call(
        functools.partial(_outproj_kernel, final=final),
        grid=(t // tm,),
        in_specs=[pl.BlockSpec((tm, D_MODEL), row), pl.BlockSpec((tm, HGRN_W), row),
                  pl.BlockSpec((tm, S5_HW), row), pl.BlockSpec((tm, S5_HW), row), pl.BlockSpec((tm, ATT_W), row),
                  pl.BlockSpec((D_MODEL, D_MODEL), lambda i: (0, 0)),
                  pl.BlockSpec((1, D_MODEL), lambda i: (0, 0))],
        out_specs=pl.BlockSpec((tm, D_MODEL), row),
        out_shape=jax.ShapeDtypeStruct((t, D_MODEL), F32),
        compiler_params=_cparams(("parallel",)),
        name="outproj",
    )(res, mh, ms0, ms1, ma, w_bf, fw)


def _attn_kernel(q_ref, k_ref, v_ref, g_ref, lq1_ref, lk1_ref, lq2_ref, lk2_ref, sw_ref, o_ref,
                 vt_s, qq_s, acc_s, s0_s, s1_s, *, lam_init, tq, tk):
    i = pl.program_id(2)
    nkb = vt_s.shape[0]
    ntile = 2 * tq // ATT_CT
    per_map = tq // ATT_CT
    nt = (((1,), (1,)), ((), ()))

    @pl.when(i == 0)
    def _():
        for j in range(nkb):
            vt_s[j, 0:ATT_DV, :] = v_ref[j * tk:(j + 1) * tk, :].T
            vt_s[j, ATT_DV:, :] = jnp.ones((ATT_VA - ATT_DV, tk), BF16)

    q = q_ref[...]
    lane = lax.broadcasted_iota(jnp.int32, (tq, ATT_DV), 1)
    qq_s[0:tq, :] = jnp.where(lane < ATT_DH, q, jnp.zeros_like(q))
    qq_s[tq:2 * tq, :] = jnp.where(lane >= ATT_DH, q, jnp.zeros_like(q))
    acc_s[...] = jnp.zeros(acc_s.shape, F32)

    def tile_mode(c, koff):
        qs = (c % per_map) * ATT_CT
        if koff is None or qs >= koff + tk - 1:
            return "full"
        return "skip" if qs + ATT_CT - 1 < koff else "causal"

    def stage(kq, sq_s, koff_q, ssm_s, ksm, koff_sm, m_run, cmax):
        new_m, new_cmax = list(m_run), [None] * ntile
        if kq is not None:
            kb = k_ref[pl.ds(kq * tk if isinstance(kq, int) else pl.multiple_of(kq * tk, tk), tk), :]
        if ksm is not None:
            vtb = vt_s[ksm]
        for c in range(ntile):
            cols = slice(c * ATT_CT, (c + 1) * ATT_CT)
            if kq is not None and tile_mode(c, koff_q) != "skip":
                s = lax.dot_general(kb, qq_s[cols, :], nt, preferred_element_type=F32)
                if tile_mode(c, koff_q) == "causal":
                    kpos = lax.broadcasted_iota(jnp.int32, (tk, ATT_CT), 0) + koff_q
                    qpos = lax.broadcasted_iota(jnp.int32, (tk, ATT_CT), 1) + (c % per_map) * ATT_CT
                    s = jnp.where(kpos <= qpos, s, MASK_VALUE)
                sq_s[:, cols] = s
                new_cmax[c] = jnp.max(s, axis=0, keepdims=True)
            if ksm is not None and tile_mode(c, koff_sm) != "skip":
                m_new = jnp.maximum(m_run[c], cmax[c])
                alpha = jnp.exp2(m_run[c] - m_new)
                p = jnp.exp2(ssm_s[:, cols] - m_new)
                acc_s[:, cols] = alpha * acc_s[:, cols] + _dot(vtb, p.astype(BF16))
                new_m[c] = m_new
        return tuple(new_m), tuple(new_cmax)

    neg = tuple(jnp.full((1, ATT_CT), MASK_VALUE, F32) for _ in range(ntile))
    d0 = 2 * i
    _, cm = stage(d0, s0_s, 0, None, None, None, neg, None)
    m_run, cm1 = stage(d0 + 1, s1_s, tk, s0_s, d0, 0, neg, cm)
    cm1 = tuple(neg[c] if x is None else x for c, x in enumerate(cm1))

    def pair(t, carry):
        m_run, cm0 = carry
        m_run, cm1 = stage(2 * t + 1, s1_s, None, s0_s, 2 * t, None, m_run, cm0)
        return stage(2 * t + 2, s0_s, None, s1_s, 2 * t + 1, None, m_run, cm1)

    @pl.when(i == 0)
    def _():
        stage(None, None, None, s1_s, d0 + 1, tk, m_run, cm1)

    @pl.when(i > 0)
    def _():
        m2, cm0 = stage(0, s0_s, None, s1_s, d0 + 1, tk, m_run, cm1)
        m2, cm0 = lax.fori_loop(0, i - 1, pair, (m2, cm0))
        m2, cml = stage(d0 - 1, s1_s, None, s0_s, d0 - 2, None, m2, cm0)
        stage(None, None, None, s1_s, d0 - 1, None, m2, cml)

    lam = (jnp.exp(jnp.sum(lq1_ref[...] * lk1_ref[...], axis=-1, keepdims=True))
           - jnp.exp(jnp.sum(lq2_ref[...] * lk2_ref[...], axis=-1, keepdims=True)) + lam_init)
    on = acc_s[0:ATT_DV, :] / acc_s[ATT_DV:ATT_DV + 1, :]
    ot = on[:, 0:tq] - lam * on[:, tq:2 * tq]
    o = ot.T
    o = o * lax.rsqrt(jnp.mean(o * o, axis=-1, keepdims=True) + EPS) * sw_ref[...] * (1.0 - lam_init)
    o_ref[...] = (o * _silu(g_ref[...])).astype(o_ref.dtype)


def _attention(q, k, v, g, lq1, lk1, lq2, lk2, sw, lam_init, batch, seq, tq):
    t = q.shape[0]
    tk = tq // 2
    nq = seq // tq
    qmap = lambda b, h, i: (b * nq + i, h)
    kvmap = lambda b, h, i: (b, h)
    cmap = lambda b, h, i: (0, 0)
    return pl.pallas_call(
        functools.partial(_attn_kernel, lam_init=lam_init, tq=tq, tk=tk),
        grid=(batch, ATT_H, nq),
        in_specs=[pl.BlockSpec((tq, ATT_DV), qmap), pl.BlockSpec((seq, ATT_DV), kvmap),
                  pl.BlockSpec((seq, ATT_DV), kvmap), pl.BlockSpec((tq, ATT_DV), qmap),
                  pl.BlockSpec((1, ATT_DH), cmap), pl.BlockSpec((1, ATT_DH), cmap),
                  pl.BlockSpec((1, ATT_DH), cmap), pl.BlockSpec((1, ATT_DH), cmap),
                  pl.BlockSpec((1, ATT_DV), cmap)],
        out_specs=pl.BlockSpec((tq, ATT_DV), qmap),
        out_shape=jax.ShapeDtypeStruct((t, ATT_W), BF16),
        scratch_shapes=[pltpu.VMEM((seq // tk, ATT_VA, tk), BF16), pltpu.VMEM((2 * tq, ATT_DV), BF16),
                        pltpu.VMEM((ATT_VA, 2 * tq), F32),
                        pltpu.VMEM((tk, 2 * tq), F32), pltpu.VMEM((tk, 2 * tq), F32)],
        compiler_params=_cparams(("parallel", "parallel", "arbitrary")),
        name="diffattn",
    )(q, k, v, g, lq1, lk1, lq2, lk2, sw)


def _head_ones(dtype):
    r = lax.broadcasted_iota(jnp.int32, (HGRN_W, HGRN_W), 0) // HGRN_HD
    c = lax.broadcasted_iota(jnp.int32, (HGRN_W, HGRN_W), 1) // HGRN_HD
    return (r == c).astype(dtype)


def _hgrn_kernel(h_ref, lbl_ref, nw_ref, o_ref, st_ref, q_s, v_s, c_s, cs_s, dec_s, oo_s, qt_s, kh_s, vt_s,
                 w_s, sc_s, *, layer, ts):
    nchunk = ts // CHUNK
    ngroup = ts // HGRN_GR
    npair = HGRN_W // HGRN_PW
    half = CHUNK // 2

    @pl.when(pl.program_id(1) == 0)
    def _():
        st_ref[...] = jnp.zeros(st_ref.shape, F32)

    lg = lbl_ref[...]
    e = jnp.exp(lg - jnp.max(lg, axis=0, keepdims=True))
    p = e / jnp.sum(e, axis=0, keepdims=True)
    lb = jnp.sum(p[0:layer + 1, :], axis=0, keepdims=True) - p[0:1, :]
    log2e = math.log2(math.e)
    log_lb = jnp.log(jnp.maximum(lb, LB_FLOOR)) * log2e
    log1m_lb = jnp.log1p(-lb) * log2e
    r16 = lax.broadcasted_iota(jnp.int32, (HGRN_GR, HGRN_W), 0) % CHUNK
    cpg = HGRN_GR // CHUNK
    by_chunk = lambda a: a.reshape(cpg, CHUNK, HGRN_W)

    def prepare(g, carry):
        rows = pl.ds(pl.multiple_of(g * HGRN_GR, HGRN_GR), HGRN_GR)
        x2 = h_ref[rows, HGRN_W:2 * HGRN_W] * log2e
        hi = h_ref[rows, 2 * HGRN_W:3 * HGRN_W]
        b = log1m_lb - (jnp.maximum(-x2, 0.0) + jnp.log2(1.0 + jnp.exp2(-jnp.abs(x2))))
        log_f = jnp.maximum(log_lb, b) + jnp.log2(1.0 + jnp.exp2(-jnp.abs(log_lb - b)))
        k = 1.0 - jnp.exp2(log_f)
        q = _silu(h_ref[rows, 0:HGRN_W])
        cum = log_f
        sh = 1
        while sh < CHUNK:
            cum = cum + jnp.where(r16 >= sh, pltpu.roll(cum, sh, axis=0), 0.0)
            sh *= 2
        cum = by_chunk(cum)
        last = jnp.broadcast_to(cum[:, CHUNK - 1:CHUNK, :], cum.shape)
        rem = last - cum
        k, q = by_chunk(k), by_chunk(q)
        chunks = pl.ds(pl.multiple_of(g * cpg, cpg), cpg)
        q_s[chunks] = q
        v_s[chunks] = by_chunk(hi)
        c_s[chunks] = cum
        cs_s[chunks] = cum - jnp.log2(k)
        qt_s[chunks] = (q * jnp.exp2(cum)).astype(BF16)
        dec_s[chunks] = jnp.exp2(last)
        kh_s[g] = (k * jnp.exp2(rem)).reshape(HGRN_GR, HGRN_W).astype(BF16)
        vt = hi.T.astype(BF16)
        for pr in range(npair):
            vt_s[pr, g] = vt[pr * HGRN_PW:(pr + 1) * HGRN_PW, :]
        return carry

    lax.fori_loop(0, ngroup, prepare, 0)

    t8 = lax.broadcasted_iota(jnp.int32, (half, HGRN_W), 0)

    def weights(n, carry):
        c0, c1 = c_s[n, 0:half, :], c_s[n, half:, :]
        q0, q1 = q_s[n, 0:half, :], q_s[n, half:, :]
        tiles = []
        for s in range(CHUNK):
            bs = jnp.broadcast_to(cs_s[n, s:s + 1, :], (half, HGRN_W))
            if s < half:
                d0 = jnp.exp2(c0 - bs)
                tiles.append(q0 * (jnp.where(t8 >= s, d0, 0.0) if s else d0))
                tiles.append(q1 * jnp.exp2(c1 - bs))
            else:
                d1 = jnp.exp2(c1 - bs)
                tiles.append(q1 * (jnp.where(t8 >= s - half, d1, 0.0) if s > half else d1))
        w_s[n] = jnp.concatenate(tiles, axis=0).astype(BF16)
        return carry

    lax.fori_loop(0, nchunk, weights, 0)
    sc_s[...] = _dot(w_s[...].reshape(nchunk * HGRN_WR, HGRN_W), _head_ones(BF16)).reshape(nchunk, HGRN_WR, HGRN_W)

    lane_chunk = lax.broadcasted_iota(jnp.int32, (HGRN_PW, HGRN_GR), 1) // CHUNK
    same_head = (lax.broadcasted_iota(jnp.int32, (HGRN_PW, HGRN_PW), 0) // HGRN_HD
                 == lax.broadcasted_iota(jnp.int32, (HGRN_PW, HGRN_PW), 1) // HGRN_HD)
    nt = (((1,), (1,)), ((), ()))

    def group(g, carry):
        state = list(carry)
        pcols = [slice(pr * HGRN_PW, (pr + 1) * HGRN_PW) for pr in range(npair)]
        upd = []
        for j in range(cpg):
            row = []
            for pr in range(npair):
                vt = vt_s[pr, g]
                lhs = jnp.where(lane_chunk == j, vt, jnp.zeros_like(vt))
                row.append(jnp.where(same_head, _dot(lhs, kh_s[g, :, pcols[pr]]), 0.0))
            upd.append(row)
        entering = []
        for j in range(cpg):
            n = g * cpg + j
            entering.append([st.astype(BF16) for st in state])
            state = [state[pr] * dec_s[n, 0:1, pcols[pr]] + upd[j][pr] for pr in range(npair)]
        for j in range(cpg):
            n = g * cpg + j
            o0 = jnp.zeros((half, HGRN_W), F32)
            o1 = jnp.zeros((half, HGRN_W), F32)
            for s in range(CHUNK):
                vb = jnp.broadcast_to(v_s[n, s:s + 1, :], (half, HGRN_W))
                if s < half:
                    o0 = o0 + sc_s[n, s * CHUNK:s * CHUNK + half, :] * vb
                    o1 = o1 + sc_s[n, s * CHUNK + half:(s + 1) * CHUNK, :] * vb
                else:
                    o1 = o1 + sc_s[n, half * CHUNK + (s - half) * half:half * CHUNK + (s - half + 1) * half, :] * vb
            oi = jnp.concatenate([lax.dot_general(qt_s[n, :, pcols[pr]], entering[j][pr], nt,
                                                  preferred_element_type=F32) for pr in range(npair)], axis=1)
            oo_s[n, 0:half, :] = o0 + oi[0:half]
            oo_s[n, half:, :] = o1 + oi[half:]
        return tuple(state)

    state = lax.fori_loop(0, ngroup, group, tuple(st_ref[pr] for pr in range(npair)))
    for pr in range(npair):
        st_ref[pr] = state[pr]

    def finish(g, carry):
        rows = pl.ds(pl.multiple_of(g * HGRN_GR, HGRN_GR), HGRN_GR)
        o = oo_s[pl.ds(pl.multiple_of(g * cpg, cpg), cpg)].reshape(HGRN_GR, HGRN_W)
        sq = o * o
        sq_hi = sq.astype(BF16)
        ones = _head_ones(BF16)
        ms = (_dot(sq_hi, ones) + _dot((sq - sq_hi.astype(F32)).astype(BF16), ones)) * (1.0 / HGRN_HD)
        hg = h_ref[rows, 3 * HGRN_W:4 * HGRN_W]
        o_ref[rows, :] = (o * lax.rsqrt(ms + EPS) * nw_ref[...] * _silu(hg)).astype(o_ref.dtype)
        return carry

    lax.fori_loop(0, ngroup, finish, 0)


def _hgrn(hproj, lb_logits, nw_tiled, layer, batch, seq, ts):
    t = hproj.shape[0]
    ns = seq // ts
    depth = lb_logits.shape[0]
    nchunk = ts // CHUNK
    npair = HGRN_W // HGRN_PW
    blk = pltpu.VMEM((nchunk, CHUNK, HGRN_W), F32)
    return pl.pallas_call(
        functools.partial(_hgrn_kernel, layer=layer, ts=ts),
        grid=(batch, ns),
        in_specs=[pl.BlockSpec((ts, 4 * HGRN_W), lambda b, i: (b * ns + i, 0)),
                  pl.BlockSpec((depth, HGRN_W), lambda b, i: (0, 0)),
                  pl.BlockSpec((1, HGRN_W), lambda b, i: (0, 0))],
        out_specs=pl.BlockSpec((ts, HGRN_W), lambda b, i: (b * ns + i, 0)),
        out_shape=jax.ShapeDtypeStruct((t, HGRN_W), BF16),
        scratch_shapes=[pltpu.VMEM((npair, HGRN_PW, HGRN_PW), F32),
                        blk, blk, blk, blk, blk, blk,
                        pltpu.VMEM((nchunk, CHUNK, HGRN_W), BF16),
                        pltpu.VMEM((ts // HGRN_GR, HGRN_GR, HGRN_W), BF16),
                        pltpu.VMEM((npair, ts // HGRN_GR, HGRN_PW, HGRN_GR), BF16),
                        pltpu.VMEM((nchunk, HGRN_WR, HGRN_W), BF16), pltpu.VMEM((nchunk, HGRN_WR, HGRN_W), F32)],
        compiler_params=_cparams(("parallel", "arbitrary")),
        name="hgrn2",
    )(hproj, lb_logits, nw_tiled)


def _s5_param_kernel(prow_ref, pcol_ref, bt_ref, btile_ref, ctile_ref, ks_ref, gs_ref, hs_ref, d_ref):
    g = pl.program_id(0)
    are, aim = prow_ref[0:1, :], prow_ref[1:2, :]
    dt = jnp.exp(prow_ref[2:3, :])
    lr, li = dt * are, dt * aim
    mag = jnp.exp(lr)
    abr, abi = mag * jnp.cos(li), mag * jnp.sin(li)
    den = are * are + aim * aim
    nr, ni = abr - 1.0, abi
    zr = (nr * are + ni * aim) / den
    zi = (ni * are - nr * aim) / den

    arec, aimc = pcol_ref[:, 0:1], pcol_ref[:, 1:2]
    dtc = jnp.exp(pcol_ref[:, 2:3])
    lrc, lic = dtc * arec, dtc * aimc
    cr, ci = ctile_ref[0], ctile_ref[1]
    wide = S5_L * S5_C
    tl = (lax.broadcasted_iota(jnp.int32, (S5_P, wide), 1) // S5_C).astype(F32)

    def powers(n):
        mg = jnp.exp(n * lrc)
        return mg * jnp.cos(n * lic), mg * jnp.sin(n * lic)

    er, ei = powers(tl)
    w1 = er * cr - ei * ci
    w2 = -(ei * cr + er * ci)
    zr1, zi1 = zr[:, 0:S5_P], zi[:, 0:S5_P]
    bbr = zr1 * bt_ref[0] - zi1 * bt_ref[1]
    bbi = zr1 * bt_ref[1] + zi1 * bt_ref[0]
    r0 = _dot_split(bbr, w1) + _dot_split(bbi, w2)
    mine = lax.broadcasted_iota(jnp.int32, (S5_C, wide), 1) // S5_C == g
    for j in range(S5_L):
        shift = (g * S5_C + (j + 1) * S5_C) % wide
        ks_ref[j] = jnp.where(mine, pltpu.roll(r0, shift, axis=1), 0.0).astype(ks_ref.dtype)

    e = (S5_L - 1 - lax.broadcasted_iota(jnp.int32, (wide, S5_ST), 0) // S5_C).astype(F32)
    mg = jnp.exp(e * lr)
    fr, fi = mg * jnp.cos(e * li), mg * jnp.sin(e * li)
    btr = zr * btile_ref[0] - zi * btile_ref[1]
    bti = zr * btile_ref[1] + zi * btile_ref[0]
    re_half = lax.broadcasted_iota(jnp.int32, (wide, S5_ST), 1) < S5_P
    g128 = jnp.where(re_half, fr * btr - fi * bti, fr * bti + fi * btr)
    tile = lax.broadcasted_iota(jnp.int32, (wide, S5_SW), 1) // S5_ST
    gfull = jnp.where(tile == g, jnp.concatenate([g128] * S5_G, axis=1), 0.0)
    gs_ref[...] = gfull.reshape(S5_L, S5_C, S5_SW).astype(gs_ref.dtype)

    pr, pi = powers(tl + 1.0)
    hcat = jnp.concatenate([cr * pr - ci * pi, -(cr * pi + ci * pr)], axis=0)
    mine_h = lax.broadcasted_iota(jnp.int32, (S5_ST, wide), 1) // S5_C == g
    for t in range(S5_L):
        shift = (g * S5_C + wide - t * S5_C) % wide
        hs_ref[t] = jnp.where(mine_h, pltpu.roll(hcat, shift, axis=1), 0.0).astype(hs_ref.dtype)

    mg = jnp.exp(float(S5_L) * lr)
    d_ref[0:1, :] = mg * jnp.cos(float(S5_L) * li)
    d_ref[1:2, :] = mg * jnp.sin(float(S5_L) * li)


def _s5_params(prow, pcol, bt, btile, ctile):
    g = prow.shape[0]
    wide = S5_L * S5_C
    m3 = lambda i: (i, 0, 0)
    m4 = lambda i: (i, 0, 0, 0)
    g4 = lambda i: (0, i, 0, 0)
    return pl.pallas_call(
        _s5_param_kernel,
        grid=(g,),
        in_specs=[pl.BlockSpec((None, 3, S5_ST), m3), pl.BlockSpec((None, S5_P, 3), m3),
                  pl.BlockSpec((None, 2, S5_C, S5_P), m4), pl.BlockSpec((None, 2, wide, S5_ST), m4),
                  pl.BlockSpec((None, 2, S5_P, wide), m4)],
        out_specs=[pl.BlockSpec((S5_L, None, S5_C, wide), g4), pl.BlockSpec((S5_L, None, S5_C, S5_SW), g4),
                   pl.BlockSpec((S5_L, None, S5_ST, wide), g4), pl.BlockSpec((None, 2, S5_ST), m3)],
        out_shape=[jax.ShapeDtypeStruct((S5_L, g, S5_C, wide), BF16), jax.ShapeDtypeStruct((S5_L, g, S5_C, S5_SW), BF16),
                   jax.ShapeDtypeStruct((S5_L, g, S5_ST, wide), BF16), jax.ShapeDtypeStruct((g, 2, S5_ST), F32)],
        compiler_params=_cparams(("parallel",)),
        name="s5_params",
    )(prow, pcol, bt, btile, ctile)


def _s5_gather_chunks(u_refs, ucat_s, n):
    for tau in range(S5_L):
        for k, u_ref in enumerate(u_refs):
            lo = tau * S5_W + k * S5_HW
            ucat_s[:, lo:lo + S5_HW] = u_ref[pl.ds(tau, n, stride=S5_L), :].astype(BF16)


def _s5_state_kernel(u0_ref, u1_ref, gs_ref, d_ref, xs_ref, ucat_s, *, nchunk):
    @pl.when(pl.program_id(1) == 0)
    def _():
        _s5_gather_chunks((u0_ref, u1_ref), ucat_s, nchunk)

    width = xs_ref.shape[1]
    x = _dot(ucat_s[...], gs_ref[...])
    sgn = jnp.where(lax.broadcasted_iota(jnp.int32, (1, width), 1) % S5_ST < S5_P, -1.0, 1.0)
    rn = lax.broadcasted_iota(jnp.int32, (nchunk, width), 0)
    pr, pi = d_ref[0:1, :], d_ref[1:2, :]

    def swap(a):
        return jnp.concatenate([pltpu.roll(a[:, k * S5_ST:(k + 1) * S5_ST], S5_P, axis=1)
                                for k in range(width // S5_ST)], axis=1)

    sh = 1
    while sh < nchunk:
        sx = jnp.where(rn >= sh, pltpu.roll(x, sh, axis=0), 0.0)
        x = x + pr * sx + (pi * sgn) * swap(sx)
        pr, pi = pr * pr - pi * pi, 2.0 * pr * pi
        sh *= 2
    xs_ref[...] = jnp.where(rn >= 1, pltpu.roll(x, 1, axis=0), 0.0).astype(xs_ref.dtype)


def _s5_state(sproj, gs, d, batch, seq):
    nchunk = seq // S5_L
    ntile = S5_SW // S5_W
    return pl.pallas_call(
        functools.partial(_s5_state_kernel, nchunk=nchunk),
        grid=(batch, ntile),
        in_specs=[pl.BlockSpec((seq, S5_HW), lambda b, j: (b, 0)), pl.BlockSpec((seq, S5_HW), lambda b, j: (b, 1)),
                  pl.BlockSpec((S5_CW, S5_W), lambda b, j: (0, j)),
                  pl.BlockSpec((2, S5_W), lambda b, j: (0, j))],
        out_specs=pl.BlockSpec((nchunk, S5_W), lambda b, j: (b, j)),
        out_shape=jax.ShapeDtypeStruct((batch * nchunk, S5_SW), BF16),
        scratch_shapes=[pltpu.VMEM((nchunk, S5_CW), BF16)],
        compiler_params=_cparams(("parallel", "arbitrary")),
        name="s5_state",
    )(sproj, sproj, gs, d)


def _s5_out_kernel(u0_ref, u1_ref, g0_ref, g1_ref, xs_ref, ks_ref, hs_ref, dsk_ref, w_ref, b_ref,
                   o0_ref, o1_ref, ucat_s, *, nb):
    t = pl.program_id(1)

    @pl.when(t == 0)
    def _():
        _s5_gather_chunks((u0_ref, u1_ref), ucat_s, nb)

    for tt in range(S5_L):
        @pl.when(t == tt)
        def _():
            rows = pl.ds(tt, nb, stride=S5_L)
            cat = lambda a, b: jnp.concatenate([a[rows, :], b[rows, :]], axis=1)
            y = _dot(ucat_s[:, 0:(tt + 1) * S5_W], ks_ref[(S5_L - 1 - tt) * S5_W:, :]) + _dot(xs_ref[...], hs_ref[...])
            y = y + dsk_ref[...] * cat(u0_ref, u1_ref)
            y = y * (0.5 * (1.0 + jnp.tanh(math.sqrt(2.0 / math.pi) * (y + 0.044715 * (y * y * y)))))
            z = _dot(y.astype(BF16), w_ref[...]) + b_ref[...]
            out = y * _sigmoid(z) * _silu(cat(g0_ref, g1_ref))
            o0_ref[rows, :] = out[:, 0:S5_HW]
            o1_ref[rows, :] = out[:, S5_HW:]


def _s5_out(sproj, xs, ks, hs, d_skip, w_bf, b, rb):
    t = sproj.shape[0]
    nb = rb // S5_L
    cst = lambda r, i: (0, 0)
    return pl.pallas_call(
        functools.partial(_s5_out_kernel, nb=nb),
        grid=(t // rb, S5_L),
        in_specs=[pl.BlockSpec((rb, S5_HW), lambda r, i: (r, 0)), pl.BlockSpec((rb, S5_HW), lambda r, i: (r, 1)),
                  pl.BlockSpec((rb, S5_HW), lambda r, i: (r, 2)), pl.BlockSpec((rb, S5_HW), lambda r, i: (r, 3)),
                  pl.BlockSpec((nb, S5_SW), lambda r, i: (r, 0)),
                  pl.BlockSpec((S5_CW, S5_W), cst),
                  pl.BlockSpec((None, S5_SW, S5_W), lambda r, i: (i, 0, 0)),
                  pl.BlockSpec((1, S5_W), cst), pl.BlockSpec((S5_W, S5_W), cst), pl.BlockSpec((1, S5_W), cst)],
        out_specs=[pl.BlockSpec((rb, S5_HW), lambda r, i: (r, 0))] * 2,
        out_shape=[jax.ShapeDtypeStruct((t, S5_HW), F32)] * 2,
        scratch_shapes=[pltpu.VMEM((nb, S5_CW), BF16)],
        compiler_params=_cparams(("parallel", "arbitrary")),
        name="s5_out",
    )(sproj, sproj, sproj, sproj, xs, ks, hs, d_skip, w_bf, b)


def _s5(sproj, a_re, a_im, b_re, b_im, c_re, c_im, log_dt, d_skip, glu_w, glu_b, batch, seq, rb):
    ldt = jnp.broadcast_to(log_dt[:, None], a_re.shape)
    twice = lambda a: jnp.concatenate([a, a], axis=-1)
    prow = twice(jnp.stack([a_re, a_im, ldt], axis=1))
    pcol = jnp.stack([a_re, a_im, ldt], axis=2)
    bt = jnp.stack([b_re, b_im], axis=1).transpose(0, 1, 3, 2)
    btile = twice(jnp.tile(bt, (1, 1, S5_L, 1)))
    ctile = jnp.tile(jnp.stack([c_re, c_im], axis=1).transpose(0, 1, 3, 2), (1, 1, 1, S5_L))
    ks, gs, hs, d = _s5_params(prow, pcol, bt, btile, ctile)
    ks = ks.reshape(S5_CW, S5_W)
    gs = gs.reshape(S5_CW, S5_SW)
    hs = hs.reshape(S5_L, S5_SW, S5_W)
    d = d.transpose(1, 0, 2).reshape(2, S5_SW)
    xs = _s5_state(sproj, gs, d, batch, seq)
    return _s5_out(sproj, xs, ks, hs, d_skip[None, :], glu_w.astype(BF16), glu_b[None, :], rb)


def _pick(n, pref):
    b = min(n, pref)
    while n % b:
        b //= 2
    return b


def kernel(x, norm_w, w_in, w_out, hgrn_lb_logits, hgrn_norm_w, s5_a_re, s5_a_im, s5_b_re, s5_b_im, s5_c_re, s5_c_im, s5_log_dt, s5_d, s5_glu_w, s5_glu_b, diff_lq1, diff_lk1, diff_lq2, diff_lk2, diff_subln_w, final_norm_w):
    batch, seq, _ = x.shape
    depth = norm_w.shape[0]
    t = batch * seq
    tm = _pick(seq, 512)
    tq = _pick(seq, 1024)
    ts = _pick(seq, 512)
    rope = _rope_tables(seq)
    h_res = x.astype(F32).reshape(t, D_MODEL)
    for l in range(depth):
        hp, sp, aq, ak, av, ag = _inproj(h_res, norm_w[l][None, :].astype(F32), w_in[l].astype(BF16), rope, seq, tm)
        mix_h = _hgrn(hp, hgrn_lb_logits.astype(F32), jnp.tile(hgrn_norm_w[l].astype(F32), HGRN_W // HGRN_HD)[None, :],
                      l, batch, seq, ts)
        mix_s = _s5(sp, s5_a_re[l], s5_a_im[l], s5_b_re[l], s5_b_im[l], s5_c_re[l], s5_c_im[l], s5_log_dt[l],
                    s5_d[l], s5_glu_w[l], s5_glu_b[l], batch, seq, _pick(seq, 4096))
        lam_init = 0.8 - 0.6 * math.exp(-0.3 * l)
        mix_a = _attention(aq, ak, av, ag, diff_lq1[l][None, :], diff_lk1[l][None, :], diff_lq2[l][None, :],
                           diff_lk2[l][None, :], diff_subln_w[l][None, :], lam_init, batch, seq, tq)
        h_res = _outproj(h_res, mix_h, mix_s[0], mix_s[1], mix_a, w_out[l].astype(BF16), final_norm_w[None, :].astype(F32),
                         l == depth - 1, _pick(seq, 1024))
    return h_res.reshape(batch, seq, D_MODEL).astype(x.dtype)
```

```python
import functools
import math

import jax
import jax.numpy as jnp
from jax import lax
from jax.experimental import pallas as pl
from jax.experimental.pallas import tpu as pltpu

F32 = jnp.float32
BF16 = jnp.bfloat16

D_MODEL = 1024
HGRN_W = 256
HGRN_HD = 64
CHUNK = 16
HGRN_WR = 8 * 16 + 8 * 8
HGRN_PW = 2 * HGRN_HD
HGRN_GR = 128
LB_FLOOR = 1e-30
S5_W = 256
S5_C = 16
S5_G = S5_W // S5_C
S5_P = 64
S5_L = 16
S5_ST = 2 * S5_P
S5_SW = S5_G * S5_ST
S5_CW = S5_L * S5_W
S5_TG = 4
S5_HW = 128
ATT_W = 512
ATT_DH = 64
ATT_DV = 128
ATT_H = ATT_W // ATT_DV
ATT_CT = 256
ATT_VA = ATT_DV + 16
ROPE_DIM = 16
ROPE_THETA = 500000.0
MASK_VALUE = -1e30
Q_SCALE = ATT_DH ** -0.5 * math.log2(math.e)
EPS = 1e-6
D_IN = 4 * HGRN_W + 2 * S5_W + 4 * ATT_W
COL_S5 = 4 * HGRN_W
COL_AQ = COL_S5 + 2 * S5_W
COL_AK = COL_AQ + ATT_W
COL_AV = COL_AK + ATT_W
COL_AG = COL_AV + ATT_W

VMEM_LIMIT = 56 * 1024 * 1024


def _cparams(sem):
    return pltpu.CompilerParams(dimension_semantics=sem, vmem_limit_bytes=VMEM_LIMIT)


def _sigmoid(x):
    return 1.0 / (1.0 + jnp.exp(-x))


def _softplus_neg_abs(x):
    return jnp.log(1.0 + jnp.exp(-jnp.abs(x)))


def _silu(x):
    return x * _sigmoid(x)


def _dot(a, b):
    return jnp.dot(a, b, preferred_element_type=F32)


def _dot_split(a, b):
    ah = a.astype(BF16)
    al = (a - ah.astype(F32)).astype(BF16)
    bh = b.astype(BF16)
    bl = (b - bh.astype(F32)).astype(BF16)
    return _dot(ah, bh) + _dot(ah, bl) + _dot(al, bh)


def _rope(t, rc, ra, rb):
    return t * rc + pltpu.roll(t, 128 - ROPE_DIM // 2, axis=1) * ra + pltpu.roll(t, ROPE_DIM // 2, axis=1) * rb


def _inproj_kernel(x_ref, nw_ref, w_ref, rope_ref, h_ref, s_ref, q_ref, k_ref, v_ref, g_ref):
    x = x_ref[...]
    hn = x * lax.rsqrt(jnp.mean(x * x, axis=-1, keepdims=True) + EPS) * nw_ref[...]
    proj = _dot(hn.astype(BF16), w_ref[...])
    h_ref[...] = proj[:, :COL_S5]
    s_ref[...] = proj[:, COL_S5:COL_AQ]
    rc = rope_ref[:, 0:128]
    ra = rope_ref[:, 128:256]
    rb = rope_ref[:, 256:384]
    for h in range(ATT_H):
        lo = h * ATT_DV
        q = proj[:, COL_AQ + lo:COL_AQ + lo + ATT_DV]
        k = proj[:, COL_AK + lo:COL_AK + lo + ATT_DV]
        q_ref[:, lo:lo + ATT_DV] = (_rope(q, rc, ra, rb) * Q_SCALE).astype(BF16)
        k_ref[:, lo:lo + ATT_DV] = _rope(k, rc, ra, rb).astype(BF16)
    v_ref[...] = proj[:, COL_AV:COL_AG].astype(BF16)
    g_ref[...] = proj[:, COL_AG:]


def _rope_tables(seq):
    pos = jnp.arange(seq, dtype=F32)
    inv_freq = ROPE_THETA ** (-jnp.arange(0, ROPE_DIM, 2, dtype=F32) / ROPE_DIM)
    ang = pos[:, None] * inv_freq[None, :]
    cos, sin = jnp.cos(ang), jnp.sin(ang)
    half = ROPE_DIM // 2
    zeros = jnp.zeros((seq, ATT_DH - ROPE_DIM), F32)
    zh = jnp.zeros((seq, half), F32)
    rc = jnp.concatenate([cos, cos, jnp.ones((seq, ATT_DH - ROPE_DIM), F32)], axis=1)
    ra = jnp.concatenate([-sin, zh, zeros], axis=1)
    rb = jnp.concatenate([zh, sin, zeros], axis=1)
    return jnp.concatenate([jnp.tile(t, (1, 2)) for t in (rc, ra, rb)], axis=1)


def _inproj(x2, norm_w, w_bf, rope, seq, tm):
    t = x2.shape[0]
    nblk = seq // tm
    row = lambda i: (i, 0)
    return pl.pallas_call(
        _inproj_kernel,
        grid=(t // tm,),
        in_specs=[pl.BlockSpec((tm, D_MODEL), row),
                  pl.BlockSpec((1, D_MODEL), lambda i: (0, 0)),
                  pl.BlockSpec((D_MODEL, D_IN), lambda i: (0, 0)),
                  pl.BlockSpec((tm, 384), lambda i: (i % nblk, 0))],
        out_specs=[pl.BlockSpec((tm, COL_S5), row), pl.BlockSpec((tm, 2 * S5_W), row),
                   pl.BlockSpec((tm, ATT_W), row), pl.BlockSpec((tm, ATT_W), row),
                   pl.BlockSpec((tm, ATT_W), row), pl.BlockSpec((tm, ATT_W), row)],
        out_shape=[jax.ShapeDtypeStruct((t, COL_S5), F32), jax.ShapeDtypeStruct((t, 2 * S5_W), F32),
                   jax.ShapeDtypeStruct((t, ATT_W), BF16), jax.ShapeDtypeStruct((t, ATT_W), BF16),
                   jax.ShapeDtypeStruct((t, ATT_W), BF16), jax.ShapeDtypeStruct((t, ATT_W), F32)],
        compiler_params=_cparams(("parallel",)),
        name="inproj",
    )(x2, norm_w, w_bf, rope)


def _outproj_kernel(res_ref, mh_ref, ms0_ref, ms1_ref, ma_ref, w_ref, fw_ref, o_ref, *, final):
    acc = res_ref[...]
    acc = acc + _dot(mh_ref[...].astype(BF16), w_ref[0:HGRN_W, :])
    acc = acc + _dot(ms0_ref[...].astype(BF16), w_ref[HGRN_W:HGRN_W + S5_HW, :])
    acc = acc + _dot(ms1_ref[...].astype(BF16), w_ref[HGRN_W + S5_HW:HGRN_W + S5_W, :])
    acc = acc + _dot(ma_ref[...].astype(BF16), w_ref[HGRN_W + S5_W:, :])
    if final:
        acc = acc * lax.rsqrt(jnp.mean(acc * acc, axis=-1, keepdims=True) + EPS) * fw_ref[...]
    o_ref[...] = acc


def _outproj(res, mh, ms0, ms1, ma, w_bf, fw, final, tm):
    t = res.shape[0]
    row = lambda i: (i, 0)
    return pl.pallas_call(
        functools.partial(_outproj_kernel, final=final),
        grid=(t // tm,),
        in_specs=[pl.BlockSpec((tm, D_MODEL), row), pl.BlockSpec((tm, HGRN_W), row),
                  pl.BlockSpec((tm, S5_HW), row), pl.BlockSpec((tm, S5_HW), row), pl.BlockSpec((tm, ATT_W), row),
                  pl.BlockSpec((D_MODEL, D_MODEL), lambda i: (0, 0)),
                  pl.BlockSpec((1, D_MODEL), lambda i: (0, 0))],
        out_specs=pl.BlockSpec((tm, D_MODEL), row),
        out_shape=jax.ShapeDtypeStruct((t, D_MODEL), F32),
        compiler_params=_cparams(("parallel",)),
        name="outproj",
    )(res, mh, ms0, ms1, ma, w_bf, fw)


def _attn_kernel(q_ref, k_ref, v_ref, g_ref, lq1_ref, lk1_ref, lq2_ref, lk2_ref, sw_ref, o_ref,
                 vt_s, qq_s, acc_s, s0_s, s1_s, *, lam_init, tq, tk):
    i = pl.program_id(2)
    nkb = vt_s.shape[0]
    ntile = 2 * tq // ATT_CT
    per_map = tq // ATT_CT
    nt = (((1,), (1,)), ((), ()))

    @pl.when(i == 0)
    def _():
        for j in range(nkb):
            vt_s[j, 0:ATT_DV, :] = v_ref[j * tk:(j + 1) * tk, :].T
            vt_s[j, ATT_DV:, :] = jnp.ones((ATT_VA - ATT_DV, tk), BF16)

    q = q_ref[...]
    lane = lax.broadcasted_iota(jnp.int32, (tq, ATT_DV), 1)
    qq_s[0:tq, :] = jnp.where(lane < ATT_DH, q, jnp.zeros_like(q))
    qq_s[tq:2 * tq, :] = jnp.where(lane >= ATT_DH, q, jnp.zeros_like(q))
    acc_s[...] = jnp.zeros(acc_s.shape, F32)

    def tile_mode(c, koff):
        qs = (c % per_map) * ATT_CT
        if koff is None or qs >= koff + tk - 1:
            return "full"
        return "skip" if qs + ATT_CT - 1 < koff else "causal"

    def stage(kq, sq_s, koff_q, ssm_s, ksm, koff_sm, m_run, cmax):
        new_m, new_cmax = list(m_run), [None] * ntile
        if kq is not None:
            kb = k_ref[pl.ds(kq * tk if isinstance(kq, int) else pl.multiple_of(kq * tk, tk), tk), :]
        if ksm is not None:
            vtb = vt_s[ksm]
        for c in range(ntile):
            cols = slice(c * ATT_CT, (c + 1) * ATT_CT)
            if kq is not None and tile_mode(c, koff_q) != "skip":
                s = lax.dot_general(kb, qq_s[cols, :], nt, preferred_element_type=F32)
                if tile_mode(c, koff_q) == "causal":
                    kpos = lax.broadcasted_iota(jnp.int32, (tk, ATT_CT), 0) + koff_q
                    qpos = lax.broadcasted_iota(jnp.int32, (tk, ATT_CT), 1) + (c % per_map) * ATT_CT
                    s = jnp.where(kpos <= qpos, s, MASK_VALUE)
                sq_s[:, cols] = s
                new_cmax[c] = jnp.max(s, axis=0, keepdims=True)
            if ksm is not None and tile_mode(c, koff_sm) != "skip":
                m_new = jnp.maximum(m_run[c], cmax[c])
                alpha = jnp.exp2(m_run[c] - m_new)
                p = jnp.exp2(ssm_s[:, cols] - m_new)
                acc_s[:, cols] = alpha * acc_s[:, cols] + _dot(vtb, p.astype(BF16))
                new_m[c] = m_new
        return tuple(new_m), tuple(new_cmax)

    neg = tuple(jnp.full((1, ATT_CT), MASK_VALUE, F32) for _ in range(ntile))
    d0 = 2 * i
    _, cm = stage(d0, s0_s, 0, None, None, None, neg, None)
    m_run, cm1 = stage(d0 + 1, s1_s, tk, s0_s, d0, 0, neg, cm)
    cm1 = tuple(neg[c] if x is None else x for c, x in enumerate(cm1))

    def pair(t, carry):
        m_run, cm0 = carry
        m_run, cm1 = stage(2 * t + 1, s1_s, None, s0_s, 2 * t, None, m_run, cm0)
        return stage(2 * t + 2, s0_s, None, s1_s, 2 * t + 1, None, m_run, cm1)

    @pl.when(i == 0)
    def _():
        stage(None, None, None, s1_s, d0 + 1, tk, m_run, cm1)

    @pl.when(i > 0)
    def _():
        m2, cm0 = stage(0, s0_s, None, s1_s, d0 + 1, tk, m_run, cm1)
        m2, cm0 = lax.fori_loop(0, i - 1, pair, (m2, cm0))
        m2, cml = stage(d0 - 1, s1_s, None, s0_s, d0 - 2, None, m2, cm0)
        stage(None, None, None, s1_s, d0 - 1, None, m2, cml)

    lam = (jnp.exp(jnp.sum(lq1_ref[...] * lk1_ref[...], axis=-1, keepdims=True))
           - jnp.exp(jnp.sum(lq2_ref[...] * lk2_ref[...], axis=-1, keepdims=True)) + lam_init)
    on = acc_s[0:ATT_DV, :] / acc_s[ATT_DV:ATT_DV + 1, :]
    ot = on[:, 0:tq] - lam * on[:, tq:2 * tq]
    o = ot.T
    o = o * lax.rsqrt(jnp.mean(o * o, axis=-1, keepdims=True) + EPS) * sw_ref[...] * (1.0 - lam_init)
    o_ref[...] = (o * _silu(g_ref[...])).astype(o_ref.dtype)


def _attention(q, k, v, g, lq1, lk1, lq2, lk2, sw, lam_init, batch, seq, tq):
    t = q.shape[0]
    tk = tq // 2
    nq = seq // tq
    qmap = lambda b, h, i: (b * nq + i, h)
    kvmap = lambda b, h, i: (b, h)
    cmap = lambda b, h, i: (0, 0)
    return pl.pallas_call(
        functools.partial(_attn_kernel, lam_init=lam_init, tq=tq, tk=tk),
        grid=(batch, ATT_H, nq),
        in_specs=[pl.BlockSpec((tq, ATT_DV), qmap), pl.BlockSpec((seq, ATT_DV), kvmap),
                  pl.BlockSpec((seq, ATT_DV), kvmap), pl.BlockSpec((tq, ATT_DV), qmap),
                  pl.BlockSpec((1, ATT_DH), cmap), pl.BlockSpec((1, ATT_DH), cmap),
                  pl.BlockSpec((1, ATT_DH), cmap), pl.BlockSpec((1, ATT_DH), cmap),
                  pl.BlockSpec((1, ATT_DV), cmap)],
        out_specs=pl.BlockSpec((tq, ATT_DV), qmap),
        out_shape=jax.ShapeDtypeStruct((t, ATT_W), BF16),
        scratch_shapes=[pltpu.VMEM((seq // tk, ATT_VA, tk), BF16), pltpu.VMEM((2 * tq, ATT_DV), BF16),
                        pltpu.VMEM((ATT_VA, 2 * tq), F32),
                        pltpu.VMEM((tk, 2 * tq), F32), pltpu.VMEM((tk, 2 * tq), F32)],
        compiler_params=_cparams(("parallel", "parallel", "arbitrary")),
        name="diffattn",
    )(q, k, v, g, lq1, lk1, lq2, lk2, sw)


def _head_ones(dtype):
    r = lax.broadcasted_iota(jnp.int32, (HGRN_W, HGRN_W), 0) // HGRN_HD
    c = lax.broadcasted_iota(jnp.int32, (HGRN_W, HGRN_W), 1) // HGRN_HD
    return (r == c).astype(dtype)


def _hgrn_kernel(h_ref, lbl_ref, nw_ref, o_ref, st_ref, q_s, v_s, c_s, cs_s, dec_s, oo_s, qt_s, kh_s, vt_s,
                 w_s, sc_s, *, layer, ts):
    nchunk = ts // CHUNK
    ngroup = ts // HGRN_GR
    npair = HGRN_W // HGRN_PW
    half = CHUNK // 2

    @pl.when(pl.program_id(1) == 0)
    def _():
        st_ref[...] = jnp.zeros(st_ref.shape, F32)

    lg = lbl_ref[...]
    e = jnp.exp(lg - jnp.max(lg, axis=0, keepdims=True))
    p = e / jnp.sum(e, axis=0, keepdims=True)
    lb = jnp.sum(p[0:layer + 1, :], axis=0, keepdims=True) - p[0:1, :]
    log2e = math.log2(math.e)
    log_lb = jnp.log(jnp.maximum(lb, LB_FLOOR)) * log2e
    log1m_lb = jnp.log1p(-lb) * log2e
    r16 = lax.broadcasted_iota(jnp.int32, (HGRN_GR, HGRN_W), 0) % CHUNK
    cpg = HGRN_GR // CHUNK
    by_chunk = lambda a: a.reshape(cpg, CHUNK, HGRN_W)

    def prepare(g, carry):
        rows = pl.ds(pl.multiple_of(g * HGRN_GR, HGRN_GR), HGRN_GR)
        x2 = h_ref[rows, HGRN_W:2 * HGRN_W] * log2e
        hi = h_ref[rows, 2 * HGRN_W:3 * HGRN_W]
        b = log1m_lb - (jnp.maximum(-x2, 0.0) + jnp.log2(1.0 + jnp.exp2(-jnp.abs(x2))))
        log_f = jnp.maximum(log_lb, b) + jnp.log2(1.0 + jnp.exp2(-jnp.abs(log_lb - b)))
        k = 1.0 - jnp.exp2(log_f)
        q = _silu(h_ref[rows, 0:HGRN_W])
        cum = log_f
        sh = 1
        while sh < CHUNK:
            cum = cum + jnp.where(r16 >= sh, pltpu.roll(cum, sh, axis=0), 0.0)
            sh *= 2
        cum = by_chunk(cum)
        last = jnp.broadcast_to(cum[:, CHUNK - 1:CHUNK, :], cum.shape)
        rem = last - cum
        k, q = by_chunk(k), by_chunk(q)
        chunks = pl.ds(pl.multiple_of(g * cpg, cpg), cpg)
        q_s[chunks] = q
        v_s[chunks] = by_chunk(hi)
        c_s[chunks] = cum
        cs_s[chunks] = cum - jnp.log2(k)
        qt_s[chunks] = (q * jnp.exp2(cum)).astype(BF16)
        dec_s[chunks] = jnp.exp2(last)
        kh_s[g] = (k * jnp.exp2(rem)).reshape(HGRN_GR, HGRN_W).astype(BF16)
        vt = hi.T.astype(BF16)
        for pr in range(npair):
            vt_s[pr, g] = vt[pr * HGRN_PW:(pr + 1) * HGRN_PW, :]
        return carry

    lax.fori_loop(0, ngroup, prepare, 0)

    t8 = lax.broadcasted_iota(jnp.int32, (half, HGRN_W), 0)

    def weights(n, carry):
        c0, c1 = c_s[n, 0:half, :], c_s[n, half:, :]
        q0, q1 = q_s[n, 0:half, :], q_s[n, half:, :]
        tiles = []
        for s in range(CHUNK):
            bs = jnp.broadcast_to(cs_s[n, s:s + 1, :], (half, HGRN_W))
            if s < half:
                d0 = jnp.exp2(c0 - bs)
                tiles.append(q0 * (jnp.where(t8 >= s, d0, 0.0) if s else d0))
                tiles.append(q1 * jnp.exp2(c1 - bs))
            else:
                d1 = jnp.exp2(c1 - bs)
                tiles.append(q1 * (jnp.where(t8 >= s - half, d1, 0.0) if s > half else d1))
        w_s[n] = jnp.concatenate(tiles, axis=0).astype(BF16)
        return carry

    lax.fori_loop(0, nchunk, weights, 0)
    sc_s[...] = _dot(w_s[...].reshape(nchunk * HGRN_WR, HGRN_W), _head_ones(BF16)).reshape(nchunk, HGRN_WR, HGRN_W)

    lane_chunk = lax.broadcasted_iota(jnp.int32, (HGRN_PW, HGRN_GR), 1) // CHUNK
    same_head = (lax.broadcasted_iota(jnp.int32, (HGRN_PW, HGRN_PW), 0) // HGRN_HD
                 == lax.broadcasted_iota(jnp.int32, (HGRN_PW, HGRN_PW), 1) // HGRN_HD)
    nt = (((1,), (1,)), ((), ()))

    def group(g, carry):
        state = list(carry)
        pcols = [slice(pr * HGRN_PW, (pr + 1) * HGRN_PW) for pr in range(npair)]
        upd = []
        for j in range(cpg):
            row = []
            for pr in range(npair):
                vt = vt_s[pr, g]
                lhs = jnp.where(lane_chunk == j, vt, jnp.zeros_like(vt))
                row.append(jnp.where(same_head, _dot(lhs, kh_s[g, :, pcols[pr]]), 0.0))
            upd.append(row)
        entering = []
        for j in range(cpg):
            n = g * cpg + j
            entering.append([st.astype(BF16) for st in state])
            state = [state[pr] * dec_s[n, 0:1, pcols[pr]] + upd[j][pr] for pr in range(npair)]
        for j in range(cpg):
            n = g * cpg + j
            o0 = jnp.zeros((half, HGRN_W), F32)
            o1 = jnp.zeros((half, HGRN_W), F32)
            for s in range(CHUNK):
                vb = jnp.broadcast_to(v_s[n, s:s + 1, :], (half, HGRN_W))
                if s < half:
                    o0 = o0 + sc_s[n, s * CHUNK:s * CHUNK + half, :] * vb
                    o1 = o1 + sc_s[n, s * CHUNK + half:(s + 1) * CHUNK, :] * vb
                else:
                    o1 = o1 + sc_s[n, half * CHUNK + (s - half) * half:half * CHUNK + (s - half + 1) * half, :] * vb
            oi = jnp.concatenate([lax.dot_general(qt_s[n, :, pcols[pr]], entering[j][pr], nt,
                                                  preferred_element_type=F32) for pr in range(npair)], axis=1)
            oo_s[n, 0:half, :] = o0 + oi[0:half]
            oo_s[n, half:, :] = o1 + oi[half:]
        return tuple(state)

    state = lax.fori_loop(0, ngroup, group, tuple(st_ref[pr] for pr in range(npair)))
    for pr in range(npair):
        st_ref[pr] = state[pr]

    def finish(g, carry):
        rows = pl.ds(pl.multiple_of(g * HGRN_GR, HGRN_GR), HGRN_GR)
        o = oo_s[pl.ds(pl.multiple_of(g * cpg, cpg), cpg)].reshape(HGRN_GR, HGRN_W)
        sq = o * o
        sq_hi = sq.astype(BF16)
        ones = _head_ones(BF16)
        ms = (_dot(sq_hi, ones) + _dot((sq - sq_hi.astype(F32)).astype(BF16), ones)) * (1.0 / HGRN_HD)
        hg = h_ref[rows, 3 * HGRN_W:4 * HGRN_W]
        o_ref[rows, :] = (o * lax.rsqrt(ms + EPS) * nw_ref[...] * _silu(hg)).astype(o_ref.dtype)
        return carry

    lax.fori_loop(0, ngroup, finish, 0)


def _hgrn(hproj, lb_logits, nw_tiled, layer, batch, seq, ts):
    t = hproj.shape[0]
    ns = seq // ts
    depth = lb_logits.shape[0]
    nchunk = ts // CHUNK
    npair = HGRN_W // HGRN_PW
    blk = pltpu.VMEM((nchunk, CHUNK, HGRN_W), F32)
    return pl.pallas_call(
        functools.partial(_hgrn_kernel, layer=layer, ts=ts),
        grid=(batch, ns),
        in_specs=[pl.BlockSpec((ts, 4 * HGRN_W), lambda b, i: (b * ns + i, 0)),
                  pl.BlockSpec((depth, HGRN_W), lambda b, i: (0, 0)),
                  pl.BlockSpec((1, HGRN_W), lambda b, i: (0, 0))],
        out_specs=pl.BlockSpec((ts, HGRN_W), lambda b, i: (b * ns + i, 0)),
        out_shape=jax.ShapeDtypeStruct((t, HGRN_W), BF16),
        scratch_shapes=[pltpu.VMEM((npair, HGRN_PW, HGRN_PW), F32),
                        blk, blk, blk, blk, blk, blk,
                        pltpu.VMEM((nchunk, CHUNK, HGRN_W), BF16),
                        pltpu.VMEM((ts // HGRN_GR, HGRN_GR, HGRN_W), BF16),
                        pltpu.VMEM((npair, ts // HGRN_GR, HGRN_PW, HGRN_GR), BF16),
                        pltpu.VMEM((nchunk, HGRN_WR, HGRN_W), BF16), pltpu.VMEM((nchunk, HGRN_WR, HGRN_W), F32)],
        compiler_params=_cparams(("parallel", "arbitrary")),
        name="hgrn2",
    )(hproj, lb_logits, nw_tiled)


def _s5_param_kernel(prow_ref, pcol_ref, bt_ref, btile_ref, ctile_ref, ks_ref, gs_ref, hs_ref, d_ref):
    g = pl.program_id(0)
    are, aim = prow_ref[0:1, :], prow_ref[1:2, :]
    dt = jnp.exp(prow_ref[2:3, :])
    lr, li = dt * are, dt * aim
    mag = jnp.exp(lr)
    abr, abi = mag * jnp.cos(li), mag * jnp.sin(li)
    den = are * are + aim * aim
    nr, ni = abr - 1.0, abi
    zr = (nr * are + ni * aim) / den
    zi = (ni * are - nr * aim) / den

    arec, aimc = pcol_ref[:, 0:1], pcol_ref[:, 1:2]
    dtc = jnp.exp(pcol_ref[:, 2:3])
    lrc, lic = dtc * arec, dtc * aimc
    cr, ci = ctile_ref[0], ctile_ref[1]
    wide = S5_L * S5_C
    tl = (lax.broadcasted_iota(jnp.int32, (S5_P, wide), 1) // S5_C).astype(F32)

    def powers(n):
        mg = jnp.exp(n * lrc)
        return mg * jnp.cos(n * lic), mg * jnp.sin(n * lic)

    er, ei = powers(tl)
    w1 = er * cr - ei * ci
    w2 = -(ei * cr + er * ci)
    zr1, zi1 = zr[:, 0:S5_P], zi[:, 0:S5_P]
    bbr = zr1 * bt_ref[0] - zi1 * bt_ref[1]
    bbi = zr1 * bt_ref[1] + zi1 * bt_ref[0]
    r0 = _dot_split(bbr, w1) + _dot_split(bbi, w2)
    mine = lax.broadcasted_iota(jnp.int32, (S5_C, wide), 1) // S5_C == g
    for j in range(S5_L):
        shift = (g * S5_C + (j + 1) * S5_C) % wide
        ks_ref[j] = jnp.where(mine, pltpu.roll(r0, shift, axis=1), 0.0).astype(ks_ref.dtype)

    e = (S5_L - 1 - lax.broadcasted_iota(jnp.int32, (wide, S5_ST), 0) // S5_C).astype(F32)
    mg = jnp.exp(e * lr)
    fr, fi = mg * jnp.cos(e * li), mg * jnp.sin(e * li)
    btr = zr * btile_ref[0] - zi * btile_ref[1]
    bti = zr * btile_ref[1] + zi * btile_ref[0]
    re_half = lax.broadcasted_iota(jnp.int32, (wide, S5_ST), 1) < S5_P
    g128 = jnp.where(re_half, fr * btr - fi * bti, fr * bti + fi * btr)
    tile = lax.broadcasted_iota(jnp.int32, (wide, S5_SW), 1) // S5_ST
    gfull = jnp.where(tile == g, jnp.concatenate([g128] * S5_G, axis=1), 0.0)
    gs_ref[...] = gfull.reshape(S5_L, S5_C, S5_SW).astype(gs_ref.dtype)

    pr, pi = powers(tl + 1.0)
    hcat = jnp.concatenate([cr * pr - ci * pi, -(cr * pi + ci * pr)], axis=0)
    mine_h = lax.broadcasted_iota(jnp.int32, (S5_ST, wide), 1) // S5_C == g
    for t in range(S5_L):
        shift = (g * S5_C + wide - t * S5_C) % wide
        hs_ref[t] = jnp.where(mine_h, pltpu.roll(hcat, shift, axis=1), 0.0).astype(hs_ref.dtype)

    mg = jnp.exp(float(S5_L) * lr)
    d_ref[0:1, :] = mg * jnp.cos(float(S5_L) * li)
    d_ref[1:2, :] = mg * jnp.sin(float(S5_L) * li)


def _s5_params(prow, pcol, bt, btile, ctile):
    g = prow.shape[0]
    wide = S5_L * S5_C
    m3 = lambda i: (i, 0, 0)
    m4 = lambda i: (i, 0, 0, 0)
    g4 = lambda i: (0, i, 0, 0)
    return pl.pallas_call(
        _s5_param_kernel,
        grid=(g,),
        in_specs=[pl.BlockSpec((None, 3, S5_ST), m3), pl.BlockSpec((None, S5_P, 3), m3),
                  pl.BlockSpec((None, 2, S5_C, S5_P), m4), pl.BlockSpec((None, 2, wide, S5_ST), m4),
                  pl.BlockSpec((None, 2, S5_P, wide), m4)],
        out_specs=[pl.BlockSpec((S5_L, None, S5_C, wide), g4), pl.BlockSpec((S5_L, None, S5_C, S5_SW), g4),
                   pl.BlockSpec((S5_L, None, S5_ST, wide), g4), pl.BlockSpec((None, 2, S5_ST), m3)],
        out_shape=[jax.ShapeDtypeStruct((S5_L, g, S5_C, wide), BF16), jax.ShapeDtypeStruct((S5_L, g, S5_C, S5_SW), BF16),
                   jax.ShapeDtypeStruct((S5_L, g, S5_ST, wide), BF16), jax.ShapeDtypeStruct((g, 2, S5_ST), F32)],
        compiler_params=_cparams(("parallel",)),
        name="s5_params",
    )(prow, pcol, bt, btile, ctile)


def _s5_gather_chunks(u_refs, ucat_s, n):
    for tau in range(S5_L):
        for k, u_ref in enumerate(u_refs):
            lo = tau * S5_W + k * S5_HW
            ucat_s[:, lo:lo + S5_HW] = u_ref[pl.ds(tau, n, stride=S5_L), :].astype(BF16)


def _s5_state_kernel(u0_ref, u1_ref, gs_ref, d_ref, xs_ref, ucat_s, *, nchunk):
    @pl.when(pl.program_id(1) == 0)
    def _():
        _s5_gather_chunks((u0_ref, u1_ref), ucat_s, nchunk)

    width = xs_ref.shape[1]
    x = _dot(ucat_s[...], gs_ref[...])
    sgn = jnp.where(lax.broadcasted_iota(jnp.int32, (1, width), 1) % S5_ST < S5_P, -1.0, 1.0)
    rn = lax.broadcasted_iota(jnp.int32, (nchunk, width), 0)
    pr, pi = d_ref[0:1, :], d_ref[1:2, :]

    def swap(a):
        return jnp.concatenate([pltpu.roll(a[:, k * S5_ST:(k + 1) * S5_ST], S5_P, axis=1)
                                for k in range(width // S5_ST)], axis=1)

    sh = 1
    while sh < nchunk:
        sx = jnp.where(rn >= sh, pltpu.roll(x, sh, axis=0), 0.0)
        x = x + pr * sx + (pi * sgn) * swap(sx)
        pr, pi = pr * pr - pi * pi, 2.0 * pr * pi
        sh *= 2
    xs_ref[...] = jnp.where(rn >= 1, pltpu.roll(x, 1, axis=0), 0.0).astype(xs_ref.dtype)


def _s5_state(sproj, gs, d, batch, seq):
    nchunk = seq // S5_L
    ntile = S5_SW // S5_W
    return pl.pallas_call(
        functools.partial(_s5_state_kernel, nchunk=nchunk),
        grid=(batch, ntile),
        in_specs=[pl.BlockSpec((seq, S5_HW), lambda b, j: (b, 0)), pl.BlockSpec((seq, S5_HW), lambda b, j: (b, 1)),
                  pl.BlockSpec((S5_CW, S5_W), lambda b, j: (0, j)),
                  pl.BlockSpec((2, S5_W), lambda b, j: (0, j))],
        out_specs=pl.BlockSpec((nchunk, S5_W), lambda b, j: (b, j)),
        out_shape=jax.ShapeDtypeStruct((batch * nchunk, S5_SW), BF16),
        scratch_shapes=[pltpu.VMEM((nchunk, S5_CW), BF16)],
        compiler_params=_cparams(("parallel", "arbitrary")),
        name="s5_state",
    )(sproj, sproj, gs, d)


def _s5_out_kernel(u0_ref, u1_ref, g0_ref, g1_ref, xs_ref, ks_ref, hs_ref, dsk_ref, w_ref, b_ref,
                   o0_ref, o1_ref, ucat_s, *, nb):
    step = pl.program_id(1)

    @pl.when(step == 0)
    def _():
        _s5_gather_chunks((u0_ref, u1_ref), ucat_s, nb)

    def response(tt, k):
        return (_dot(ucat_s[:, 0:(tt + 1) * S5_W], ks_ref[(S5_L - 1 - tt) * S5_W:, :])
                + _dot(xs_ref[...], hs_ref[k]))

    def finish(tt, y):
        rows = pl.ds(tt, nb, stride=S5_L)
        cat = lambda a, b: jnp.concatenate([a[rows, :], b[rows, :]], axis=1)
        y = y + dsk_ref[...] * cat(u0_ref, u1_ref)
        y = y * (0.5 * (1.0 + jnp.tanh(math.sqrt(2.0 / math.pi) * (y + 0.044715 * (y * y * y)))))
        z = _dot(y.astype(BF16), w_ref[...]) + b_ref[...]
        out = y * _sigmoid(z) * _silu(cat(g0_ref, g1_ref))
        o0_ref[rows, :] = out[:, 0:S5_HW]
        o1_ref[rows, :] = out[:, S5_HW:]

    for sg in range(S5_L // S5_TG):
        @pl.when(step == sg)
        def _():
            ys = [response(sg * S5_TG + k, k) for k in range(S5_TG)]
            for k in range(S5_TG):
                finish(sg * S5_TG + k, ys[k])


def _s5_out(sproj, xs, ks, hs, d_skip, w_bf, b, rb):
    t = sproj.shape[0]
    nb = rb // S5_L
    cst = lambda r, i: (0, 0)
    return pl.pallas_call(
        functools.partial(_s5_out_kernel, nb=nb),
        grid=(t // rb, S5_L // S5_TG),
        in_specs=[pl.BlockSpec((rb, S5_HW), lambda r, i: (r, 0)), pl.BlockSpec((rb, S5_HW), lambda r, i: (r, 1)),
                  pl.BlockSpec((rb, S5_HW), lambda r, i: (r, 2)), pl.BlockSpec((rb, S5_HW), lambda r, i: (r, 3)),
                  pl.BlockSpec((nb, S5_SW), lambda r, i: (r, 0)),
                  pl.BlockSpec((S5_CW, S5_W), cst),
                  pl.BlockSpec((S5_TG, S5_SW, S5_W), lambda r, i: (i, 0, 0)),
                  pl.BlockSpec((1, S5_W), cst), pl.BlockSpec((S5_W, S5_W), cst), pl.BlockSpec((1, S5_W), cst)],
        out_specs=[pl.BlockSpec((rb, S5_HW), lambda r, i: (r, 0))] * 2,
        out_shape=[jax.ShapeDtypeStruct((t, S5_HW), F32)] * 2,
        scratch_shapes=[pltpu.VMEM((nb, S5_CW), BF16)],
        compiler_params=_cparams(("parallel", "arbitrary")),
        name="s5_out",
    )(sproj, sproj, sproj, sproj, xs, ks, hs, d_skip, w_bf, b)


def _s5(sproj, a_re, a_im, b_re, b_im, c_re, c_im, log_dt, d_skip, glu_w, glu_b, batch, seq, rb):
    ldt = jnp.broadcast_to(log_dt[:, None], a_re.shape)
    twice = lambda a: jnp.concatenate([a, a], axis=-1)
    prow = twice(jnp.stack([a_re, a_im, ldt], axis=1))
    pcol = jnp.stack([a_re, a_im, ldt], axis=2)
    bt = jnp.stack([b_re, b_im], axis=1).transpose(0, 1, 3, 2)
    btile = twice(jnp.tile(bt, (1, 1, S5_L, 1)))
    ctile = jnp.tile(jnp.stack([c_re, c_im], axis=1).transpose(0, 1, 3, 2), (1, 1, 1, S5_L))
    ks, gs, hs, d = _s5_params(prow, pcol, bt, btile, ctile)
    ks = ks.reshape(S5_CW, S5_W)
    gs = gs.reshape(S5_CW, S5_SW)
    hs = hs.reshape(S5_L, S5_SW, S5_W)
    d = d.transpose(1, 0, 2).reshape(2, S5_SW)
    xs = _s5_state(sproj, gs, d, batch, seq)
    return _s5_out(sproj, xs, ks, hs, d_skip[None, :], glu_w.astype(BF16), glu_b[None, :], rb)


def _pick(n, pref):
    b = min(n, pref)
    while n % b:
        b //= 2
    return b


def kernel(x, norm_w, w_in, w_out, hgrn_lb_logits, hgrn_norm_w, s5_a_re, s5_a_im, s5_b_re, s5_b_im, s5_c_re, s5_c_im, s5_log_dt, s5_d, s5_glu_w, s5_glu_b, diff_lq1, diff_lk1, diff_lq2, diff_lk2, diff_subln_w, final_norm_w):
    batch, seq, _ = x.shape
    depth = norm_w.shape[0]
    t = batch * seq
    tm = _pick(seq, 512)
    tq = _pick(seq, 1024)
    ts = _pick(seq, 512)
    rope = _rope_tables(seq)
    h_res = x.astype(F32).reshape(t, D_MODEL)
    for l in range(depth):
        hp, sp, aq, ak, av, ag = _inproj(h_res, norm_w[l][None, :].astype(F32), w_in[l].astype(BF16), rope, seq, tm)
        mix_h = _hgrn(hp, hgrn_lb_logits.astype(F32), jnp.tile(hgrn_norm_w[l].astype(F32), HGRN_W // HGRN_HD)[None, :],
                      l, batch, seq, ts)
        mix_s = _s5(sp, s5_a_re[l], s5_a_im[l], s5_b_re[l], s5_b_im[l], s5_c_re[l], s5_c_im[l], s5_log_dt[l],
                    s5_d[l], s5_glu_w[l], s5_glu_b[l], batch, seq, _pick(seq, 4096))
        lam_init = 0.8 - 0.6 * math.exp(-0.3 * l)
        mix_a = _attention(aq, ak, av, ag, diff_lq1[l][None, :], diff_lk1[l][None, :], diff_lq2[l][None, :],
                           diff_lk2[l][None, :], diff_subln_w[l][None, :], lam_init, batch, seq, tq)
        h_res = _outproj(h_res, mix_h, mix_s[0], mix_s[1], mix_a, w_out[l].astype(BF16), final_norm_w[None, :].astype(F32),
                         l == depth - 1, _pick(seq, 1024))
    return h_res.reshape(batch, seq, D_MODEL).astype(x.dtype)
```

```python
import functools
import math

import jax
import jax.numpy as jnp
from jax import lax
from jax.experimental import pallas as pl
from jax.experimental.pallas import tpu as pltpu

F32 = jnp.float32
BF16 = jnp.bfloat16

D_MODEL = 1024
HGRN_W = 256
HGRN_HD = 64
CHUNK = 16
HGRN_WR = 8 * 16 + 8 * 8
HGRN_PW = 2 * HGRN_HD
HGRN_GR = 128
LB_FLOOR = 1e-30
S5_W = 256
S5_C = 16
S5_G = S5_W // S5_C
S5_P = 64
S5_L = 16
S5_ST = 2 * S5_P
S5_SW = S5_G * S5_ST
S5_CW = S5_L * S5_W
S5_TG = 8
S5_HW = 128
ATT_W = 512
ATT_DH = 64
ATT_DV = 128
ATT_H = ATT_W // ATT_DV
ATT_CT = 256
ATT_VA = ATT_DV + 16
ATT_TQ = 2048
ATT_TK = 512
ROPE_DIM = 16
ROPE_THETA = 500000.0
MASK_VALUE = -1e30
Q_SCALE = ATT_DH ** -0.5 * math.log2(math.e)
EPS = 1e-6
D_IN = 4 * HGRN_W + 2 * S5_W + 4 * ATT_W
COL_S5 = 4 * HGRN_W
COL_AQ = COL_S5 + 2 * S5_W
COL_AK = COL_AQ + ATT_W
COL_AV = COL_AK + ATT_W
COL_AG = COL_AV + ATT_W

VMEM_LIMIT = 56 * 1024 * 1024


def _cparams(sem):
    return pltpu.CompilerParams(dimension_semantics=sem, vmem_limit_bytes=VMEM_LIMIT)


def _sigmoid(x):
    return 1.0 / (1.0 + jnp.exp(-x))


def _softplus_neg_abs(x):
    return jnp.log(1.0 + jnp.exp(-jnp.abs(x)))


def _silu(x):
    return x * _sigmoid(x)


def _dot(a, b):
    return jnp.dot(a, b, preferred_element_type=F32)


def _dot_split(a, b):
    ah = a.astype(BF16)
    al = (a - ah.astype(F32)).astype(BF16)
    bh = b.astype(BF16)
    bl = (b - bh.astype(F32)).astype(BF16)
    return _dot(ah, bh) + _dot(ah, bl) + _dot(al, bh)


def _rope(t, rc, ra, rb):
    return t * rc + pltpu.roll(t, 128 - ROPE_DIM // 2, axis=1) * ra + pltpu.roll(t, ROPE_DIM // 2, axis=1) * rb


def _inproj_kernel(x_ref, nw_ref, w_ref, rope_ref, h_ref, s_ref, q_ref, k_ref, v_ref, g_ref):
    x = x_ref[...]
    hn = x * lax.rsqrt(jnp.mean(x * x, axis=-1, keepdims=True) + EPS) * nw_ref[...]
    proj = _dot(hn.astype(BF16), w_ref[...])
    h_ref[...] = proj[:, :COL_S5]
    s_ref[...] = proj[:, COL_S5:COL_AQ]
    rc = rope_ref[:, 0:128]
    ra = rope_ref[:, 128:256]
    rb = rope_ref[:, 256:384]
    for h in range(ATT_H):
        lo = h * ATT_DV
        q = proj[:, COL_AQ + lo:COL_AQ + lo + ATT_DV]
        k = proj[:, COL_AK + lo:COL_AK + lo + ATT_DV]
        q_ref[:, lo:lo + ATT_DV] = (_rope(q, rc, ra, rb) * Q_SCALE).astype(BF16)
        k_ref[:, lo:lo + ATT_DV] = _rope(k, rc, ra, rb).astype(BF16)
    v_ref[...] = proj[:, COL_AV:COL_AG].astype(BF16)
    g_ref[...] = proj[:, COL_AG:]


def _rope_tables(seq):
    pos = jnp.arange(seq, dtype=F32)
    inv_freq = ROPE_THETA ** (-jnp.arange(0, ROPE_DIM, 2, dtype=F32) / ROPE_DIM)
    ang = pos[:, None] * inv_freq[None, :]
    cos, sin = jnp.cos(ang), jnp.sin(ang)
    half = ROPE_DIM // 2
    zeros = jnp.zeros((seq, ATT_DH - ROPE_DIM), F32)
    zh = jnp.zeros((seq, half), F32)
    rc = jnp.concatenate([cos, cos, jnp.ones((seq, ATT_DH - ROPE_DIM), F32)], axis=1)
    ra = jnp.concatenate([-sin, zh, zeros], axis=1)
    rb = jnp.concatenate([zh, sin, zeros], axis=1)
    return jnp.concatenate([jnp.tile(t, (1, 2)) for t in (rc, ra, rb)], axis=1)


def _inproj(x2, norm_w, w_bf, rope, seq, tm):
    t = x2.shape[0]
    nblk = seq // tm
    row = lambda i: (i, 0)
    return pl.pallas_call(
        _inproj_kernel,
        grid=(t // tm,),
        in_specs=[pl.BlockSpec((tm, D_MODEL), row),
                  pl.BlockSpec((1, D_MODEL), lambda i: (0, 0)),
                  pl.BlockSpec((D_MODEL, D_IN), lambda i: (0, 0)),
                  pl.BlockSpec((tm, 384), lambda i: (i % nblk, 0))],
        out_specs=[pl.BlockSpec((tm, COL_S5), row), pl.BlockSpec((tm, 2 * S5_W), row),
                   pl.BlockSpec((tm, ATT_W), row), pl.BlockSpec((tm, ATT_W), row),
                   pl.BlockSpec((tm, ATT_W), row), pl.BlockSpec((tm, ATT_W), row)],
        out_shape=[jax.ShapeDtypeStruct((t, COL_S5), F32), jax.ShapeDtypeStruct((t, 2 * S5_W), F32),
                   jax.ShapeDtypeStruct((t, ATT_W), BF16), jax.ShapeDtypeStruct((t, ATT_W), BF16),
                   jax.ShapeDtypeStruct((t, ATT_W), BF16), jax.ShapeDtypeStruct((t, ATT_W), F32)],
        compiler_params=_cparams(("parallel",)),
        name="inproj",
    )(x2, norm_w, w_bf, rope)


def _outproj_kernel(res_ref, mh_ref, ms0_ref, ms1_ref, ma_ref, w_ref, fw_ref, o_ref, *, final):
    acc = res_ref[...]
    acc = acc + _dot(mh_ref[...].astype(BF16), w_ref[0:HGRN_W, :])
    acc = acc + _dot(ms0_ref[...].astype(BF16), w_ref[HGRN_W:HGRN_W + S5_HW, :])
    acc = acc + _dot(ms1_ref[...].astype(BF16), w_ref[HGRN_W + S5_HW:HGRN_W + S5_W, :])
    acc = acc + _dot(ma_ref[...].astype(BF16), w_ref[HGRN_W + S5_W:, :])
    if final:
        acc = acc * lax.rsqrt(jnp.mean(acc * acc, axis=-1, keepdims=True) + EPS) * fw_ref[...]
    o_ref[...] = acc


def _outproj(res, mh, ms0, ms1, ma, w_bf, fw, final, tm):
    t = res.shape[0]
    row = lambda i: (i, 0)
    return pl.pallas_call(
        functools.partial(_outproj_kernel, final=final),
        grid=(t // tm,),
        in_specs=[pl.BlockSpec((tm, D_MODEL), row), pl.BlockSpec((tm, HGRN_W), row),
                  pl.BlockSpec((tm, S5_HW), row), pl.BlockSpec((tm, S5_HW), row), pl.BlockSpec((tm, ATT_W), row),
                  pl.BlockSpec((D_MODEL, D_MODEL), lambda i: (0, 0)),
                  pl.BlockSpec((1, D_MODEL), lambda i: (0, 0))],
        out_specs=pl.BlockSpec((tm, D_MODEL), row),
        out_shape=jax.ShapeDtypeStruct((t, D_MODEL), F32),
        compiler_params=_cparams(("parallel",)),
        name="outproj",
    )(res, mh, ms0, ms1, ma, w_bf, fw)


def _attn_kernel(q_ref, k_ref, v_ref, g_ref, lq1_ref, lk1_ref, lq2_ref, lk2_ref, sw_ref, o_ref,
                 vt_s, qq_s, acc_s, s0_s, s1_s, *, lam_init, tq, tk):
    i = pl.program_id(2)
    nkb = vt_s.shape[0]
    ntile = 2 * tq // ATT_CT
    per_map = tq // ATT_CT
    nt = (((1,), (1,)), ((), ()))

    @pl.when(i == 0)
    def _():
        for j in range(nkb):
            vt_s[j, 0:ATT_DV, :] = v_ref[j * tk:(j + 1) * tk, :].T
            vt_s[j, ATT_DV:, :] = jnp.ones((ATT_VA - ATT_DV, tk), BF16)

    q = q_ref[...]
    lane = lax.broadcasted_iota(jnp.int32, (tq, ATT_DV), 1)
    qq_s[0:tq, :] = jnp.where(lane < ATT_DH, q, jnp.zeros_like(q))
    qq_s[tq:2 * tq, :] = jnp.where(lane >= ATT_DH, q, jnp.zeros_like(q))
    acc_s[...] = jnp.zeros(acc_s.shape, F32)

    def tile_mode(c, koff):
        qs = (c % per_map) * ATT_CT
        if koff is None or qs >= koff + tk - 1:
            return "full"
        return "skip" if qs + ATT_CT - 1 < koff else "causal"

    def stage(kq, sq_s, koff_q, ssm_s, ksm, koff_sm, m_run, cmax):
        new_m, new_cmax = list(m_run), [None] * ntile
        if kq is not None:
            kb = k_ref[pl.ds(kq * tk if isinstance(kq, int) else pl.multiple_of(kq * tk, tk), tk), :]
        if ksm is not None:
            vtb = vt_s[ksm]
        for c in range(ntile):
            cols = slice(c * ATT_CT, (c + 1) * ATT_CT)
            if kq is not None and tile_mode(c, koff_q) != "skip":
                s = lax.dot_general(kb, qq_s[cols, :], nt, preferred_element_type=F32)
                if tile_mode(c, koff_q) == "causal":
                    kpos = lax.broadcasted_iota(jnp.int32, (tk, ATT_CT), 0) + koff_q
                    qpos = lax.broadcasted_iota(jnp.int32, (tk, ATT_CT), 1) + (c % per_map) * ATT_CT
                    s = jnp.where(kpos <= qpos, s, MASK_VALUE)
                sq_s[:, cols] = s
                new_cmax[c] = jnp.max(s, axis=0, keepdims=True)
            if ksm is not None and tile_mode(c, koff_sm) != "skip":
                m_new = jnp.maximum(m_run[c], cmax[c])
                alpha = jnp.exp2(m_run[c] - m_new)
                p = jnp.exp2(ssm_s[:, cols] - m_new)
                acc_s[:, cols] = alpha * acc_s[:, cols] + _dot(vtb, p.astype(BF16))
                new_m[c] = m_new
        return tuple(new_m), tuple(new_cmax)

    neg = tuple(jnp.full((1, ATT_CT), MASK_VALUE, F32) for _ in range(ntile))
    ndiag = tq // tk
    nfull = ndiag * i
    bufs = (s0_s, s1_s)
    fill = lambda cmx: tuple(neg[c] if x is None else x for c, x in enumerate(cmx))
    _, cm = stage(nfull, bufs[0], 0, None, None, None, neg, None)
    m_run = neg
    for m in range(1, ndiag):
        m_run, cm = stage(nfull + m, bufs[m % 2], m * tk, bufs[(m - 1) % 2], nfull + m - 1, (m - 1) * tk,
                          m_run, fill(cm))
    cm = fill(cm)
    last = ndiag - 1

    def pair(t, carry):
        m_run, cm0 = carry
        m_run, cm1 = stage(2 * t + 1, s1_s, None, s0_s, 2 * t, None, m_run, cm0)
        return stage(2 * t + 2, s0_s, None, s1_s, 2 * t + 1, None, m_run, cm1)

    @pl.when(i == 0)
    def _():
        stage(None, None, None, s1_s, nfull + last, last * tk, m_run, cm)

    @pl.when(i > 0)
    def _():
        m2, cm0 = stage(0, s0_s, None, s1_s, nfull + last, last * tk, m_run, cm)
        m2, cm0 = lax.fori_loop(0, nfull // 2 - 1, pair, (m2, cm0))
        m2, cml = stage(nfull - 1, s1_s, None, s0_s, nfull - 2, None, m2, cm0)
        stage(None, None, None, s1_s, nfull - 1, None, m2, cml)

    lam = (jnp.exp(jnp.sum(lq1_ref[...] * lk1_ref[...], axis=-1, keepdims=True))
           - jnp.exp(jnp.sum(lq2_ref[...] * lk2_ref[...], axis=-1, keepdims=True)) + lam_init)
    on = acc_s[0:ATT_DV, :] / acc_s[ATT_DV:ATT_DV + 1, :]
    ot = on[:, 0:tq] - lam * on[:, tq:2 * tq]
    o = ot.T
    o = o * lax.rsqrt(jnp.mean(o * o, axis=-1, keepdims=True) + EPS) * sw_ref[...] * (1.0 - lam_init)
    o_ref[...] = (o * _silu(g_ref[...])).astype(o_ref.dtype)


def _attention(q, k, v, g, lq1, lk1, lq2, lk2, sw, lam_init, batch, seq, tq):
    t = q.shape[0]
    tk = min(ATT_TK, tq // 2)
    nq = seq // tq
    qmap = lambda b, h, i: (b * nq + i, h)
    kvmap = lambda b, h, i: (b, h)
    cmap = lambda b, h, i: (0, 0)
    return pl.pallas_call(
        functools.partial(_attn_kernel, lam_init=lam_init, tq=tq, tk=tk),
        grid=(batch, ATT_H, nq),
        in_specs=[pl.BlockSpec((tq, ATT_DV), qmap), pl.BlockSpec((seq, ATT_DV), kvmap),
                  pl.BlockSpec((seq, ATT_DV), kvmap), pl.BlockSpec((tq, ATT_DV), qmap),
                  pl.BlockSpec((1, ATT_DH), cmap), pl.BlockSpec((1, ATT_DH), cmap),
                  pl.BlockSpec((1, ATT_DH), cmap), pl.BlockSpec((1, ATT_DH), cmap),
                  pl.BlockSpec((1, ATT_DV), cmap)],
        out_specs=pl.BlockSpec((tq, ATT_DV), qmap),
        out_shape=jax.ShapeDtypeStruct((t, ATT_W), BF16),
        scratch_shapes=[pltpu.VMEM((seq // tk, ATT_VA, tk), BF16), pltpu.VMEM((2 * tq, ATT_DV), BF16),
                        pltpu.VMEM((ATT_VA, 2 * tq), F32),
                        pltpu.VMEM((tk, 2 * tq), F32), pltpu.VMEM((tk, 2 * tq), F32)],
        compiler_params=_cparams(("parallel", "parallel", "arbitrary")),
        name="diffattn",
    )(q, k, v, g, lq1, lk1, lq2, lk2, sw)


def _head_ones(dtype):
    r = lax.broadcasted_iota(jnp.int32, (HGRN_W, HGRN_W), 0) // HGRN_HD
    c = lax.broadcasted_iota(jnp.int32, (HGRN_W, HGRN_W), 1) // HGRN_HD
    return (r == c).astype(dtype)


def _hgrn_kernel(h_ref, lbl_ref, nw_ref, o_ref, st_ref, q_s, v_s, c_s, cs_s, dec_s, oo_s, qt_s, kh_s, vt_s,
                 w_s, sc_s, *, layer, ts):
    nchunk = ts // CHUNK
    ngroup = ts // HGRN_GR
    npair = HGRN_W // HGRN_PW
    half = CHUNK // 2

    @pl.when(pl.program_id(1) == 0)
    def _():
        st_ref[...] = jnp.zeros(st_ref.shape, F32)

    lg = lbl_ref[...]
    e = jnp.exp(lg - jnp.max(lg, axis=0, keepdims=True))
    p = e / jnp.sum(e, axis=0, keepdims=True)
    lb = jnp.sum(p[0:layer + 1, :], axis=0, keepdims=True) - p[0:1, :]
    log2e = math.log2(math.e)
    log_lb = jnp.log(jnp.maximum(lb, LB_FLOOR)) * log2e
    log1m_lb = jnp.log1p(-lb) * log2e
    r16 = lax.broadcasted_iota(jnp.int32, (HGRN_GR, HGRN_W), 0) % CHUNK
    cpg = HGRN_GR // CHUNK
    by_chunk = lambda a: a.reshape(cpg, CHUNK, HGRN_W)

    def prepare(g, carry):
        rows = pl.ds(pl.multiple_of(g * HGRN_GR, HGRN_GR), HGRN_GR)
        x2 = h_ref[rows, HGRN_W:2 * HGRN_W] * log2e
        hi = h_ref[rows, 2 * HGRN_W:3 * HGRN_W]
        b = log1m_lb - (jnp.maximum(-x2, 0.0) + jnp.log2(1.0 + jnp.exp2(-jnp.abs(x2))))
        log_f = jnp.maximum(log_lb, b) + jnp.log2(1.0 + jnp.exp2(-jnp.abs(log_lb - b)))
        k = 1.0 - jnp.exp2(log_f)
        q = _silu(h_ref[rows, 0:HGRN_W])
        cum = log_f
        sh = 1
        while sh < CHUNK:
            cum = cum + jnp.where(r16 >= sh, pltpu.roll(cum, sh, axis=0), 0.0)
            sh *= 2
        cum = by_chunk(cum)
        last = jnp.broadcast_to(cum[:, CHUNK - 1:CHUNK, :], cum.shape)
        rem = last - cum
        k, q = by_chunk(k), by_chunk(q)
        chunks = pl.ds(pl.multiple_of(g * cpg, cpg), cpg)
        q_s[chunks] = q
        v_s[chunks] = by_chunk(hi)
        c_s[chunks] = cum
        cs_s[chunks] = cum - jnp.log2(k)
        qt_s[chunks] = (q * jnp.exp2(cum)).astype(BF16)
        dec_s[chunks] = jnp.exp2(last)
        kh_s[g] = (k * jnp.exp2(rem)).reshape(HGRN_GR, HGRN_W).astype(BF16)
        vt = hi.T.astype(BF16)
        for pr in range(npair):
            vt_s[pr, g] = vt[pr * HGRN_PW:(pr + 1) * HGRN_PW, :]
        return carry

    lax.fori_loop(0, ngroup, prepare, 0)

    t8 = lax.broadcasted_iota(jnp.int32, (half, HGRN_W), 0)

    def weights(n, carry):
        c0, c1 = c_s[n, 0:half, :], c_s[n, half:, :]
        q0, q1 = q_s[n, 0:half, :], q_s[n, half:, :]
        tiles = []
        for s in range(CHUNK):
            bs = jnp.broadcast_to(cs_s[n, s:s + 1, :], (half, HGRN_W))
            if s < half:
                d0 = jnp.exp2(c0 - bs)
                tiles.append(q0 * (jnp.where(t8 >= s, d0, 0.0) if s else d0))
                tiles.append(q1 * jnp.exp2(c1 - bs))
            else:
                d1 = jnp.exp2(c1 - bs)
                tiles.append(q1 * (jnp.where(t8 >= s - half, d1, 0.0) if s > half else d1))
        w_s[n] = jnp.concatenate(tiles, axis=0).astype(BF16)
        return carry

    lax.fori_loop(0, nchunk, weights, 0)
    sc_s[...] = _dot(w_s[...].reshape(nchunk * HGRN_WR, HGRN_W), _head_ones(BF16)).reshape(nchunk, HGRN_WR, HGRN_W)

    lane_chunk = lax.broadcasted_iota(jnp.int32, (HGRN_PW, HGRN_GR), 1) // CHUNK
    same_head = (lax.broadcasted_iota(jnp.int32, (HGRN_PW, HGRN_PW), 0) // HGRN_HD
                 == lax.broadcasted_iota(jnp.int32, (HGRN_PW, HGRN_PW), 1) // HGRN_HD)
    nt = (((1,), (1,)), ((), ()))

    def group(g, carry):
        state = list(carry)
        pcols = [slice(pr * HGRN_PW, (pr + 1) * HGRN_PW) for pr in range(npair)]
        upd = []
        for j in range(cpg):
            row = []
            for pr in range(npair):
                vt = vt_s[pr, g]
                lhs = jnp.where(lane_chunk == j, vt, jnp.zeros_like(vt))
                row.append(jnp.where(same_head, _dot(lhs, kh_s[g, :, pcols[pr]]), 0.0))
            upd.append(row)
        entering = []
        for j in range(cpg):
            n = g * cpg + j
            entering.append([st.astype(BF16) for st in state])
            state = [state[pr] * dec_s[n, 0:1, pcols[pr]] + upd[j][pr] for pr in range(npair)]
        for j in range(cpg):
            n = g * cpg + j
            o0 = jnp.zeros((half, HGRN_W), F32)
            o1 = jnp.zeros((half, HGRN_W), F32)
            for s in range(CHUNK):
                vb = jnp.broadcast_to(v_s[n, s:s + 1, :], (half, HGRN_W))
                if s < half:
                    o0 = o0 + sc_s[n, s * CHUNK:s * CHUNK + half, :] * vb
                    o1 = o1 + sc_s[n, s * CHUNK + half:(s + 1) * CHUNK, :] * vb
                else:
                    o1 = o1 + sc_s[n, half * CHUNK + (s - half) * half:half * CHUNK + (s - half + 1) * half, :] * vb
            oi = jnp.concatenate([lax.dot_general(qt_s[n, :, pcols[pr]], entering[j][pr], nt,
                                                  preferred_element_type=F32) for pr in range(npair)], axis=1)
            oo_s[n, 0:half, :] = o0 + oi[0:half]
            oo_s[n, half:, :] = o1 + oi[half:]
        return tuple(state)

    state = lax.fori_loop(0, ngroup, group, tuple(st_ref[pr] for pr in range(npair)))
    for pr in range(npair):
        st_ref[pr] = state[pr]

    def finish(g, carry):
        rows = pl.ds(pl.multiple_of(g * HGRN_GR, HGRN_GR), HGRN_GR)
        o = oo_s[pl.ds(pl.multiple_of(g * cpg, cpg), cpg)].reshape(HGRN_GR, HGRN_W)
        sq = o * o
        sq_hi = sq.astype(BF16)
        ones = _head_ones(BF16)
        ms = (_dot(sq_hi, ones) + _dot((sq - sq_hi.astype(F32)).astype(BF16), ones)) * (1.0 / HGRN_HD)
        hg = h_ref[rows, 3 * HGRN_W:4 * HGRN_W]
        o_ref[rows, :] = (o * lax.rsqrt(ms + EPS) * nw_ref[...] * _silu(hg)).astype(o_ref.dtype)
        return carry

    lax.fori_loop(0, ngroup, finish, 0)


def _hgrn(hproj, lb_logits, nw_tiled, layer, batch, seq, ts):
    t = hproj.shape[0]
    ns = seq // ts
    depth = lb_logits.shape[0]
    nchunk = ts // CHUNK
    npair = HGRN_W // HGRN_PW
    blk = pltpu.VMEM((nchunk, CHUNK, HGRN_W), F32)
    return pl.pallas_call(
        functools.partial(_hgrn_kernel, layer=layer, ts=ts),
        grid=(batch, ns),
        in_specs=[pl.BlockSpec((ts, 4 * HGRN_W), lambda b, i: (b * ns + i, 0)),
                  pl.BlockSpec((depth, HGRN_W), lambda b, i: (0, 0)),
                  pl.BlockSpec((1, HGRN_W), lambda b, i: (0, 0))],
        out_specs=pl.BlockSpec((ts, HGRN_W), lambda b, i: (b * ns + i, 0)),
        out_shape=jax.ShapeDtypeStruct((t, HGRN_W), BF16),
        scratch_shapes=[pltpu.VMEM((npair, HGRN_PW, HGRN_PW), F32),
                        blk, blk, blk, blk, blk, blk,
                        pltpu.VMEM((nchunk, CHUNK, HGRN_W), BF16),
                        pltpu.VMEM((ts // HGRN_GR, HGRN_GR, HGRN_W), BF16),
                        pltpu.VMEM((npair, ts // HGRN_GR, HGRN_PW, HGRN_GR), BF16),
                        pltpu.VMEM((nchunk, HGRN_WR, HGRN_W), BF16), pltpu.VMEM((nchunk, HGRN_WR, HGRN_W), F32)],
        compiler_params=_cparams(("parallel", "arbitrary")),
        name="hgrn2",
    )(hproj, lb_logits, nw_tiled)


def _s5_param_kernel(prow_ref, pcol_ref, bt_ref, btile_ref, ctile_ref, ks_ref, gs_ref, hs_ref, d_ref):
    g = pl.program_id(0)
    are, aim = prow_ref[0:1, :], prow_ref[1:2, :]
    dt = jnp.exp(prow_ref[2:3, :])
    lr, li = dt * are, dt * aim
    mag = jnp.exp(lr)
    abr, abi = mag * jnp.cos(li), mag * jnp.sin(li)
    den = are * are + aim * aim
    nr, ni = abr - 1.0, abi
    zr = (nr * are + ni * aim) / den
    zi = (ni * are - nr * aim) / den

    arec, aimc = pcol_ref[:, 0:1], pcol_ref[:, 1:2]
    dtc = jnp.exp(pcol_ref[:, 2:3])
    lrc, lic = dtc * arec, dtc * aimc
    cr, ci = ctile_ref[0], ctile_ref[1]
    wide = S5_L * S5_C
    tl = (lax.broadcasted_iota(jnp.int32, (S5_P, wide), 1) // S5_C).astype(F32)

    def powers(n):
        mg = jnp.exp(n * lrc)
        return mg * jnp.cos(n * lic), mg * jnp.sin(n * lic)

    er, ei = powers(tl)
    w1 = er * cr - ei * ci
    w2 = -(ei * cr + er * ci)
    zr1, zi1 = zr[:, 0:S5_P], zi[:, 0:S5_P]
    bbr = zr1 * bt_ref[0] - zi1 * bt_ref[1]
    bbi = zr1 * bt_ref[1] + zi1 * bt_ref[0]
    r0 = _dot_split(bbr, w1) + _dot_split(bbi, w2)
    mine = lax.broadcasted_iota(jnp.int32, (S5_C, wide), 1) // S5_C == g
    for j in range(S5_L):
        shift = (g * S5_C + (j + 1) * S5_C) % wide
        ks_ref[j] = jnp.where(mine, pltpu.roll(r0, shift, axis=1), 0.0).astype(ks_ref.dtype)

    e = (S5_L - 1 - lax.broadcasted_iota(jnp.int32, (wide, S5_ST), 0) // S5_C).astype(F32)
    mg = jnp.exp(e * lr)
    fr, fi = mg * jnp.cos(e * li), mg * jnp.sin(e * li)
    btr = zr * btile_ref[0] - zi * btile_ref[1]
    bti = zr * btile_ref[1] + zi * btile_ref[0]
    re_half = lax.broadcasted_iota(jnp.int32, (wide, S5_ST), 1) < S5_P
    g128 = jnp.where(re_half, fr * btr - fi * bti, fr * bti + fi * btr)
    tile = lax.broadcasted_iota(jnp.int32, (wide, S5_SW), 1) // S5_ST
    gfull = jnp.where(tile == g, jnp.concatenate([g128] * S5_G, axis=1), 0.0)
    gs_ref[...] = gfull.reshape(S5_L, S5_C, S5_SW).astype(gs_ref.dtype)

    pr, pi = powers(tl + 1.0)
    hcat = jnp.concatenate([cr * pr - ci * pi, -(cr * pi + ci * pr)], axis=0)
    mine_h = lax.broadcasted_iota(jnp.int32, (S5_ST, wide), 1) // S5_C == g
    for t in range(S5_L):
        shift = (g * S5_C + wide - t * S5_C) % wide
        hs_ref[t] = jnp.where(mine_h, pltpu.roll(hcat, shift, axis=1), 0.0).astype(hs_ref.dtype)

    mg = jnp.exp(float(S5_L) * lr)
    d_ref[0:1, :] = mg * jnp.cos(float(S5_L) * li)
    d_ref[1:2, :] = mg * jnp.sin(float(S5_L) * li)


def _s5_params(prow, pcol, bt, btile, ctile):
    g = prow.shape[0]
    wide = S5_L * S5_C
    m3 = lambda i: (i, 0, 0)
    m4 = lambda i: (i, 0, 0, 0)
    g4 = lambda i: (0, i, 0, 0)
    return pl.pallas_call(
        _s5_param_kernel,
        grid=(g,),
        in_specs=[pl.BlockSpec((None, 3, S5_ST), m3), pl.BlockSpec((None, S5_P, 3), m3),
                  pl.BlockSpec((None, 2, S5_C, S5_P), m4), pl.BlockSpec((None, 2, wide, S5_ST), m4),
                  pl.BlockSpec((None, 2, S5_P, wide), m4)],
        out_specs=[pl.BlockSpec((S5_L, None, S5_C, wide), g4), pl.BlockSpec((S5_L, None, S5_C, S5_SW), g4),
                   pl.BlockSpec((S5_L, None, S5_ST, wide), g4), pl.BlockSpec((None, 2, S5_ST), m3)],
        out_shape=[jax.ShapeDtypeStruct((S5_L, g, S5_C, wide), BF16), jax.ShapeDtypeStruct((S5_L, g, S5_C, S5_SW), BF16),
                   jax.ShapeDtypeStruct((S5_L, g, S5_ST, wide), BF16), jax.ShapeDtypeStruct((g, 2, S5_ST), F32)],
        compiler_params=_cparams(("parallel",)),
        name="s5_params",
    )(prow, pcol, bt, btile, ctile)


def _s5_gather_chunks(u_refs, ucat_s, n):
    for tau in range(S5_L):
        for k, u_ref in enumerate(u_refs):
            lo = tau * S5_W + k * S5_HW
            ucat_s[:, lo:lo + S5_HW] = u_ref[pl.ds(tau, n, stride=S5_L), :].astype(BF16)


def _s5_state_kernel(u0_ref, u1_ref, gs_ref, d_ref, xs_ref, ucat_s, *, nchunk):
    @pl.when(pl.program_id(1) == 0)
    def _():
        _s5_gather_chunks((u0_ref, u1_ref), ucat_s, nchunk)

    width = xs_ref.shape[1]
    x = _dot(ucat_s[...], gs_ref[...])
    sgn = jnp.where(lax.broadcasted_iota(jnp.int32, (1, width), 1) % S5_ST < S5_P, -1.0, 1.0)
    rn = lax.broadcasted_iota(jnp.int32, (nchunk, width), 0)
    pr, pi = d_ref[0:1, :], d_ref[1:2, :]

    def swap(a):
        return jnp.concatenate([pltpu.roll(a[:, k * S5_ST:(k + 1) * S5_ST], S5_P, axis=1)
                                for k in range(width // S5_ST)], axis=1)

    sh = 1
    while sh < nchunk:
        sx = jnp.where(rn >= sh, pltpu.roll(x, sh, axis=0), 0.0)
        x = x + pr * sx + (pi * sgn) * swap(sx)
        pr, pi = pr * pr - pi * pi, 2.0 * pr * pi
        sh *= 2
    xs_ref[...] = jnp.where(rn >= 1, pltpu.roll(x, 1, axis=0), 0.0).astype(xs_ref.dtype)


def _s5_state(sproj, gs, d, batch, seq):
    nchunk = seq // S5_L
    ntile = S5_SW // S5_W
    return pl.pallas_call(
        functools.partial(_s5_state_kernel, nchunk=nchunk),
        grid=(batch, ntile),
        in_specs=[pl.BlockSpec((seq, S5_HW), lambda b, j: (b, 0)), pl.BlockSpec((seq, S5_HW), lambda b, j: (b, 1)),
                  pl.BlockSpec((S5_CW, S5_W), lambda b, j: (0, j)),
                  pl.BlockSpec((2, S5_W), lambda b, j: (0, j))],
        out_specs=pl.BlockSpec((nchunk, S5_W), lambda b, j: (b, j)),
        out_shape=jax.ShapeDtypeStruct((batch * nchunk, S5_SW), BF16),
        scratch_shapes=[pltpu.VMEM((nchunk, S5_CW), BF16)],
        compiler_params=_cparams(("parallel", "arbitrary")),
        name="s5_state",
    )(sproj, sproj, gs, d)


def _s5_out_kernel(u0_ref, u1_ref, g0_ref, g1_ref, xs_ref, ks_ref, hs_ref, dsk_ref, w_ref, b_ref,
                   o0_ref, o1_ref, ucat_s, *, nb):
    step = pl.program_id(1)

    @pl.when(step == 0)
    def _():
        _s5_gather_chunks((u0_ref, u1_ref), ucat_s, nb)

    def response(tt, k):
        return (_dot(ucat_s[:, 0:(tt + 1) * S5_W], ks_ref[(S5_L - 1 - tt) * S5_W:, :])
                + _dot(xs_ref[...], hs_ref[k]))

    def finish(tt, y):
        rows = pl.ds(tt, nb, stride=S5_L)
        cat = lambda a, b: jnp.concatenate([a[rows, :], b[rows, :]], axis=1)
        y = y + dsk_ref[...] * cat(u0_ref, u1_ref)
        y = y * (0.5 * (1.0 + jnp.tanh(math.sqrt(2.0 / math.pi) * (y + 0.044715 * (y * y * y)))))
        z = _dot(y.astype(BF16), w_ref[...]) + b_ref[...]
        out = y * _sigmoid(z) * _silu(cat(g0_ref, g1_ref))
        o0_ref[rows, :] = out[:, 0:S5_HW]
        o1_ref[rows, :] = out[:, S5_HW:]

    for sg in range(S5_L // S5_TG):
        @pl.when(step == sg)
        def _():
            ys = [response(sg * S5_TG + k, k) for k in range(S5_TG)]
            for k in range(S5_TG):
                finish(sg * S5_TG + k, ys[k])


def _s5_out(sproj, xs, ks, hs, d_skip, w_bf, b, rb):
    t = sproj.shape[0]
    nb = rb // S5_L
    cst = lambda r, i: (0, 0)
    return pl.pallas_call(
        functools.partial(_s5_out_kernel, nb=nb),
        grid=(t // rb, S5_L // S5_TG),
        in_specs=[pl.BlockSpec((rb, S5_HW), lambda r, i: (r, 0)), pl.BlockSpec((rb, S5_HW), lambda r, i: (r, 1)),
                  pl.BlockSpec((rb, S5_HW), lambda r, i: (r, 2)), pl.BlockSpec((rb, S5_HW), lambda r, i: (r, 3)),
                  pl.BlockSpec((nb, S5_SW), lambda r, i: (r, 0)),
                  pl.BlockSpec((S5_CW, S5_W), cst),
                  pl.BlockSpec((S5_TG, S5_SW, S5_W), lambda r, i: (i, 0, 0)),
                  pl.BlockSpec((1, S5_W), cst), pl.BlockSpec((S5_W, S5_W), cst), pl.BlockSpec((1, S5_W), cst)],
        out_specs=[pl.BlockSpec((rb, S5_HW), lambda r, i: (r, 0))] * 2,
        out_shape=[jax.ShapeDtypeStruct((t, S5_HW), F32)] * 2,
        scratch_shapes=[pltpu.VMEM((nb, S5_CW), BF16)],
        compiler_params=_cparams(("parallel", "arbitrary")),
        name="s5_out",
    )(sproj, sproj, sproj, sproj, xs, ks, hs, d_skip, w_bf, b)


def _s5(sproj, a_re, a_im, b_re, b_im, c_re, c_im, log_dt, d_skip, glu_w, glu_b, batch, seq, rb):
    ldt = jnp.broadcast_to(log_dt[:, None], a_re.shape)
    twice = lambda a: jnp.concatenate([a, a], axis=-1)
    prow = twice(jnp.stack([a_re, a_im, ldt], axis=1))
    pcol = jnp.stack([a_re, a_im, ldt], axis=2)
    bt = jnp.stack([b_re, b_im], axis=1).transpose(0, 1, 3, 2)
    btile = twice(jnp.tile(bt, (1, 1, S5_L, 1)))
    ctile = jnp.tile(jnp.stack([c_re, c_im], axis=1).transpose(0, 1, 3, 2), (1, 1, 1, S5_L))
    ks, gs, hs, d = _s5_params(prow, pcol, bt, btile, ctile)
    ks = ks.reshape(S5_CW, S5_W)
    gs = gs.reshape(S5_CW, S5_SW)
    hs = hs.reshape(S5_L, S5_SW, S5_W)
    d = d.transpose(1, 0, 2).reshape(2, S5_SW)
    xs = _s5_state(sproj, gs, d, batch, seq)
    return _s5_out(sproj, xs, ks, hs, d_skip[None, :], glu_w.astype(BF16), glu_b[None, :], rb)


def _pick(n, pref):
    b = min(n, pref)
    while n % b:
        b //= 2
    return b


def kernel(x, norm_w, w_in, w_out, hgrn_lb_logits, hgrn_norm_w, s5_a_re, s5_a_im, s5_b_re, s5_b_im, s5_c_re, s5_c_im, s5_log_dt, s5_d, s5_glu_w, s5_glu_b, diff_lq1, diff_lk1, diff_lq2, diff_lk2, diff_subln_w, final_norm_w):
    batch, seq, _ = x.shape
    depth = norm_w.shape[0]
    t = batch * seq
    tm = _pick(seq, 512)
    tq = _pick(seq, ATT_TQ)
    ts = _pick(seq, 512)
    rope = _rope_tables(seq)
    h_res = x.astype(F32).reshape(t, D_MODEL)
    for l in range(depth):
        hp, sp, aq, ak, av, ag = _inproj(h_res, norm_w[l][None, :].astype(F32), w_in[l].astype(BF16), rope, seq, tm)
        mix_h = _hgrn(hp, hgrn_lb_logits.astype(F32), jnp.tile(hgrn_norm_w[l].astype(F32), HGRN_W // HGRN_HD)[None, :],
                      l, batch, seq, ts)
        mix_s = _s5(sp, s5_a_re[l], s5_a_im[l], s5_b_re[l], s5_b_im[l], s5_c_re[l], s5_c_im[l], s5_log_dt[l],
                    s5_d[l], s5_glu_w[l], s5_glu_b[l], batch, seq, _pick(seq, 4096))
        lam_init = 0.8 - 0.6 * math.exp(-0.3 * l)
        mix_a = _attention(aq, ak, av, ag, diff_lq1[l][None, :], diff_lk1[l][None, :], diff_lq2[l][None, :],
                           diff_lk2[l][None, :], diff_subln_w[l][None, :], lam_init, batch, seq, tq)
        h_res = _outproj(h_res, mix_h, mix_s[0], mix_s[1], mix_a, w_out[l].astype(BF16), final_norm_w[None, :].astype(F32),
                         l == depth - 1, _pick(seq, 1024))
    return h_res.reshape(batch, seq, D_MODEL).astype(x.dtype)
```

```python
import functools
import math

import jax
import jax.numpy as jnp
from jax import lax
from jax.experimental import pallas as pl
from jax.experimental.pallas import tpu as pltpu

F32 = jnp.float32
BF16 = jnp.bfloat16

D_MODEL = 1024
HGRN_W = 256
HGRN_HD = 64
CHUNK = 16
HGRN_WR = 8 * 16 + 8 * 8
HGRN_PW = 2 * HGRN_HD
HGRN_GR = 128
LB_FLOOR = 1e-30
S5_W = 256
S5_C = 16
S5_G = S5_W // S5_C
S5_P = 64
S5_L = 16
S5_ST = 2 * S5_P
S5_SW = S5_G * S5_ST
S5_CW = S5_L * S5_W
S5_TG = 8
S5_HW = 128
ATT_W = 512
ATT_DH = 64
ATT_DV = 128
ATT_H = ATT_W // ATT_DV
ATT_CT = 256
ATT_VA = ATT_DV + 16
ATT_TQ = 2048
ATT_TK = 512
ROPE_DIM = 16
ROPE_THETA = 500000.0
MASK_VALUE = -1e30
Q_SCALE = ATT_DH ** -0.5 * math.log2(math.e)
EPS = 1e-6
D_IN = 4 * HGRN_W + 2 * S5_W + 4 * ATT_W
COL_S5 = 4 * HGRN_W
COL_AQ = COL_S5 + 2 * S5_W
COL_AK = COL_AQ + ATT_W
COL_AV = COL_AK + ATT_W
COL_AG = COL_AV + ATT_W

VMEM_LIMIT = 56 * 1024 * 1024


def _cparams(sem):
    return pltpu.CompilerParams(dimension_semantics=sem, vmem_limit_bytes=VMEM_LIMIT)


def _sigmoid(x):
    return 1.0 / (1.0 + jnp.exp(-x))


def _softplus_neg_abs(x):
    return jnp.log(1.0 + jnp.exp(-jnp.abs(x)))


def _silu(x):
    return x * _sigmoid(x)


def _dot(a, b):
    return jnp.dot(a, b, preferred_element_type=F32)


def _dot_split(a, b):
    ah = a.astype(BF16)
    al = (a - ah.astype(F32)).astype(BF16)
    bh = b.astype(BF16)
    bl = (b - bh.astype(F32)).astype(BF16)
    return _dot(ah, bh) + _dot(ah, bl) + _dot(al, bh)


def _rope(t, rc, ra, rb):
    return t * rc + pltpu.roll(t, 128 - ROPE_DIM // 2, axis=1) * ra + pltpu.roll(t, ROPE_DIM // 2, axis=1) * rb


def _inproj_kernel(x_ref, nw_ref, w_ref, rope_ref, h_ref, s_ref, q_ref, k_ref, v_ref, g_ref):
    x = x_ref[...]
    hn = x * lax.rsqrt(jnp.mean(x * x, axis=-1, keepdims=True) + EPS) * nw_ref[...]
    proj = _dot(hn.astype(BF16), w_ref[...])
    h_ref[...] = proj[:, :COL_S5]
    s_ref[...] = proj[:, COL_S5:COL_AQ]
    rc = rope_ref[:, 0:128]
    ra = rope_ref[:, 128:256]
    rb = rope_ref[:, 256:384]
    for h in range(ATT_H):
        lo = h * ATT_DV
        q = proj[:, COL_AQ + lo:COL_AQ + lo + ATT_DV]
        k = proj[:, COL_AK + lo:COL_AK + lo + ATT_DV]
        q_ref[:, lo:lo + ATT_DV] = (_rope(q, rc, ra, rb) * Q_SCALE).astype(BF16)
        k_ref[:, lo:lo + ATT_DV] = _rope(k, rc, ra, rb).astype(BF16)
    v_ref[...] = proj[:, COL_AV:COL_AG].astype(BF16)
    g_ref[...] = proj[:, COL_AG:]


def _rope_tables(seq):
    pos = jnp.arange(seq, dtype=F32)
    inv_freq = ROPE_THETA ** (-jnp.arange(0, ROPE_DIM, 2, dtype=F32) / ROPE_DIM)
    ang = pos[:, None] * inv_freq[None, :]
    cos, sin = jnp.cos(ang), jnp.sin(ang)
    half = ROPE_DIM // 2
    zeros = jnp.zeros((seq, ATT_DH - ROPE_DIM), F32)
    zh = jnp.zeros((seq, half), F32)
    rc = jnp.concatenate([cos, cos, jnp.ones((seq, ATT_DH - ROPE_DIM), F32)], axis=1)
    ra = jnp.concatenate([-sin, zh, zeros], axis=1)
    rb = jnp.concatenate([zh, sin, zeros], axis=1)
    return jnp.concatenate([jnp.tile(t, (1, 2)) for t in (rc, ra, rb)], axis=1)


def _inproj(x2, norm_w, w_bf, rope, seq, tm):
    t = x2.shape[0]
    nblk = seq // tm
    row = lambda i: (i, 0)
    return pl.pallas_call(
        _inproj_kernel,
        grid=(t // tm,),
        in_specs=[pl.BlockSpec((tm, D_MODEL), row),
                  pl.BlockSpec((1, D_MODEL), lambda i: (0, 0)),
                  pl.BlockSpec((D_MODEL, D_IN), lambda i: (0, 0)),
                  pl.BlockSpec((tm, 384), lambda i: (i % nblk, 0))],
        out_specs=[pl.BlockSpec((tm, COL_S5), row), pl.BlockSpec((tm, 2 * S5_W), row),
                   pl.BlockSpec((tm, ATT_W), row), pl.BlockSpec((tm, ATT_W), row),
                   pl.BlockSpec((tm, ATT_W), row), pl.BlockSpec((tm, ATT_W), row)],
        out_shape=[jax.ShapeDtypeStruct((t, COL_S5), F32), jax.ShapeDtypeStruct((t, 2 * S5_W), F32),
                   jax.ShapeDtypeStruct((t, ATT_W), BF16), jax.ShapeDtypeStruct((t, ATT_W), BF16),
                   jax.ShapeDtypeStruct((t, ATT_W), BF16), jax.ShapeDtypeStruct((t, ATT_W), F32)],
        compiler_params=_cparams(("parallel",)),
        name="inproj",
    )(x2, norm_w, w_bf, rope)


def _outproj_kernel(res_ref, mh_ref, ms0_ref, ms1_ref, ma_ref, w_ref, fw_ref, o_ref, *, final):
    acc = res_ref[...]
    acc = acc + _dot(mh_ref[...].astype(BF16), w_ref[0:HGRN_W, :])
    acc = acc + _dot(ms0_ref[...].astype(BF16), w_ref[HGRN_W:HGRN_W + S5_HW, :])
    acc = acc + _dot(ms1_ref[...].astype(BF16), w_ref[HGRN_W + S5_HW:HGRN_W + S5_W, :])
    acc = acc + _dot(ma_ref[...].astype(BF16), w_ref[HGRN_W + S5_W:, :])
    if final:
        acc = acc * lax.rsqrt(jnp.mean(acc * acc, axis=-1, keepdims=True) + EPS) * fw_ref[...]
    o_ref[...] = acc


def _outproj(res, mh, ms0, ms1, ma, w_bf, fw, final, tm):
    t = res.shape[0]
    row = lambda i: (i, 0)
    return pl.pallas_call(
        functools.partial(_outproj_kernel, final=final),
        grid=(t // tm,),
        in_specs=[pl.BlockSpec((tm, D_MODEL), row), pl.BlockSpec((tm, HGRN_W), row),
                  pl.BlockSpec((tm, S5_HW), row), pl.BlockSpec((tm, S5_HW), row), pl.BlockSpec((tm, ATT_W), row),
                  pl.BlockSpec((D_MODEL, D_MODEL), lambda i: (0, 0)),
                  pl.BlockSpec((1, D_MODEL), lambda i: (0, 0))],
        out_specs=pl.BlockSpec((tm, D_MODEL), row),
        out_shape=jax.ShapeDtypeStruct((t, D_MODEL), F32),
        compiler_params=_cparams(("parallel",)),
        name="outproj",
    )(res, mh, ms0, ms1, ma, w_bf, fw)


def _attn_kernel(q_ref, k_ref, v_ref, g_ref, lq1_ref, lk1_ref, lq2_ref, lk2_ref, sw_ref, o_ref,
                 vt_s, qq_s, acc_s, s0_s, s1_s, *, lam_init, tq, tk):
    i = pl.program_id(2)
    nkb = vt_s.shape[0]
    ntile = 2 * tq // ATT_CT
    per_map = tq // ATT_CT
    nt = (((1,), (1,)), ((), ()))

    @pl.when(i == 0)
    def _():
        for j in range(nkb):
            vt_s[j, 0:ATT_DV, :] = v_ref[j * tk:(j + 1) * tk, :].T
            vt_s[j, ATT_DV:, :] = jnp.ones((ATT_VA - ATT_DV, tk), BF16)

    q = q_ref[...]
    lane = lax.broadcasted_iota(jnp.int32, (tq, ATT_DV), 1)
    qq_s[0:tq, :] = jnp.where(lane < ATT_DH, q, jnp.zeros_like(q))
    qq_s[tq:2 * tq, :] = jnp.where(lane >= ATT_DH, q, jnp.zeros_like(q))
    acc_s[...] = jnp.zeros(acc_s.shape, F32)

    def tile_mode(c, koff):
        qs = (c % per_map) * ATT_CT
        if koff is None or qs >= koff + tk - 1:
            return "full"
        return "skip" if qs + ATT_CT - 1 < koff else "causal"

    def stage(kq, sq_s, koff_q, ssm_s, ksm, koff_sm, m_run, cmax):
        new_m, new_cmax = list(m_run), [None] * ntile
        if kq is not None:
            kb = k_ref[pl.ds(kq * tk if isinstance(kq, int) else pl.multiple_of(kq * tk, tk), tk), :]
        if ksm is not None:
            vtb = vt_s[ksm]
        for c in range(ntile):
            cols = slice(c * ATT_CT, (c + 1) * ATT_CT)
            if kq is not None and tile_mode(c, koff_q) != "skip":
                s = lax.dot_general(kb, qq_s[cols, :], nt, preferred_element_type=F32)
                if tile_mode(c, koff_q) == "causal":
                    kpos = lax.broadcasted_iota(jnp.int32, (tk, ATT_CT), 0) + koff_q
                    qpos = lax.broadcasted_iota(jnp.int32, (tk, ATT_CT), 1) + (c % per_map) * ATT_CT
                    s = jnp.where(kpos <= qpos, s, MASK_VALUE)
                sq_s[:, cols] = s
                new_cmax[c] = jnp.max(s, axis=0, keepdims=True)
            if ksm is not None and tile_mode(c, koff_sm) != "skip":
                m_new = jnp.maximum(m_run[c], cmax[c])
                alpha = jnp.exp2(m_run[c] - m_new)
                p = jnp.exp2(ssm_s[:, cols] - m_new)
                acc_s[:, cols] = alpha * acc_s[:, cols] + _dot(vtb, p.astype(BF16))
                new_m[c] = m_new
        return tuple(new_m), tuple(new_cmax)

    neg = tuple(jnp.full((1, ATT_CT), MASK_VALUE, F32) for _ in range(ntile))
    ndiag = tq // tk
    nfull = ndiag * i
    bufs = (s0_s, s1_s)
    fill = lambda cmx: tuple(neg[c] if x is None else x for c, x in enumerate(cmx))
    _, cm = stage(nfull, bufs[0], 0, None, None, None, neg, None)
    m_run = neg
    for m in range(1, ndiag):
        m_run, cm = stage(nfull + m, bufs[m % 2], m * tk, bufs[(m - 1) % 2], nfull + m - 1, (m - 1) * tk,
                          m_run, fill(cm))
    cm = fill(cm)
    last = ndiag - 1

    def pair(t, carry):
        m_run, cm0 = carry
        m_run, cm1 = stage(2 * t + 1, s1_s, None, s0_s, 2 * t, None, m_run, cm0)
        return stage(2 * t + 2, s0_s, None, s1_s, 2 * t + 1, None, m_run, cm1)

    @pl.when(i == 0)
    def _():
        stage(None, None, None, s1_s, nfull + last, last * tk, m_run, cm)

    @pl.when(i > 0)
    def _():
        m2, cm0 = stage(0, s0_s, None, s1_s, nfull + last, last * tk, m_run, cm)
        m2, cm0 = lax.fori_loop(0, nfull // 2 - 1, pair, (m2, cm0))
        m2, cml = stage(nfull - 1, s1_s, None, s0_s, nfull - 2, None, m2, cm0)
        stage(None, None, None, s1_s, nfull - 1, None, m2, cml)

    lam = (jnp.exp(jnp.sum(lq1_ref[...] * lk1_ref[...], axis=-1, keepdims=True))
           - jnp.exp(jnp.sum(lq2_ref[...] * lk2_ref[...], axis=-1, keepdims=True)) + lam_init)
    on = acc_s[0:ATT_DV, :] / acc_s[ATT_DV:ATT_DV + 1, :]
    ot = on[:, 0:tq] - lam * on[:, tq:2 * tq]
    o = ot.T
    o = o * lax.rsqrt(jnp.mean(o * o, axis=-1, keepdims=True) + EPS) * sw_ref[...] * (1.0 - lam_init)
    o_ref[...] = (o * _silu(g_ref[...])).astype(o_ref.dtype)


def _attention(q, k, v, g, lq1, lk1, lq2, lk2, sw, lam_init, batch, seq, tq):
    t = q.shape[0]
    tk = min(ATT_TK, tq // 2)
    nq = seq // tq
    qmap = lambda b, h, i: (b * nq + i, h)
    kvmap = lambda b, h, i: (b, h)
    cmap = lambda b, h, i: (0, 0)
    return pl.pallas_call(
        functools.partial(_attn_kernel, lam_init=lam_init, tq=tq, tk=tk),
        grid=(batch, ATT_H, nq),
        in_specs=[pl.BlockSpec((tq, ATT_DV), qmap), pl.BlockSpec((seq, ATT_DV), kvmap),
                  pl.BlockSpec((seq, ATT_DV), kvmap), pl.BlockSpec((tq, ATT_DV), qmap),
                  pl.BlockSpec((1, ATT_DH), cmap), pl.BlockSpec((1, ATT_DH), cmap),
                  pl.BlockSpec((1, ATT_DH), cmap), pl.BlockSpec((1, ATT_DH), cmap),
                  pl.BlockSpec((1, ATT_DV), cmap)],
        out_specs=pl.BlockSpec((tq, ATT_DV), qmap),
        out_shape=jax.ShapeDtypeStruct((t, ATT_W), BF16),
        scratch_shapes=[pltpu.VMEM((seq // tk, ATT_VA, tk), BF16), pltpu.VMEM((2 * tq, ATT_DV), BF16),
                        pltpu.VMEM((ATT_VA, 2 * tq), F32),
                        pltpu.VMEM((tk, 2 * tq), F32), pltpu.VMEM((tk, 2 * tq), F32)],
        compiler_params=_cparams(("parallel", "parallel", "arbitrary")),
        name="diffattn",
    )(q, k, v, g, lq1, lk1, lq2, lk2, sw)


def _head_ones(dtype):
    r = lax.broadcasted_iota(jnp.int32, (HGRN_W, HGRN_W), 0) // HGRN_HD
    c = lax.broadcasted_iota(jnp.int32, (HGRN_W, HGRN_W), 1) // HGRN_HD
    return (r == c).astype(dtype)


def _hgrn_kernel(h_ref, lbl_ref, nw_ref, o_ref, st_ref, q_s, v_s, c_s, cs_s, oo_s, dect_s, qt_s, vb_s, kt_s,
                 w_s, sc_s, *, layer, ts):
    nchunk = ts // CHUNK
    ngroup = ts // HGRN_GR
    npair = HGRN_W // HGRN_PW
    half = CHUNK // 2

    @pl.when(pl.program_id(1) == 0)
    def _():
        st_ref[...] = jnp.zeros(st_ref.shape, F32)

    lg = lbl_ref[...]
    e = jnp.exp(lg - jnp.max(lg, axis=0, keepdims=True))
    p = e / jnp.sum(e, axis=0, keepdims=True)
    lb = jnp.sum(p[0:layer + 1, :], axis=0, keepdims=True) - p[0:1, :]
    log2e = math.log2(math.e)
    log_lb = jnp.log(jnp.maximum(lb, LB_FLOOR)) * log2e
    log1m_lb = jnp.log1p(-lb) * log2e
    r16 = lax.broadcasted_iota(jnp.int32, (HGRN_GR, HGRN_W), 0) % CHUNK
    cpg = HGRN_GR // CHUNK
    by_chunk = lambda a: a.reshape(cpg, CHUNK, HGRN_W)

    def prepare(g, carry):
        rows = pl.ds(pl.multiple_of(g * HGRN_GR, HGRN_GR), HGRN_GR)
        x2 = h_ref[rows, HGRN_W:2 * HGRN_W] * log2e
        hi = h_ref[rows, 2 * HGRN_W:3 * HGRN_W]
        b = log1m_lb - (jnp.maximum(-x2, 0.0) + jnp.log2(1.0 + jnp.exp2(-jnp.abs(x2))))
        log_f = jnp.maximum(log_lb, b) + jnp.log2(1.0 + jnp.exp2(-jnp.abs(log_lb - b)))
        k = 1.0 - jnp.exp2(log_f)
        q = _silu(h_ref[rows, 0:HGRN_W])
        cum = log_f
        sh = 1
        while sh < CHUNK:
            cum = cum + jnp.where(r16 >= sh, pltpu.roll(cum, sh, axis=0), 0.0)
            sh *= 2
        cum = by_chunk(cum)
        last = jnp.broadcast_to(cum[:, CHUNK - 1:CHUNK, :], cum.shape)
        rem = last - cum
        k, q = by_chunk(k), by_chunk(q)
        chunks = pl.ds(pl.multiple_of(g * cpg, cpg), cpg)
        q_s[chunks] = q
        v_s[chunks] = by_chunk(hi)
        c_s[chunks] = cum
        cs_s[chunks] = cum - jnp.log2(k)
        qt_s[chunks] = (q * jnp.exp2(cum)).astype(BF16)
        vb_s[g] = hi.astype(BF16)
        dect_s[g] = jnp.exp2(last).reshape(HGRN_GR, HGRN_W).T
        kt = (k * jnp.exp2(rem)).reshape(HGRN_GR, HGRN_W).T.astype(BF16)
        for pr in range(npair):
            kt_s[pr, g] = kt[pr * HGRN_PW:(pr + 1) * HGRN_PW, :]
        return carry

    lax.fori_loop(0, ngroup, prepare, 0)

    t8 = lax.broadcasted_iota(jnp.int32, (half, HGRN_W), 0)

    def weights(n, carry):
        c0, c1 = c_s[n, 0:half, :], c_s[n, half:, :]
        q0, q1 = q_s[n, 0:half, :], q_s[n, half:, :]
        tiles = []
        for s in range(CHUNK):
            bs = jnp.broadcast_to(cs_s[n, s:s + 1, :], (half, HGRN_W))
            if s < half:
                d0 = jnp.exp2(c0 - bs)
                tiles.append(q0 * (jnp.where(t8 >= s, d0, 0.0) if s else d0))
                tiles.append(q1 * jnp.exp2(c1 - bs))
            else:
                d1 = jnp.exp2(c1 - bs)
                tiles.append(q1 * (jnp.where(t8 >= s - half, d1, 0.0) if s > half else d1))
        w_s[n] = jnp.concatenate(tiles, axis=0).astype(BF16)
        return carry

    lax.fori_loop(0, nchunk, weights, 0)
    sc_s[...] = _dot(w_s[...].reshape(nchunk * HGRN_WR, HGRN_W), _head_ones(BF16)).reshape(nchunk, HGRN_WR, HGRN_W)

    lane_chunk = lax.broadcasted_iota(jnp.int32, (HGRN_PW, HGRN_GR), 1) // CHUNK
    same_head = (lax.broadcasted_iota(jnp.int32, (HGRN_PW, HGRN_PW), 0) // HGRN_HD
                 == lax.broadcasted_iota(jnp.int32, (HGRN_PW, HGRN_PW), 1) // HGRN_HD)

    def group(g, carry):
        state = list(carry)
        pcols = [slice(pr * HGRN_PW, (pr + 1) * HGRN_PW) for pr in range(npair)]
        upd = [[None] * npair for _ in range(cpg)]
        for pr in range(npair):
            kt = kt_s[pr, g]
            lhs = jnp.concatenate([jnp.where(lane_chunk == j, kt, jnp.zeros_like(kt)) for j in range(cpg)], axis=0)
            inc = _dot(lhs, vb_s[g, :, pcols[pr]])
            for j in range(cpg):
                upd[j][pr] = jnp.where(same_head, inc[j * HGRN_PW:(j + 1) * HGRN_PW], 0.0)
        entering = []
        for j in range(cpg):
            entering.append([st.astype(BF16) for st in state])
            for pr in range(npair):
                dec = dect_s[g, pr * HGRN_PW:(pr + 1) * HGRN_PW, j * CHUNK:j * CHUNK + 1]
                state[pr] = state[pr] * jnp.broadcast_to(dec, (HGRN_PW, HGRN_PW)) + upd[j][pr]
        for j in range(cpg):
            n = g * cpg + j
            o0 = jnp.zeros((half, HGRN_W), F32)
            o1 = jnp.zeros((half, HGRN_W), F32)
            for s in range(CHUNK):
                vb = jnp.broadcast_to(v_s[n, s:s + 1, :], (half, HGRN_W))
                if s < half:
                    o0 = o0 + sc_s[n, s * CHUNK:s * CHUNK + half, :] * vb
                    o1 = o1 + sc_s[n, s * CHUNK + half:(s + 1) * CHUNK, :] * vb
                else:
                    o1 = o1 + sc_s[n, half * CHUNK + (s - half) * half:half * CHUNK + (s - half + 1) * half, :] * vb
            oi = jnp.concatenate([_dot(qt_s[n, :, pcols[pr]], entering[j][pr]) for pr in range(npair)], axis=1)
            oo_s[n, 0:half, :] = o0 + oi[0:half]
            oo_s[n, half:, :] = o1 + oi[half:]
        return tuple(state)

    state = lax.fori_loop(0, ngroup, group, tuple(st_ref[pr] for pr in range(npair)))
    for pr in range(npair):
        st_ref[pr] = state[pr]

    def finish(g, carry):
        rows = pl.ds(pl.multiple_of(g * HGRN_GR, HGRN_GR), HGRN_GR)
        o = oo_s[pl.ds(pl.multiple_of(g * cpg, cpg), cpg)].reshape(HGRN_GR, HGRN_W)
        sq = o * o
        sq_hi = sq.astype(BF16)
        ones = _head_ones(BF16)
        ms = (_dot(sq_hi, ones) + _dot((sq - sq_hi.astype(F32)).astype(BF16), ones)) * (1.0 / HGRN_HD)
        hg = h_ref[rows, 3 * HGRN_W:4 * HGRN_W]
        o_ref[rows, :] = (o * lax.rsqrt(ms + EPS) * nw_ref[...] * _silu(hg)).astype(o_ref.dtype)
        return carry

    lax.fori_loop(0, ngroup, finish, 0)


def _hgrn(hproj, lb_logits, nw_tiled, layer, batch, seq, ts):
    t = hproj.shape[0]
    ns = seq // ts
    depth = lb_logits.shape[0]
    nchunk = ts // CHUNK
    npair = HGRN_W // HGRN_PW
    blk = pltpu.VMEM((nchunk, CHUNK, HGRN_W), F32)
    return pl.pallas_call(
        functools.partial(_hgrn_kernel, layer=layer, ts=ts),
        grid=(batch, ns),
        in_specs=[pl.BlockSpec((ts, 4 * HGRN_W), lambda b, i: (b * ns + i, 0)),
                  pl.BlockSpec((depth, HGRN_W), lambda b, i: (0, 0)),
                  pl.BlockSpec((1, HGRN_W), lambda b, i: (0, 0))],
        out_specs=pl.BlockSpec((ts, HGRN_W), lambda b, i: (b * ns + i, 0)),
        out_shape=jax.ShapeDtypeStruct((t, HGRN_W), BF16),
        scratch_shapes=[pltpu.VMEM((npair, HGRN_PW, HGRN_PW), F32),
                        blk, blk, blk, blk, blk,
                        pltpu.VMEM((ts // HGRN_GR, HGRN_W, HGRN_GR), F32),
                        pltpu.VMEM((nchunk, CHUNK, HGRN_W), BF16),
                        pltpu.VMEM((ts // HGRN_GR, HGRN_GR, HGRN_W), BF16),
                        pltpu.VMEM((npair, ts // HGRN_GR, HGRN_PW, HGRN_GR), BF16),
                        pltpu.VMEM((nchunk, HGRN_WR, HGRN_W), BF16), pltpu.VMEM((nchunk, HGRN_WR, HGRN_W), F32)],
        compiler_params=_cparams(("parallel", "arbitrary")),
        name="hgrn2",
    )(hproj, lb_logits, nw_tiled)


def _s5_param_kernel(prow_ref, bt_ref, btile_ref, ctile_ref, ks_ref, gs_ref, hs_ref, d_ref):
    g = pl.program_id(0)
    are, aim = prow_ref[0:1, :], prow_ref[1:2, :]
    dt = jnp.exp(prow_ref[2:3, :])
    lr, li = dt * are, dt * aim
    mag = jnp.exp(lr)
    abr, abi = mag * jnp.cos(li), mag * jnp.sin(li)
    den = are * are + aim * aim
    nr, ni = abr - 1.0, abi
    zr = (nr * are + ni * aim) / den
    zi = (ni * are - nr * aim) / den

    cr, ci = ctile_ref[0], ctile_ref[1]
    wide = S5_L * S5_C

    npow = 2 * S5_L
    nn = lax.broadcasted_iota(jnp.int32, (npow, S5_P), 0).astype(F32)
    mgn = jnp.exp(nn * lr[:, 0:S5_P])
    pw_r, pw_i = mgn * jnp.cos(nn * li[:, 0:S5_P]), mgn * jnp.sin(nn * li[:, 0:S5_P])
    lag = lax.broadcasted_iota(jnp.int32, (npow, wide), 1) // S5_C
    nrow = lax.broadcasted_iota(jnp.int32, (npow, wide), 0)
    tn = (((0,), (0,)), ((), ()))

    def on_lags(a, first):
        sel = (nrow == lag + first).astype(BF16)
        hi = a.astype(BF16)
        lo = (a - hi.astype(F32)).astype(BF16)
        return (lax.dot_general(hi, sel, tn, preferred_element_type=F32)
                + lax.dot_general(lo, sel, tn, preferred_element_type=F32))

    er, ei = on_lags(pw_r, 0), on_lags(pw_i, 0)
    w1 = er * cr - ei * ci
    w2 = -(ei * cr + er * ci)
    zr1, zi1 = zr[:, 0:S5_P], zi[:, 0:S5_P]
    bbr = zr1 * bt_ref[0] - zi1 * bt_ref[1]
    bbi = zr1 * bt_ref[1] + zi1 * bt_ref[0]
    r0 = _dot_split(bbr, w1) + _dot_split(bbi, w2)
    mine = lax.broadcasted_iota(jnp.int32, (S5_C, wide), 1) // S5_C == g
    for j in range(S5_L):
        shift = (g * S5_C + (j + 1) * S5_C) % wide
        ks_ref[j] = jnp.where(mine, pltpu.roll(r0, shift, axis=1), 0.0).astype(ks_ref.dtype)

    e = (S5_L - 1 - lax.broadcasted_iota(jnp.int32, (S5_L, S5_ST), 0)).astype(F32)
    mg = jnp.exp(e * lr)
    rep = (lax.broadcasted_iota(jnp.int32, (wide, S5_L), 0) // S5_C
           == lax.broadcasted_iota(jnp.int32, (wide, S5_L), 1)).astype(BF16)

    def on_rows(a):
        hi = a.astype(BF16)
        return _dot(rep, hi) + _dot(rep, (a - hi.astype(F32)).astype(BF16))

    fr, fi = on_rows(mg * jnp.cos(e * li)), on_rows(mg * jnp.sin(e * li))
    btr = zr * btile_ref[0] - zi * btile_ref[1]
    bti = zr * btile_ref[1] + zi * btile_ref[0]
    re_half = lax.broadcasted_iota(jnp.int32, (wide, S5_ST), 1) < S5_P
    g128 = jnp.where(re_half, fr * btr - fi * bti, fr * bti + fi * btr)
    tile = lax.broadcasted_iota(jnp.int32, (wide, S5_SW), 1) // S5_ST
    gfull = jnp.where(tile == g, jnp.concatenate([g128] * S5_G, axis=1), 0.0)
    gs_ref[...] = gfull.reshape(S5_L, S5_C, S5_SW).astype(gs_ref.dtype)

    pr, pi = on_lags(pw_r, 1), on_lags(pw_i, 1)
    hcat = jnp.concatenate([cr * pr - ci * pi, -(cr * pi + ci * pr)], axis=0)
    mine_h = lax.broadcasted_iota(jnp.int32, (S5_ST, wide), 1) // S5_C == g
    for t in range(S5_L):
        shift = (g * S5_C + wide - t * S5_C) % wide
        hs_ref[t] = jnp.where(mine_h, pltpu.roll(hcat, shift, axis=1), 0.0).astype(hs_ref.dtype)

    mg = jnp.exp(float(S5_L) * lr)
    d_ref[0:1, :] = mg * jnp.cos(float(S5_L) * li)
    d_ref[1:2, :] = mg * jnp.sin(float(S5_L) * li)


def _s5_params(prow, bt, btile, ctile):
    g = prow.shape[0]
    wide = S5_L * S5_C
    m3 = lambda i: (i, 0, 0)
    m4 = lambda i: (i, 0, 0, 0)
    g4 = lambda i: (0, i, 0, 0)
    return pl.pallas_call(
        _s5_param_kernel,
        grid=(g,),
        in_specs=[pl.BlockSpec((None, 3, S5_ST), m3),
                  pl.BlockSpec((None, 2, S5_C, S5_P), m4), pl.BlockSpec((None, 2, wide, S5_ST), m4),
                  pl.BlockSpec((None, 2, S5_P, wide), m4)],
        out_specs=[pl.BlockSpec((S5_L, None, S5_C, wide), g4), pl.BlockSpec((S5_L, None, S5_C, S5_SW), g4),
                   pl.BlockSpec((S5_L, None, S5_ST, wide), g4), pl.BlockSpec((None, 2, S5_ST), m3)],
        out_shape=[jax.ShapeDtypeStruct((S5_L, g, S5_C, wide), BF16), jax.ShapeDtypeStruct((S5_L, g, S5_C, S5_SW), BF16),
                   jax.ShapeDtypeStruct((S5_L, g, S5_ST, wide), BF16), jax.ShapeDtypeStruct((g, 2, S5_ST), F32)],
        compiler_params=_cparams(("parallel",)),
        name="s5_params",
    )(prow, bt, btile, ctile)


def _s5_gather_chunks(u_refs, ucat_s, n):
    for tau in range(S5_L):
        for k, u_ref in enumerate(u_refs):
            lo = tau * S5_W + k * S5_HW
            ucat_s[:, lo:lo + S5_HW] = u_ref[pl.ds(tau, n, stride=S5_L), :].astype(BF16)


def _s5_state_kernel(u0_ref, u1_ref, gs_ref, d_ref, xs_ref, ucat_s, *, nchunk):
    @pl.when(pl.program_id(1) == 0)
    def _():
        _s5_gather_chunks((u0_ref, u1_ref), ucat_s, nchunk)

    width = xs_ref.shape[1]
    x = _dot(ucat_s[...], gs_ref[...])
    sgn = jnp.where(lax.broadcasted_iota(jnp.int32, (1, width), 1) % S5_ST < S5_P, -1.0, 1.0)
    rn = lax.broadcasted_iota(jnp.int32, (nchunk, width), 0)
    pr, pi = d_ref[0:1, :], d_ref[1:2, :]

    def swap(a):
        return jnp.concatenate([pltpu.roll(a[:, k * S5_ST:(k + 1) * S5_ST], S5_P, axis=1)
                                for k in range(width // S5_ST)], axis=1)

    sh = 1
    while sh < nchunk:
        sx = jnp.where(rn >= sh, pltpu.roll(x, sh, axis=0), 0.0)
        x = x + pr * sx + (pi * sgn) * swap(sx)
        pr, pi = pr * pr - pi * pi, 2.0 * pr * pi
        sh *= 2
    xs_ref[...] = jnp.where(rn >= 1, pltpu.roll(x, 1, axis=0), 0.0).astype(xs_ref.dtype)


def _s5_state(sproj, gs, d, batch, seq):
    nchunk = seq // S5_L
    ntile = S5_SW // S5_W
    return pl.pallas_call(
        functools.partial(_s5_state_kernel, nchunk=nchunk),
        grid=(batch, ntile),
        in_specs=[pl.BlockSpec((seq, S5_HW), lambda b, j: (b, 0)), pl.BlockSpec((seq, S5_HW), lambda b, j: (b, 1)),
                  pl.BlockSpec((S5_CW, S5_W), lambda b, j: (0, j)),
                  pl.BlockSpec((2, S5_W), lambda b, j: (0, j))],
        out_specs=pl.BlockSpec((nchunk, S5_W), lambda b, j: (b, j)),
        out_shape=jax.ShapeDtypeStruct((batch * nchunk, S5_SW), BF16),
        scratch_shapes=[pltpu.VMEM((nchunk, S5_CW), BF16)],
        compiler_params=_cparams(("parallel", "arbitrary")),
        name="s5_state",
    )(sproj, sproj, gs, d)


def _s5_out_kernel(u0_ref, u1_ref, g0_ref, g1_ref, xs_ref, ks_ref, hs_ref, dsk_ref, w_ref, b_ref,
                   o0_ref, o1_ref, ucat_s, *, nb):
    step = pl.program_id(1)

    @pl.when(step == 0)
    def _():
        _s5_gather_chunks((u0_ref, u1_ref), ucat_s, nb)

    def response(tt, k):
        return (_dot(ucat_s[:, 0:(tt + 1) * S5_W], ks_ref[(S5_L - 1 - tt) * S5_W:, :])
                + _dot(xs_ref[...], hs_ref[k]))

    def finish(tt, y):
        rows = pl.ds(tt, nb, stride=S5_L)
        cat = lambda a, b: jnp.concatenate([a[rows, :], b[rows, :]], axis=1)
        y = y + dsk_ref[...] * cat(u0_ref, u1_ref)
        y = y * (0.5 * (1.0 + jnp.tanh(math.sqrt(2.0 / math.pi) * (y + 0.044715 * (y * y * y)))))
        z = _dot(y.astype(BF16), w_ref[...]) + b_ref[...]
        out = y * _sigmoid(z) * _silu(cat(g0_ref, g1_ref))
        o0_ref[rows, :] = out[:, 0:S5_HW]
        o1_ref[rows, :] = out[:, S5_HW:]

    for sg in range(S5_L // S5_TG):
        @pl.when(step == sg)
        def _():
            ys = [response(sg * S5_TG + k, k) for k in range(S5_TG)]
            for k in range(S5_TG):
                finish(sg * S5_TG + k, ys[k])


def _s5_out(sproj, xs, ks, hs, d_skip, w_bf, b, rb):
    t = sproj.shape[0]
    nb = rb // S5_L
    cst = lambda r, i: (0, 0)
    return pl.pallas_call(
        functools.partial(_s5_out_kernel, nb=nb),
        grid=(t // rb, S5_L // S5_TG),
        in_specs=[pl.BlockSpec((rb, S5_HW), lambda r, i: (r, 0)), pl.BlockSpec((rb, S5_HW), lambda r, i: (r, 1)),
                  pl.BlockSpec((rb, S5_HW), lambda r, i: (r, 2)), pl.BlockSpec((rb, S5_HW), lambda r, i: (r, 3)),
                  pl.BlockSpec((nb, S5_SW), lambda r, i: (r, 0)),
                  pl.BlockSpec((S5_CW, S5_W), cst),
                  pl.BlockSpec((S5_TG, S5_SW, S5_W), lambda r, i: (i, 0, 0)),
                  pl.BlockSpec((1, S5_W), cst), pl.BlockSpec((S5_W, S5_W), cst), pl.BlockSpec((1, S5_W), cst)],
        out_specs=[pl.BlockSpec((rb, S5_HW), lambda r, i: (r, 0))] * 2,
        out_shape=[jax.ShapeDtypeStruct((t, S5_HW), F32)] * 2,
        scratch_shapes=[pltpu.VMEM((nb, S5_CW), BF16)],
        compiler_params=_cparams(("parallel", "arbitrary")),
        name="s5_out",
    )(sproj, sproj, sproj, sproj, xs, ks, hs, d_skip, w_bf, b)


def _s5(sproj, a_re, a_im, b_re, b_im, c_re, c_im, log_dt, d_skip, glu_w, glu_b, batch, seq, rb):
    ldt = jnp.broadcast_to(log_dt[:, None], a_re.shape)
    twice = lambda a: jnp.concatenate([a, a], axis=-1)
    prow = twice(jnp.stack([a_re, a_im, ldt], axis=1))
    bt = jnp.stack([b_re, b_im], axis=1).transpose(0, 1, 3, 2)
    btile = twice(jnp.tile(bt, (1, 1, S5_L, 1)))
    ctile = jnp.tile(jnp.stack([c_re, c_im], axis=1).transpose(0, 1, 3, 2), (1, 1, 1, S5_L))
    ks, gs, hs, d = _s5_params(prow, bt, btile, ctile)
    ks = ks.reshape(S5_CW, S5_W)
    gs = gs.reshape(S5_CW, S5_SW)
    hs = hs.reshape(S5_L, S5_SW, S5_W)
    d = d.transpose(1, 0, 2).reshape(2, S5_SW)
    xs = _s5_state(sproj, gs, d, batch, seq)
    return _s5_out(sproj, xs, ks, hs, d_skip[None, :], glu_w.astype(BF16), glu_b[None, :], rb)


def _pick(n, pref):
    b = min(n, pref)
    while n % b:
        b //= 2
    return b


def kernel(x, norm_w, w_in, w_out, hgrn_lb_logits, hgrn_norm_w, s5_a_re, s5_a_im, s5_b_re, s5_b_im, s5_c_re, s5_c_im, s5_log_dt, s5_d, s5_glu_w, s5_glu_b, diff_lq1, diff_lk1, diff_lq2, diff_lk2, diff_subln_w, final_norm_w):
    batch, seq, _ = x.shape
    depth = norm_w.shape[0]
    t = batch * seq
    tm = _pick(seq, 512)
    tq = _pick(seq, ATT_TQ)
    ts = _pick(seq, 512)
    rope = _rope_tables(seq)
    h_res = x.astype(F32).reshape(t, D_MODEL)
    for l in range(depth):
        hp, sp, aq, ak, av, ag = _inproj(h_res, norm_w[l][None, :].astype(F32), w_in[l].astype(BF16), rope, seq, tm)
        mix_h = _hgrn(hp, hgrn_lb_logits.astype(F32), jnp.tile(hgrn_norm_w[l].astype(F32), HGRN_W // HGRN_HD)[None, :],
                      l, batch, seq, ts)
        mix_s = _s5(sp, s5_a_re[l], s5_a_im[l], s5_b_re[l], s5_b_im[l], s5_c_re[l], s5_c_im[l], s5_log_dt[l],
                    s5_d[l], s5_glu_w[l], s5_glu_b[l], batch, seq, _pick(seq, 4096))
        lam_init = 0.8 - 0.6 * math.exp(-0.3 * l)
        mix_a = _attention(aq, ak, av, ag, diff_lq1[l][None, :], diff_lk1[l][None, :], diff_lq2[l][None, :],
                           diff_lk2[l][None, :], diff_subln_w[l][None, :], lam_init, batch, seq, tq)
        h_res = _outproj(h_res, mix_h, mix_s[0], mix_s[1], mix_a, w_out[l].astype(BF16), final_norm_w[None, :].astype(F32),
                         l == depth - 1, _pick(seq, 1024))
    return h_res.reshape(batch, seq, D_MODEL).astype(x.dtype)
```

```python
import functools
import math

import jax
import jax.numpy as jnp
from jax import lax
from jax.experimental import pallas as pl
from jax.experimental.pallas import tpu as pltpu

F32 = jnp.float32
BF16 = jnp.bfloat16

D_MODEL = 1024
HGRN_W = 256
HGRN_HD = 64
CHUNK = 16
HGRN_WR = 8 * 16 + 8 * 8
HGRN_PW = 2 * HGRN_HD
HGRN_GR = 128
LB_FLOOR = 1e-30
S5_W = 256
S5_C = 16
S5_G = S5_W // S5_C
S5_P = 64
S5_L = 16
S5_ST = 2 * S5_P
S5_SW = S5_G * S5_ST
S5_CW = S5_L * S5_W
S5_TG = 8
S5_HW = 128
ATT_W = 512
ATT_DH = 64
ATT_DV = 128
ATT_H = ATT_W // ATT_DV
ATT_CT = 256
ATT_VA = ATT_DV + 16
ATT_TQ = 2048
ATT_TK = 512
ROPE_DIM = 16
ROPE_THETA = 500000.0
MASK_VALUE = -1e30
Q_SCALE = ATT_DH ** -0.5 * math.log2(math.e)
EPS = 1e-6
D_IN = 4 * HGRN_W + 2 * S5_W + 4 * ATT_W
COL_S5 = 4 * HGRN_W
COL_AQ = COL_S5 + 2 * S5_W
COL_AK = COL_AQ + ATT_W
COL_AV = COL_AK + ATT_W
COL_AG = COL_AV + ATT_W

VMEM_LIMIT = 56 * 1024 * 1024


def _cparams(sem):
    return pltpu.CompilerParams(dimension_semantics=sem, vmem_limit_bytes=VMEM_LIMIT)


def _sigmoid(x):
    return 1.0 / (1.0 + jnp.exp(-x))


def _softplus_neg_abs(x):
    return jnp.log(1.0 + jnp.exp(-jnp.abs(x)))


def _silu(x):
    return x * _sigmoid(x)


def _dot(a, b):
    return jnp.dot(a, b, preferred_element_type=F32)


def _dot_split(a, b):
    ah = a.astype(BF16)
    al = (a - ah.astype(F32)).astype(BF16)
    bh = b.astype(BF16)
    bl = (b - bh.astype(F32)).astype(BF16)
    return _dot(ah, bh) + _dot(ah, bl) + _dot(al, bh)


def _rope(t, rc, ra, rb):
    return t * rc + pltpu.roll(t, 128 - ROPE_DIM // 2, axis=1) * ra + pltpu.roll(t, ROPE_DIM // 2, axis=1) * rb


def _inproj_kernel(x_ref, nw_ref, w_ref, rope_ref, h_ref, s_ref, uc_ref, q_ref, k_ref, v_ref, g_ref, us_s):
    x = x_ref[...]
    hn = x * lax.rsqrt(jnp.mean(x * x, axis=-1, keepdims=True) + EPS) * nw_ref[...]
    proj = _dot(hn.astype(BF16), w_ref[...])
    h_ref[...] = proj[:, :COL_S5]
    s_ref[...] = proj[:, COL_S5:COL_AQ]
    nchunk = uc_ref.shape[0]
    for k in range(S5_W // S5_HW):
        us_s[k] = proj[:, COL_S5 + k * S5_HW:COL_S5 + (k + 1) * S5_HW]
    for tau in range(S5_L):
        for k in range(S5_W // S5_HW):
            lo = tau * S5_W + k * S5_HW
            uc_ref[:, lo:lo + S5_HW] = us_s[k, pl.ds(tau, nchunk, stride=S5_L), :].astype(BF16)
    rc = rope_ref[:, 0:128]
    ra = rope_ref[:, 128:256]
    rb = rope_ref[:, 256:384]
    for h in range(ATT_H):
        lo = h * ATT_DV
        q = proj[:, COL_AQ + lo:COL_AQ + lo + ATT_DV]
        k = proj[:, COL_AK + lo:COL_AK + lo + ATT_DV]
        q_ref[:, lo:lo + ATT_DV] = (_rope(q, rc, ra, rb) * Q_SCALE).astype(BF16)
        k_ref[:, lo:lo + ATT_DV] = _rope(k, rc, ra, rb).astype(BF16)
    v_ref[...] = proj[:, COL_AV:COL_AG].astype(BF16)
    g_ref[...] = proj[:, COL_AG:]


def _rope_tables(seq):
    pos = jnp.arange(seq, dtype=F32)
    inv_freq = ROPE_THETA ** (-jnp.arange(0, ROPE_DIM, 2, dtype=F32) / ROPE_DIM)
    ang = pos[:, None] * inv_freq[None, :]
    cos, sin = jnp.cos(ang), jnp.sin(ang)
    half = ROPE_DIM // 2
    zeros = jnp.zeros((seq, ATT_DH - ROPE_DIM), F32)
    zh = jnp.zeros((seq, half), F32)
    rc = jnp.concatenate([cos, cos, jnp.ones((seq, ATT_DH - ROPE_DIM), F32)], axis=1)
    ra = jnp.concatenate([-sin, zh, zeros], axis=1)
    rb = jnp.concatenate([zh, sin, zeros], axis=1)
    return jnp.concatenate([jnp.tile(t, (1, 2)) for t in (rc, ra, rb)], axis=1)


def _inproj(x2, norm_w, w_bf, rope, seq, tm):
    t = x2.shape[0]
    nblk = seq // tm
    row = lambda i: (i, 0)
    return pl.pallas_call(
        _inproj_kernel,
        grid=(t // tm,),
        in_specs=[pl.BlockSpec((tm, D_MODEL), row),
                  pl.BlockSpec((1, D_MODEL), lambda i: (0, 0)),
                  pl.BlockSpec((D_MODEL, D_IN), lambda i: (0, 0)),
                  pl.BlockSpec((tm, 384), lambda i: (i % nblk, 0))],
        out_specs=[pl.BlockSpec((tm, COL_S5), row), pl.BlockSpec((tm, 2 * S5_W), row),
                   pl.BlockSpec((tm // S5_L, S5_CW), row),
                   pl.BlockSpec((tm, ATT_W), row), pl.BlockSpec((tm, ATT_W), row),
                   pl.BlockSpec((tm, ATT_W), row), pl.BlockSpec((tm, ATT_W), row)],
        out_shape=[jax.ShapeDtypeStruct((t, COL_S5), F32), jax.ShapeDtypeStruct((t, 2 * S5_W), F32),
                   jax.ShapeDtypeStruct((t // S5_L, S5_CW), BF16),
                   jax.ShapeDtypeStruct((t, ATT_W), BF16), jax.ShapeDtypeStruct((t, ATT_W), BF16),
                   jax.ShapeDtypeStruct((t, ATT_W), BF16), jax.ShapeDtypeStruct((t, ATT_W), F32)],
        scratch_shapes=[pltpu.VMEM((S5_W // S5_HW, tm, S5_HW), F32)],
        compiler_params=_cparams(("parallel",)),
        name="inproj",
    )(x2, norm_w, w_bf, rope)


def _outproj_kernel(res_ref, mh_ref, ms0_ref, ms1_ref, ma_ref, w_ref, fw_ref, o_ref, *, final):
    acc = res_ref[...]
    acc = acc + _dot(mh_ref[...].astype(BF16), w_ref[0:HGRN_W, :])
    acc = acc + _dot(ms0_ref[...].astype(BF16), w_ref[HGRN_W:HGRN_W + S5_HW, :])
    acc = acc + _dot(ms1_ref[...].astype(BF16), w_ref[HGRN_W + S5_HW:HGRN_W + S5_W, :])
    acc = acc + _dot(ma_ref[...].astype(BF16), w_ref[HGRN_W + S5_W:, :])
    if final:
        acc = acc * lax.rsqrt(jnp.mean(acc * acc, axis=-1, keepdims=True) + EPS) * fw_ref[...]
    o_ref[...] = acc


def _outproj(res, mh, ms0, ms1, ma, w_bf, fw, final, tm):
    t = res.shape[0]
    row = lambda i: (i, 0)
    return pl.pallas_call(
        functools.partial(_outproj_kernel, final=final),
        grid=(t // tm,),
        in_specs=[pl.BlockSpec((tm, D_MODEL), row), pl.BlockSpec((tm, HGRN_W), row),
                  pl.BlockSpec((tm, S5_HW), row), pl.BlockSpec((tm, S5_HW), row), pl.BlockSpec((tm, ATT_W), row),
                  pl.BlockSpec((D_MODEL, D_MODEL), lambda i: (0, 0)),
                  pl.BlockSpec((1, D_MODEL), lambda i: (0, 0))],
        out_specs=pl.BlockSpec((tm, D_MODEL), row),
        out_shape=jax.ShapeDtypeStruct((t, D_MODEL), F32),
        compiler_params=_cparams(("parallel",)),
        name="outproj",
    )(res, mh, ms0, ms1, ma, w_bf, fw)


def _attn_kernel(q_ref, k_ref, v_ref, g_ref, lq1_ref, lk1_ref, lq2_ref, lk2_ref, sw_ref, o_ref,
                 vt_s, qq_s, acc_s, s0_s, s1_s, *, lam_init, tq, tk):
    i = pl.program_id(2)
    nkb = vt_s.shape[0]
    ntile = 2 * tq // ATT_CT
    per_map = tq // ATT_CT
    nt = (((1,), (1,)), ((), ()))

    @pl.when(i == 0)
    def _():
        for j in range(nkb):
            vt_s[j, 0:ATT_DV, :] = v_ref[j * tk:(j + 1) * tk, :].T
            vt_s[j, ATT_DV:, :] = jnp.ones((ATT_VA - ATT_DV, tk), BF16)

    q = q_ref[...]
    lane = lax.broadcasted_iota(jnp.int32, (tq, ATT_DV), 1)
    qq_s[0:tq, :] = jnp.where(lane < ATT_DH, q, jnp.zeros_like(q))
    qq_s[tq:2 * tq, :] = jnp.where(lane >= ATT_DH, q, jnp.zeros_like(q))
    acc_s[...] = jnp.zeros(acc_s.shape, F32)

    def tile_mode(c, koff):
        qs = (c % per_map) * ATT_CT
        if koff is None or qs >= koff + tk - 1:
            return "full"
        return "skip" if qs + ATT_CT - 1 < koff else "causal"

    def stage(kq, sq_s, koff_q, ssm_s, ksm, koff_sm, m_run, cmax):
        new_m, new_cmax = list(m_run), [None] * ntile
        if kq is not None:
            kb = k_ref[pl.ds(kq * tk if isinstance(kq, int) else pl.multiple_of(kq * tk, tk), tk), :]
        if ksm is not None:
            vtb = vt_s[ksm]
        for c in range(ntile):
            cols = slice(c * ATT_CT, (c + 1) * ATT_CT)
            if kq is not None and tile_mode(c, koff_q) != "skip":
                s = lax.dot_general(kb, qq_s[cols, :], nt, preferred_element_type=F32)
                if tile_mode(c, koff_q) == "causal":
                    kpos = lax.broadcasted_iota(jnp.int32, (tk, ATT_CT), 0) + koff_q
                    qpos = lax.broadcasted_iota(jnp.int32, (tk, ATT_CT), 1) + (c % per_map) * ATT_CT
                    s = jnp.where(kpos <= qpos, s, MASK_VALUE)
                sq_s[:, cols] = s
                new_cmax[c] = jnp.max(s, axis=0, keepdims=True)
            if ksm is not None and tile_mode(c, koff_sm) != "skip":
                m_new = jnp.maximum(m_run[c], cmax[c])
                alpha = jnp.exp2(m_run[c] - m_new)
                p = jnp.exp2(ssm_s[:, cols] - m_new)
                acc_s[:, cols] = alpha * acc_s[:, cols] + _dot(vtb, p.astype(BF16))
                new_m[c] = m_new
        return tuple(new_m), tuple(new_cmax)

    neg = tuple(jnp.full((1, ATT_CT), MASK_VALUE, F32) for _ in range(ntile))
    ndiag = tq // tk
    nfull = ndiag * i
    bufs = (s0_s, s1_s)
    fill = lambda cmx: tuple(neg[c] if x is None else x for c, x in enumerate(cmx))
    _, cm = stage(nfull, bufs[0], 0, None, None, None, neg, None)
    m_run = neg
    for m in range(1, ndiag):
        m_run, cm = stage(nfull + m, bufs[m % 2], m * tk, bufs[(m - 1) % 2], nfull + m - 1, (m - 1) * tk,
                          m_run, fill(cm))
    cm = fill(cm)
    last = ndiag - 1

    def pair(t, carry):
        m_run, cm0 = carry
        m_run, cm1 = stage(2 * t + 1, s1_s, None, s0_s, 2 * t, None, m_run, cm0)
        return stage(2 * t + 2, s0_s, None, s1_s, 2 * t + 1, None, m_run, cm1)

    @pl.when(i == 0)
    def _():
        stage(None, None, None, s1_s, nfull + last, last * tk, m_run, cm)

    @pl.when(i > 0)
    def _():
        m2, cm0 = stage(0, s0_s, None, s1_s, nfull + last, last * tk, m_run, cm)
        m2, cm0 = lax.fori_loop(0, nfull // 2 - 1, pair, (m2, cm0))
        m2, cml = stage(nfull - 1, s1_s, None, s0_s, nfull - 2, None, m2, cm0)
        stage(None, None, None, s1_s, nfull - 1, None, m2, cml)

    lam = (jnp.exp(jnp.sum(lq1_ref[...] * lk1_ref[...], axis=-1, keepdims=True))
           - jnp.exp(jnp.sum(lq2_ref[...] * lk2_ref[...], axis=-1, keepdims=True)) + lam_init)
    on = acc_s[0:ATT_DV, :] / acc_s[ATT_DV:ATT_DV + 1, :]
    ot = on[:, 0:tq] - lam * on[:, tq:2 * tq]
    o = ot.T
    o = o * lax.rsqrt(jnp.mean(o * o, axis=-1, keepdims=True) + EPS) * sw_ref[...] * (1.0 - lam_init)
    o_ref[...] = (o * _silu(g_ref[...])).astype(o_ref.dtype)


def _attention(q, k, v, g, lq1, lk1, lq2, lk2, sw, lam_init, batch, seq, tq):
    t = q.shape[0]
    tk = min(ATT_TK, tq // 2)
    nq = seq // tq
    qmap = lambda b, h, i: (b * nq + i, h)
    kvmap = lambda b, h, i: (b, h)
    cmap = lambda b, h, i: (0, 0)
    return pl.pallas_call(
        functools.partial(_attn_kernel, lam_init=lam_init, tq=tq, tk=tk),
        grid=(batch, ATT_H, nq),
        in_specs=[pl.BlockSpec((tq, ATT_DV), qmap), pl.BlockSpec((seq, ATT_DV), kvmap),
                  pl.BlockSpec((seq, ATT_DV), kvmap), pl.BlockSpec((tq, ATT_DV), qmap),
                  pl.BlockSpec((1, ATT_DH), cmap), pl.BlockSpec((1, ATT_DH), cmap),
                  pl.BlockSpec((1, ATT_DH), cmap), pl.BlockSpec((1, ATT_DH), cmap),
                  pl.BlockSpec((1, ATT_DV), cmap)],
        out_specs=pl.BlockSpec((tq, ATT_DV), qmap),
        out_shape=jax.ShapeDtypeStruct((t, ATT_W), BF16),
        scratch_shapes=[pltpu.VMEM((seq // tk, ATT_VA, tk), BF16), pltpu.VMEM((2 * tq, ATT_DV), BF16),
                        pltpu.VMEM((ATT_VA, 2 * tq), F32),
                        pltpu.VMEM((tk, 2 * tq), F32), pltpu.VMEM((tk, 2 * tq), F32)],
        compiler_params=_cparams(("parallel", "parallel", "arbitrary")),
        name="diffattn",
    )(q, k, v, g, lq1, lk1, lq2, lk2, sw)


def _head_ones(dtype):
    r = lax.broadcasted_iota(jnp.int32, (HGRN_W, HGRN_W), 0) // HGRN_HD
    c = lax.broadcasted_iota(jnp.int32, (HGRN_W, HGRN_W), 1) // HGRN_HD
    return (r == c).astype(dtype)


def _hgrn_kernel(h_ref, lbl_ref, nw_ref, o_ref, st_ref, q_s, v_s, c_s, cs_s, oo_s, dect_s, qt_s, vb_s, kt_s,
                 w_s, sc_s, *, layer, ts):
    nchunk = ts // CHUNK
    ngroup = ts // HGRN_GR
    npair = HGRN_W // HGRN_PW
    half = CHUNK // 2

    @pl.when(pl.program_id(1) == 0)
    def _():
        st_ref[...] = jnp.zeros(st_ref.shape, F32)

    lg = lbl_ref[...]
    e = jnp.exp(lg - jnp.max(lg, axis=0, keepdims=True))
    p = e / jnp.sum(e, axis=0, keepdims=True)
    lb = jnp.sum(p[0:layer + 1, :], axis=0, keepdims=True) - p[0:1, :]
    log2e = math.log2(math.e)
    log_lb = jnp.log(jnp.maximum(lb, LB_FLOOR)) * log2e
    log1m_lb = jnp.log1p(-lb) * log2e
    r16 = lax.broadcasted_iota(jnp.int32, (HGRN_GR, HGRN_W), 0) % CHUNK
    cpg = HGRN_GR // CHUNK
    by_chunk = lambda a: a.reshape(cpg, CHUNK, HGRN_W)

    def prepare(g, carry):
        rows = pl.ds(pl.multiple_of(g * HGRN_GR, HGRN_GR), HGRN_GR)
        x2 = h_ref[rows, HGRN_W:2 * HGRN_W] * log2e
        hi = h_ref[rows, 2 * HGRN_W:3 * HGRN_W]
        b = log1m_lb - (jnp.maximum(-x2, 0.0) + jnp.log2(1.0 + jnp.exp2(-jnp.abs(x2))))
        log_f = jnp.maximum(log_lb, b) + jnp.log2(1.0 + jnp.exp2(-jnp.abs(log_lb - b)))
        k = 1.0 - jnp.exp2(log_f)
        q = _silu(h_ref[rows, 0:HGRN_W])
        cum = log_f
        sh = 1
        while sh < CHUNK:
            cum = cum + jnp.where(r16 >= sh, pltpu.roll(cum, sh, axis=0), 0.0)
            sh *= 2
        cum = by_chunk(cum)
        last = jnp.broadcast_to(cum[:, CHUNK - 1:CHUNK, :], cum.shape)
        rem = last - cum
        k, q = by_chunk(k), by_chunk(q)
        chunks = pl.ds(pl.multiple_of(g * cpg, cpg), cpg)
        q_s[chunks] = q
        v_s[chunks] = by_chunk(hi)
        c_s[chunks] = cum
        cs_s[chunks] = cum - jnp.log2(k)
        qt_s[chunks] = (q * jnp.exp2(cum)).astype(BF16)
        vb_s[g] = hi.astype(BF16)
        dect_s[g] = jnp.exp2(last).reshape(HGRN_GR, HGRN_W).T
        kt = (k * jnp.exp2(rem)).reshape(HGRN_GR, HGRN_W).T.astype(BF16)
        for pr in range(npair):
            kt_s[pr, g] = kt[pr * HGRN_PW:(pr + 1) * HGRN_PW, :]
        return carry

    lax.fori_loop(0, ngroup, prepare, 0)

    t8 = lax.broadcasted_iota(jnp.int32, (half, HGRN_W), 0)

    def weights(n, carry):
        c0, c1 = c_s[n, 0:half, :], c_s[n, half:, :]
        q0, q1 = q_s[n, 0:half, :], q_s[n, half:, :]
        tiles = []
        for s in range(CHUNK):
            bs = jnp.broadcast_to(cs_s[n, s:s + 1, :], (half, HGRN_W))
            if s < half:
                d0 = jnp.exp2(c0 - bs)
                tiles.append(q0 * (jnp.where(t8 >= s, d0, 0.0) if s else d0))
                tiles.append(q1 * jnp.exp2(c1 - bs))
            else:
                d1 = jnp.exp2(c1 - bs)
                tiles.append(q1 * (jnp.where(t8 >= s - half, d1, 0.0) if s > half else d1))
        w_s[n] = jnp.concatenate(tiles, axis=0).astype(BF16)
        return carry

    lax.fori_loop(0, nchunk, weights, 0)
    sc_s[...] = _dot(w_s[...].reshape(nchunk * HGRN_WR, HGRN_W), _head_ones(BF16)).reshape(nchunk, HGRN_WR, HGRN_W)

    lane_chunk = lax.broadcasted_iota(jnp.int32, (HGRN_PW, HGRN_GR), 1) // CHUNK
    same_head = (lax.broadcasted_iota(jnp.int32, (HGRN_PW, HGRN_PW), 0) // HGRN_HD
                 == lax.broadcasted_iota(jnp.int32, (HGRN_PW, HGRN_PW), 1) // HGRN_HD)

    def group(g, carry):
        state = list(carry)
        pcols = [slice(pr * HGRN_PW, (pr + 1) * HGRN_PW) for pr in range(npair)]
        upd = [[None] * npair for _ in range(cpg)]
        for pr in range(npair):
            kt = kt_s[pr, g]
            lhs = jnp.concatenate([jnp.where(lane_chunk == j, kt, jnp.zeros_like(kt)) for j in range(cpg)], axis=0)
            inc = _dot(lhs, vb_s[g, :, pcols[pr]])
            for j in range(cpg):
                upd[j][pr] = jnp.where(same_head, inc[j * HGRN_PW:(j + 1) * HGRN_PW], 0.0)
        entering = []
        for j in range(cpg):
            entering.append([st.astype(BF16) for st in state])
            for pr in range(npair):
                dec = dect_s[g, pr * HGRN_PW:(pr + 1) * HGRN_PW, j * CHUNK:j * CHUNK + 1]
                state[pr] = state[pr] * jnp.broadcast_to(dec, (HGRN_PW, HGRN_PW)) + upd[j][pr]
        for j in range(cpg):
            n = g * cpg + j
            o0 = jnp.zeros((half, HGRN_W), F32)
            o1 = jnp.zeros((half, HGRN_W), F32)
            for s in range(CHUNK):
                vb = jnp.broadcast_to(v_s[n, s:s + 1, :], (half, HGRN_W))
                if s < half:
                    o0 = o0 + sc_s[n, s * CHUNK:s * CHUNK + half, :] * vb
                    o1 = o1 + sc_s[n, s * CHUNK + half:(s + 1) * CHUNK, :] * vb
                else:
                    o1 = o1 + sc_s[n, half * CHUNK + (s - half) * half:half * CHUNK + (s - half + 1) * half, :] * vb
            oi = jnp.concatenate([_dot(qt_s[n, :, pcols[pr]], entering[j][pr]) for pr in range(npair)], axis=1)
            oo_s[n, 0:half, :] = o0 + oi[0:half]
            oo_s[n, half:, :] = o1 + oi[half:]
        return tuple(state)

    state = lax.fori_loop(0, ngroup, group, tuple(st_ref[pr] for pr in range(npair)))
    for pr in range(npair):
        st_ref[pr] = state[pr]

    def finish(g, carry):
        rows = pl.ds(pl.multiple_of(g * HGRN_GR, HGRN_GR), HGRN_GR)
        o = oo_s[pl.ds(pl.multiple_of(g * cpg, cpg), cpg)].reshape(HGRN_GR, HGRN_W)
        sq = o * o
        sq_hi = sq.astype(BF16)
        ones = _head_ones(BF16)
        ms = (_dot(sq_hi, ones) + _dot((sq - sq_hi.astype(F32)).astype(BF16), ones)) * (1.0 / HGRN_HD)
        hg = h_ref[rows, 3 * HGRN_W:4 * HGRN_W]
        o_ref[rows, :] = (o * lax.rsqrt(ms + EPS) * nw_ref[...] * _silu(hg)).astype(o_ref.dtype)
        return carry

    lax.fori_loop(0, ngroup, finish, 0)


def _hgrn(hproj, lb_logits, nw_tiled, layer, batch, seq, ts):
    t = hproj.shape[0]
    ns = seq // ts
    depth = lb_logits.shape[0]
    nchunk = ts // CHUNK
    npair = HGRN_W // HGRN_PW
    blk = pltpu.VMEM((nchunk, CHUNK, HGRN_W), F32)
    return pl.pallas_call(
        functools.partial(_hgrn_kernel, layer=layer, ts=ts),
        grid=(batch, ns),
        in_specs=[pl.BlockSpec((ts, 4 * HGRN_W), lambda b, i: (b * ns + i, 0)),
                  pl.BlockSpec((depth, HGRN_W), lambda b, i: (0, 0)),
                  pl.BlockSpec((1, HGRN_W), lambda b, i: (0, 0))],
        out_specs=pl.BlockSpec((ts, HGRN_W), lambda b, i: (b * ns + i, 0)),
        out_shape=jax.ShapeDtypeStruct((t, HGRN_W), BF16),
        scratch_shapes=[pltpu.VMEM((npair, HGRN_PW, HGRN_PW), F32),
                        blk, blk, blk, blk, blk,
                        pltpu.VMEM((ts // HGRN_GR, HGRN_W, HGRN_GR), F32),
                        pltpu.VMEM((nchunk, CHUNK, HGRN_W), BF16),
                        pltpu.VMEM((ts // HGRN_GR, HGRN_GR, HGRN_W), BF16),
                        pltpu.VMEM((npair, ts // HGRN_GR, HGRN_PW, HGRN_GR), BF16),
                        pltpu.VMEM((nchunk, HGRN_WR, HGRN_W), BF16), pltpu.VMEM((nchunk, HGRN_WR, HGRN_W), F32)],
        compiler_params=_cparams(("parallel", "arbitrary")),
        name="hgrn2",
    )(hproj, lb_logits, nw_tiled)


def _s5_param_kernel(prow_ref, bt_ref, btile_ref, ctile_ref, ks_ref, gs_ref, hs_ref, d_ref):
    g = pl.program_id(0)
    are, aim = prow_ref[0:1, :], prow_ref[1:2, :]
    dt = jnp.exp(prow_ref[2:3, :])
    lr, li = dt * are, dt * aim
    mag = jnp.exp(lr)
    abr, abi = mag * jnp.cos(li), mag * jnp.sin(li)
    den = are * are + aim * aim
    nr, ni = abr - 1.0, abi
    zr = (nr * are + ni * aim) / den
    zi = (ni * are - nr * aim) / den

    cr, ci = ctile_ref[0], ctile_ref[1]
    wide = S5_L * S5_C

    npow = 2 * S5_L
    nn = lax.broadcasted_iota(jnp.int32, (npow, S5_P), 0).astype(F32)
    mgn = jnp.exp(nn * lr[:, 0:S5_P])
    pw_r, pw_i = mgn * jnp.cos(nn * li[:, 0:S5_P]), mgn * jnp.sin(nn * li[:, 0:S5_P])
    lag = lax.broadcasted_iota(jnp.int32, (npow, wide), 1) // S5_C
    nrow = lax.broadcasted_iota(jnp.int32, (npow, wide), 0)
    tn = (((0,), (0,)), ((), ()))

    def on_lags(a, first):
        sel = (nrow == lag + first).astype(BF16)
        hi = a.astype(BF16)
        lo = (a - hi.astype(F32)).astype(BF16)
        return (lax.dot_general(hi, sel, tn, preferred_element_type=F32)
                + lax.dot_general(lo, sel, tn, preferred_element_type=F32))

    er, ei = on_lags(pw_r, 0), on_lags(pw_i, 0)
    w1 = er * cr - ei * ci
    w2 = -(ei * cr + er * ci)
    zr1, zi1 = zr[:, 0:S5_P], zi[:, 0:S5_P]
    bbr = zr1 * bt_ref[0] - zi1 * bt_ref[1]
    bbi = zr1 * bt_ref[1] + zi1 * bt_ref[0]
    r0 = _dot_split(bbr, w1) + _dot_split(bbi, w2)
    mine = lax.broadcasted_iota(jnp.int32, (S5_C, wide), 1) // S5_C == g
    for j in range(S5_L):
        shift = (g * S5_C + (j + 1) * S5_C) % wide
        ks_ref[j] = jnp.where(mine, pltpu.roll(r0, shift, axis=1), 0.0).astype(ks_ref.dtype)

    e = (S5_L - 1 - lax.broadcasted_iota(jnp.int32, (S5_L, S5_ST), 0)).astype(F32)
    mg = jnp.exp(e * lr)
    rep = (lax.broadcasted_iota(jnp.int32, (wide, S5_L), 0) // S5_C
           == lax.broadcasted_iota(jnp.int32, (wide, S5_L), 1)).astype(BF16)

    def on_rows(a):
        hi = a.astype(BF16)
        return _dot(rep, hi) + _dot(rep, (a - hi.astype(F32)).astype(BF16))

    fr, fi = on_rows(mg * jnp.cos(e * li)), on_rows(mg * jnp.sin(e * li))
    btr = zr * btile_ref[0] - zi * btile_ref[1]
    bti = zr * btile_ref[1] + zi * btile_ref[0]
    re_half = lax.broadcasted_iota(jnp.int32, (wide, S5_ST), 1) < S5_P
    g128 = jnp.where(re_half, fr * btr - fi * bti, fr * bti + fi * btr)
    tile = lax.broadcasted_iota(jnp.int32, (wide, S5_SW), 1) // S5_ST
    gfull = jnp.where(tile == g, jnp.concatenate([g128] * S5_G, axis=1), 0.0)
    gs_ref[...] = gfull.reshape(S5_L, S5_C, S5_SW).astype(gs_ref.dtype)

    pr, pi = on_lags(pw_r, 1), on_lags(pw_i, 1)
    hcat = jnp.concatenate([cr * pr - ci * pi, -(cr * pi + ci * pr)], axis=0)
    mine_h = lax.broadcasted_iota(jnp.int32, (S5_ST, wide), 1) // S5_C == g
    for t in range(S5_L):
        shift = (g * S5_C + wide - t * S5_C) % wide
        hs_ref[t] = jnp.where(mine_h, pltpu.roll(hcat, shift, axis=1), 0.0).astype(hs_ref.dtype)

    mg = jnp.exp(float(S5_L) * lr)
    d_ref[0:1, :] = mg * jnp.cos(float(S5_L) * li)
    d_ref[1:2, :] = mg * jnp.sin(float(S5_L) * li)


def _s5_params(prow, bt, btile, ctile):
    g = prow.shape[0]
    wide = S5_L * S5_C
    m3 = lambda i: (i, 0, 0)
    m4 = lambda i: (i, 0, 0, 0)
    g4 = lambda i: (0, i, 0, 0)
    return pl.pallas_call(
        _s5_param_kernel,
        grid=(g,),
        in_specs=[pl.BlockSpec((None, 3, S5_ST), m3),
                  pl.BlockSpec((None, 2, S5_C, S5_P), m4), pl.BlockSpec((None, 2, wide, S5_ST), m4),
                  pl.BlockSpec((None, 2, S5_P, wide), m4)],
        out_specs=[pl.BlockSpec((S5_L, None, S5_C, wide), g4), pl.BlockSpec((S5_L, None, S5_C, S5_SW), g4),
                   pl.BlockSpec((S5_L, None, S5_ST, wide), g4), pl.BlockSpec((None, 2, S5_ST), m3)],
        out_shape=[jax.ShapeDtypeStruct((S5_L, g, S5_C, wide), BF16), jax.ShapeDtypeStruct((S5_L, g, S5_C, S5_SW), BF16),
                   jax.ShapeDtypeStruct((S5_L, g, S5_ST, wide), BF16), jax.ShapeDtypeStruct((g, 2, S5_ST), F32)],
        compiler_params=_cparams(("parallel",)),
        name="s5_params",
    )(prow, bt, btile, ctile)


def _s5_state_kernel(ucat_ref, gs_ref, d_ref, xs_ref, *, nchunk):
    width = xs_ref.shape[1]
    x = _dot(ucat_ref[...], gs_ref[...])
    sgn = jnp.where(lax.broadcasted_iota(jnp.int32, (1, width), 1) % S5_ST < S5_P, -1.0, 1.0)
    rn = lax.broadcasted_iota(jnp.int32, (nchunk, width), 0)
    pr, pi = d_ref[0:1, :], d_ref[1:2, :]

    def swap(a):
        return jnp.concatenate([pltpu.roll(a[:, k * S5_ST:(k + 1) * S5_ST], S5_P, axis=1)
                                for k in range(width // S5_ST)], axis=1)

    sh = 1
    while sh < nchunk:
        sx = jnp.where(rn >= sh, pltpu.roll(x, sh, axis=0), 0.0)
        x = x + pr * sx + (pi * sgn) * swap(sx)
        pr, pi = pr * pr - pi * pi, 2.0 * pr * pi
        sh *= 2
    xs_ref[...] = jnp.where(rn >= 1, pltpu.roll(x, 1, axis=0), 0.0).astype(xs_ref.dtype)


def _s5_state(ucat, gs, d, batch, seq):
    nchunk = seq // S5_L
    ntile = S5_SW // S5_W
    return pl.pallas_call(
        functools.partial(_s5_state_kernel, nchunk=nchunk),
        grid=(batch, ntile),
        in_specs=[pl.BlockSpec((nchunk, S5_CW), lambda b, j: (b, 0)),
                  pl.BlockSpec((S5_CW, S5_W), lambda b, j: (0, j)),
                  pl.BlockSpec((2, S5_W), lambda b, j: (0, j))],
        out_specs=pl.BlockSpec((nchunk, S5_W), lambda b, j: (b, j)),
        out_shape=jax.ShapeDtypeStruct((batch * nchunk, S5_SW), BF16),
        compiler_params=_cparams(("parallel", "arbitrary")),
        name="s5_state",
    )(ucat, gs, d)


def _s5_out_kernel(ucat_s, u0_ref, u1_ref, g0_ref, g1_ref, xs_ref, ks_ref, hs_ref, dsk_ref, w_ref, b_ref,
                   o0_ref, o1_ref, *, nb):
    step = pl.program_id(1)

    def response(tt, k):
        return (_dot(ucat_s[:, 0:(tt + 1) * S5_W], ks_ref[(S5_L - 1 - tt) * S5_W:, :])
                + _dot(xs_ref[...], hs_ref[k]))

    def finish(tt, y):
        rows = pl.ds(tt, nb, stride=S5_L)
        cat = lambda a, b: jnp.concatenate([a[rows, :], b[rows, :]], axis=1)
        y = y + dsk_ref[...] * cat(u0_ref, u1_ref)
        y = y * (0.5 * (1.0 + jnp.tanh(math.sqrt(2.0 / math.pi) * (y + 0.044715 * (y * y * y)))))
        z = _dot(y.astype(BF16), w_ref[...]) + b_ref[...]
        out = y * _sigmoid(z) * _silu(cat(g0_ref, g1_ref))
        o0_ref[rows, :] = out[:, 0:S5_HW]
        o1_ref[rows, :] = out[:, S5_HW:]

    for sg in range(S5_L // S5_TG):
        @pl.when(step == sg)
        def _():
            ys = [response(sg * S5_TG + k, k) for k in range(S5_TG)]
            for k in range(S5_TG):
                finish(sg * S5_TG + k, ys[k])


def _s5_out(ucat, sproj, xs, ks, hs, d_skip, w_bf, b, rb):
    t = sproj.shape[0]
    nb = rb // S5_L
    cst = lambda r, i: (0, 0)
    return pl.pallas_call(
        functools.partial(_s5_out_kernel, nb=nb),
        grid=(t // rb, S5_L // S5_TG),
        in_specs=[pl.BlockSpec((nb, S5_CW), lambda r, i: (r, 0)),
                  pl.BlockSpec((rb, S5_HW), lambda r, i: (r, 0)), pl.BlockSpec((rb, S5_HW), lambda r, i: (r, 1)),
                  pl.BlockSpec((rb, S5_HW), lambda r, i: (r, 2)), pl.BlockSpec((rb, S5_HW), lambda r, i: (r, 3)),
                  pl.BlockSpec((nb, S5_SW), lambda r, i: (r, 0)),
                  pl.BlockSpec((S5_CW, S5_W), cst),
                  pl.BlockSpec((S5_TG, S5_SW, S5_W), lambda r, i: (i, 0, 0)),
                  pl.BlockSpec((1, S5_W), cst), pl.BlockSpec((S5_W, S5_W), cst), pl.BlockSpec((1, S5_W), cst)],
        out_specs=[pl.BlockSpec((rb, S5_HW), lambda r, i: (r, 0))] * 2,
        out_shape=[jax.ShapeDtypeStruct((t, S5_HW), F32)] * 2,
        compiler_params=_cparams(("parallel", "arbitrary")),
        name="s5_out",
    )(ucat, sproj, sproj, sproj, sproj, xs, ks, hs, d_skip, w_bf, b)


def _s5(ucat, sproj, a_re, a_im, b_re, b_im, c_re, c_im, log_dt, d_skip, glu_w, glu_b, batch, seq, rb):
    ldt = jnp.broadcast_to(log_dt[:, None], a_re.shape)
    twice = lambda a: jnp.concatenate([a, a], axis=-1)
    prow = twice(jnp.stack([a_re, a_im, ldt], axis=1))
    bt = jnp.stack([b_re, b_im], axis=1).transpose(0, 1, 3, 2)
    btile = twice(jnp.tile(bt, (1, 1, S5_L, 1)))
    ctile = jnp.tile(jnp.stack([c_re, c_im], axis=1).transpose(0, 1, 3, 2), (1, 1, 1, S5_L))
    ks, gs, hs, d = _s5_params(prow, bt, btile, ctile)
    ks = ks.reshape(S5_CW, S5_W)
    gs = gs.reshape(S5_CW, S5_SW)
    hs = hs.reshape(S5_L, S5_SW, S5_W)
    d = d.transpose(1, 0, 2).reshape(2, S5_SW)
    xs = _s5_state(ucat, gs, d, batch, seq)
    return _s5_out(ucat, sproj, xs, ks, hs, d_skip[None, :], glu_w.astype(BF16), glu_b[None, :], rb)


def _pick(n, pref):
    b = min(n, pref)
    while n % b:
        b //= 2
    return b


def kernel(x, norm_w, w_in, w_out, hgrn_lb_logits, hgrn_norm_w, s5_a_re, s5_a_im, s5_b_re, s5_b_im, s5_c_re, s5_c_im, s5_log_dt, s5_d, s5_glu_w, s5_glu_b, diff_lq1, diff_lk1, diff_lq2, diff_lk2, diff_subln_w, final_norm_w):
    batch, seq, _ = x.shape
    depth = norm_w.shape[0]
    t = batch * seq
    tm = _pick(seq, 512)
    tq = _pick(seq, ATT_TQ)
    ts = _pick(seq, 512)
    rope = _rope_tables(seq)
    h_res = x.astype(F32).reshape(t, D_MODEL)
    for l in range(depth):
        hp, sp, uc, aq, ak, av, ag = _inproj(h_res, norm_w[l][None, :].astype(F32), w_in[l].astype(BF16), rope, seq, tm)
        mix_h = _hgrn(hp, hgrn_lb_logits.astype(F32), jnp.tile(hgrn_norm_w[l].astype(F32), HGRN_W // HGRN_HD)[None, :],
                      l, batch, seq, ts)
        mix_s = _s5(uc, sp, s5_a_re[l], s5_a_im[l], s5_b_re[l], s5_b_im[l], s5_c_re[l], s5_c_im[l], s5_log_dt[l],
                    s5_d[l], s5_glu_w[l], s5_glu_b[l], batch, seq, _pick(seq, 4096))
        lam_init = 0.8 - 0.6 * math.exp(-0.3 * l)
        mix_a = _attention(aq, ak, av, ag, diff_lq1[l][None, :], diff_lk1[l][None, :], diff_lq2[l][None, :],
                           diff_lk2[l][None, :], diff_subln_w[l][None, :], lam_init, batch, seq, tq)
        h_res = _outproj(h_res, mix_h, mix_s[0], mix_s[1], mix_a, w_out[l].astype(BF16), final_norm_w[None, :].astype(F32),
                         l == depth - 1, _pick(seq, 1024))
    return h_res.reshape(batch, seq, D_MODEL).astype(x.dtype)
```

```python
import functools
import math

import jax
import jax.numpy as jnp
from jax import lax
from jax.experimental import pallas as pl
from jax.experimental.pallas import tpu as pltpu

F32 = jnp.float32
BF16 = jnp.bfloat16

D_MODEL = 1024
HGRN_W = 256
HGRN_HD = 64
CHUNK = 16
HGRN_WR = 8 * 16 + 8 * 8
HGRN_PW = 2 * HGRN_HD
HGRN_GR = 128
LB_FLOOR = 1e-30
S5_W = 256
S5_C = 16
S5_G = S5_W // S5_C
S5_P = 64
S5_L = 16
S5_ST = 2 * S5_P
S5_SW = S5_G * S5_ST
S5_CW = S5_L * S5_W
S5_TG = 8
S5_HW = 128
ATT_W = 512
ATT_DH = 64
ATT_DV = 128
ATT_H = ATT_W // ATT_DV
ATT_CT = 256
ATT_VA = ATT_DV + 16
ATT_TQ = 2048
ATT_TK = 512
ROPE_DIM = 16
ROPE_THETA = 500000.0
MASK_VALUE = -1e30
Q_SCALE = ATT_DH ** -0.5 * math.log2(math.e)
EPS = 1e-6
D_IN = 4 * HGRN_W + 2 * S5_W + 4 * ATT_W
COL_S5 = 4 * HGRN_W
COL_AQ = COL_S5 + 2 * S5_W
COL_AK = COL_AQ + ATT_W
COL_AV = COL_AK + ATT_W
COL_AG = COL_AV + ATT_W

VMEM_LIMIT = 56 * 1024 * 1024


def _cparams(sem):
    return pltpu.CompilerParams(dimension_semantics=sem, vmem_limit_bytes=VMEM_LIMIT)


def _sigmoid(x):
    return 1.0 / (1.0 + jnp.exp(-x))


def _softplus_neg_abs(x):
    return jnp.log(1.0 + jnp.exp(-jnp.abs(x)))


def _silu(x):
    return x * _sigmoid(x)


def _dot(a, b):
    return jnp.dot(a, b, preferred_element_type=F32)


def _dot_split(a, b):
    ah = a.astype(BF16)
    al = (a - ah.astype(F32)).astype(BF16)
    bh = b.astype(BF16)
    bl = (b - bh.astype(F32)).astype(BF16)
    return _dot(ah, bh) + _dot(ah, bl) + _dot(al, bh)


def _rope(t, rc, ra, rb):
    return t * rc + pltpu.roll(t, 128 - ROPE_DIM // 2, axis=1) * ra + pltpu.roll(t, ROPE_DIM // 2, axis=1) * rb


def _project(x, nw_ref, w_ref, rope_ref, h_ref, s_ref, uc_ref, q_ref, k_ref, v_ref, g_ref, us_s):
    hn = x * lax.rsqrt(jnp.mean(x * x, axis=-1, keepdims=True) + EPS) * nw_ref[...]
    proj = _dot(hn.astype(BF16), w_ref[...])
    h_ref[...] = proj[:, :COL_S5]
    s_ref[...] = proj[:, COL_S5:COL_AQ]
    nchunk = uc_ref.shape[0]
    for k in range(S5_W // S5_HW):
        us_s[k] = proj[:, COL_S5 + k * S5_HW:COL_S5 + (k + 1) * S5_HW]
    for tau in range(S5_L):
        for k in range(S5_W // S5_HW):
            lo = tau * S5_W + k * S5_HW
            uc_ref[:, lo:lo + S5_HW] = us_s[k, pl.ds(tau, nchunk, stride=S5_L), :].astype(BF16)
    rc = rope_ref[:, 0:128]
    ra = rope_ref[:, 128:256]
    rb = rope_ref[:, 256:384]
    for h in range(ATT_H):
        lo = h * ATT_DV
        q = proj[:, COL_AQ + lo:COL_AQ + lo + ATT_DV]
        k = proj[:, COL_AK + lo:COL_AK + lo + ATT_DV]
        q_ref[:, lo:lo + ATT_DV] = (_rope(q, rc, ra, rb) * Q_SCALE).astype(BF16)
        k_ref[:, lo:lo + ATT_DV] = _rope(k, rc, ra, rb).astype(BF16)
    v_ref[...] = proj[:, COL_AV:COL_AG].astype(BF16)
    g_ref[...] = proj[:, COL_AG:]


def _inproj_kernel(x_ref, nw_ref, w_ref, rope_ref, *outs_and_scratch):
    _project(x_ref[...], nw_ref, w_ref, rope_ref, *outs_and_scratch)


def _rope_tables(seq):
    pos = jnp.arange(seq, dtype=F32)
    inv_freq = ROPE_THETA ** (-jnp.arange(0, ROPE_DIM, 2, dtype=F32) / ROPE_DIM)
    ang = pos[:, None] * inv_freq[None, :]
    cos, sin = jnp.cos(ang), jnp.sin(ang)
    half = ROPE_DIM // 2
    zeros = jnp.zeros((seq, ATT_DH - ROPE_DIM), F32)
    zh = jnp.zeros((seq, half), F32)
    rc = jnp.concatenate([cos, cos, jnp.ones((seq, ATT_DH - ROPE_DIM), F32)], axis=1)
    ra = jnp.concatenate([-sin, zh, zeros], axis=1)
    rb = jnp.concatenate([zh, sin, zeros], axis=1)
    return jnp.concatenate([jnp.tile(t, (1, 2)) for t in (rc, ra, rb)], axis=1)


def _inproj(x2, norm_w, w_bf, rope, seq, tm):
    t = x2.shape[0]
    return pl.pallas_call(
        _inproj_kernel,
        grid=(t // tm,),
        in_specs=_project_in_specs(seq, tm),
        out_specs=_project_out_specs(tm),
        out_shape=_project_out_shapes(t),
        scratch_shapes=[pltpu.VMEM((S5_W // S5_HW, tm, S5_HW), F32)],
        compiler_params=_cparams(("parallel",)),
        name="inproj",
    )(x2, norm_w, w_bf, rope)


def _project_in_specs(seq, tm):
    nblk = seq // tm
    return [pl.BlockSpec((tm, D_MODEL), lambda i: (i, 0)),
            pl.BlockSpec((1, D_MODEL), lambda i: (0, 0)),
            pl.BlockSpec((D_MODEL, D_IN), lambda i: (0, 0)),
            pl.BlockSpec((tm, 384), lambda i: (i % nblk, 0))]


def _project_out_specs(tm):
    row = lambda i: (i, 0)
    return [pl.BlockSpec((tm, COL_S5), row), pl.BlockSpec((tm, 2 * S5_W), row),
            pl.BlockSpec((tm // S5_L, S5_CW), row),
            pl.BlockSpec((tm, ATT_W), row), pl.BlockSpec((tm, ATT_W), row),
            pl.BlockSpec((tm, ATT_W), row), pl.BlockSpec((tm, ATT_W), row)]


def _project_out_shapes(t):
    return [jax.ShapeDtypeStruct((t, COL_S5), F32), jax.ShapeDtypeStruct((t, 2 * S5_W), F32),
            jax.ShapeDtypeStruct((t // S5_L, S5_CW), BF16),
            jax.ShapeDtypeStruct((t, ATT_W), BF16), jax.ShapeDtypeStruct((t, ATT_W), BF16),
            jax.ShapeDtypeStruct((t, ATT_W), BF16), jax.ShapeDtypeStruct((t, ATT_W), F32)]


def _residual(res_ref, mh_ref, ms0_ref, ms1_ref, ma_ref, w_ref):
    acc = res_ref[...]
    acc = acc + _dot(mh_ref[...].astype(BF16), w_ref[0:HGRN_W, :])
    acc = acc + _dot(ms0_ref[...].astype(BF16), w_ref[HGRN_W:HGRN_W + S5_HW, :])
    acc = acc + _dot(ms1_ref[...].astype(BF16), w_ref[HGRN_W + S5_HW:HGRN_W + S5_W, :])
    return acc + _dot(ma_ref[...].astype(BF16), w_ref[HGRN_W + S5_W:, :])


def _mix_in_specs(tm):
    row = lambda i: (i, 0)
    return [pl.BlockSpec((tm, D_MODEL), row), pl.BlockSpec((tm, HGRN_W), row),
            pl.BlockSpec((tm, S5_HW), row), pl.BlockSpec((tm, S5_HW), row), pl.BlockSpec((tm, ATT_W), row),
            pl.BlockSpec((D_MODEL, D_MODEL), lambda i: (0, 0))]


def _outproj_final_kernel(res_ref, mh_ref, ms0_ref, ms1_ref, ma_ref, w_ref, fw_ref, o_ref):
    acc = _residual(res_ref, mh_ref, ms0_ref, ms1_ref, ma_ref, w_ref)
    o_ref[...] = acc * lax.rsqrt(jnp.mean(acc * acc, axis=-1, keepdims=True) + EPS) * fw_ref[...]


def _outproj_final(res, mh, ms0, ms1, ma, w_bf, fw, tm):
    t = res.shape[0]
    return pl.pallas_call(
        _outproj_final_kernel,
        grid=(t // tm,),
        in_specs=_mix_in_specs(tm) + [pl.BlockSpec((1, D_MODEL), lambda i: (0, 0))],
        out_specs=pl.BlockSpec((tm, D_MODEL), lambda i: (i, 0)),
        out_shape=jax.ShapeDtypeStruct((t, D_MODEL), F32),
        compiler_params=_cparams(("parallel",)),
        name="outproj",
    )(res, mh, ms0, ms1, ma, w_bf, fw)


def _outin_kernel(res_ref, mh_ref, ms0_ref, ms1_ref, ma_ref, wo_ref, nw_ref, wi_ref, rope_ref, o_ref,
                  *outs_and_scratch):
    acc = _residual(res_ref, mh_ref, ms0_ref, ms1_ref, ma_ref, wo_ref)
    o_ref[...] = acc
    _project(acc, nw_ref, wi_ref, rope_ref, *outs_and_scratch)


def _outin(res, mh, ms0, ms1, ma, wo_bf, norm_w, wi_bf, rope, seq, tm):
    t = res.shape[0]
    return pl.pallas_call(
        _outin_kernel,
        grid=(t // tm,),
        in_specs=_mix_in_specs(tm) + _project_in_specs(seq, tm)[1:],
        out_specs=[pl.BlockSpec((tm, D_MODEL), lambda i: (i, 0))] + _project_out_specs(tm),
        out_shape=[jax.ShapeDtypeStruct((t, D_MODEL), F32)] + _project_out_shapes(t),
        scratch_shapes=[pltpu.VMEM((S5_W // S5_HW, tm, S5_HW), F32)],
        compiler_params=_cparams(("parallel",)),
        name="outproj_inproj",
    )(res, mh, ms0, ms1, ma, wo_bf, norm_w, wi_bf, rope)


def _attn_kernel(q_ref, k_ref, v_ref, g_ref, lq1_ref, lk1_ref, lq2_ref, lk2_ref, sw_ref, o_ref,
                 vt_s, qq_s, acc_s, s0_s, s1_s, *, lam_init, tq, tk):
    i = pl.program_id(2)
    nkb = vt_s.shape[0]
    ntile = 2 * tq // ATT_CT
    per_map = tq // ATT_CT
    nt = (((1,), (1,)), ((), ()))

    @pl.when(i == 0)
    def _():
        for j in range(nkb):
            vt_s[j, 0:ATT_DV, :] = v_ref[j * tk:(j + 1) * tk, :].T
            vt_s[j, ATT_DV:, :] = jnp.ones((ATT_VA - ATT_DV, tk), BF16)

    q = q_ref[...]
    lane = lax.broadcasted_iota(jnp.int32, (tq, ATT_DV), 1)
    qq_s[0:tq, :] = jnp.where(lane < ATT_DH, q, jnp.zeros_like(q))
    qq_s[tq:2 * tq, :] = jnp.where(lane >= ATT_DH, q, jnp.zeros_like(q))
    acc_s[...] = jnp.zeros(acc_s.shape, F32)

    def tile_mode(c, koff):
        qs = (c % per_map) * ATT_CT
        if koff is None or qs >= koff + tk - 1:
            return "full"
        return "skip" if qs + ATT_CT - 1 < koff else "causal"

    def stage(kq, sq_s, koff_q, ssm_s, ksm, koff_sm, m_run, cmax):
        new_m, new_cmax = list(m_run), [None] * ntile
        if kq is not None:
            kb = k_ref[pl.ds(kq * tk if isinstance(kq, int) else pl.multiple_of(kq * tk, tk), tk), :]
        if ksm is not None:
            vtb = vt_s[ksm]
        for c in range(ntile):
            cols = slice(c * ATT_CT, (c + 1) * ATT_CT)
            if kq is not None and tile_mode(c, koff_q) != "skip":
                s = lax.dot_general(kb, qq_s[cols, :], nt, preferred_element_type=F32)
                if tile_mode(c, koff_q) == "causal":
                    kpos = lax.broadcasted_iota(jnp.int32, (tk, ATT_CT), 0) + koff_q
                    qpos = lax.broadcasted_iota(jnp.int32, (tk, ATT_CT), 1) + (c % per_map) * ATT_CT
                    s = jnp.where(kpos <= qpos, s, MASK_VALUE)
                sq_s[:, cols] = s
                new_cmax[c] = jnp.max(s, axis=0, keepdims=True)
            if ksm is not None and tile_mode(c, koff_sm) != "skip":
                m_new = jnp.maximum(m_run[c], cmax[c])
                alpha = jnp.exp2(m_run[c] - m_new)
                p = jnp.exp2(ssm_s[:, cols] - m_new)
                acc_s[:, cols] = alpha * acc_s[:, cols] + _dot(vtb, p.astype(BF16))
                new_m[c] = m_new
        return tuple(new_m), tuple(new_cmax)

    neg = tuple(jnp.full((1, ATT_CT), MASK_VALUE, F32) for _ in range(ntile))
    ndiag = tq // tk
    nfull = ndiag * i
    bufs = (s0_s, s1_s)
    fill = lambda cmx: tuple(neg[c] if x is None else x for c, x in enumerate(cmx))
    _, cm = stage(nfull, bufs[0], 0, None, None, None, neg, None)
    m_run = neg
    for m in range(1, ndiag):
        m_run, cm = stage(nfull + m, bufs[m % 2], m * tk, bufs[(m - 1) % 2], nfull + m - 1, (m - 1) * tk,
                          m_run, fill(cm))
    cm = fill(cm)
    last = ndiag - 1

    def pair(t, carry):
        m_run, cm0 = carry
        m_run, cm1 = stage(2 * t + 1, s1_s, None, s0_s, 2 * t, None, m_run, cm0)
        return stage(2 * t + 2, s0_s, None, s1_s, 2 * t + 1, None, m_run, cm1)

    @pl.when(i == 0)
    def _():
        stage(None, None, None, s1_s, nfull + last, last * tk, m_run, cm)

    @pl.when(i > 0)
    def _():
        m2, cm0 = stage(0, s0_s, None, s1_s, nfull + last, last * tk, m_run, cm)
        m2, cm0 = lax.fori_loop(0, nfull // 2 - 1, pair, (m2, cm0))
        m2, cml = stage(nfull - 1, s1_s, None, s0_s, nfull - 2, None, m2, cm0)
        stage(None, None, None, s1_s, nfull - 1, None, m2, cml)

    lam = (jnp.exp(jnp.sum(lq1_ref[...] * lk1_ref[...], axis=-1, keepdims=True))
           - jnp.exp(jnp.sum(lq2_ref[...] * lk2_ref[...], axis=-1, keepdims=True)) + lam_init)
    on = acc_s[0:ATT_DV, :] / acc_s[ATT_DV:ATT_DV + 1, :]
    ot = on[:, 0:tq] - lam * on[:, tq:2 * tq]
    o = ot.T
    o = o * lax.rsqrt(jnp.mean(o * o, axis=-1, keepdims=True) + EPS) * sw_ref[...] * (1.0 - lam_init)
    o_ref[...] = (o * _silu(g_ref[...])).astype(o_ref.dtype)


def _attention(q, k, v, g, lq1, lk1, lq2, lk2, sw, lam_init, batch, seq, tq):
    t = q.shape[0]
    tk = min(ATT_TK, tq // 2)
    nq = seq // tq
    qmap = lambda b, h, i: (b * nq + i, h)
    kvmap = lambda b, h, i: (b, h)
    cmap = lambda b, h, i: (0, 0)
    return pl.pallas_call(
        functools.partial(_attn_kernel, lam_init=lam_init, tq=tq, tk=tk),
        grid=(batch, ATT_H, nq),
        in_specs=[pl.BlockSpec((tq, ATT_DV), qmap), pl.BlockSpec((seq, ATT_DV), kvmap),
                  pl.BlockSpec((seq, ATT_DV), kvmap), pl.BlockSpec((tq, ATT_DV), qmap),
                  pl.BlockSpec((1, ATT_DH), cmap), pl.BlockSpec((1, ATT_DH), cmap),
                  pl.BlockSpec((1, ATT_DH), cmap), pl.BlockSpec((1, ATT_DH), cmap),
                  pl.BlockSpec((1, ATT_DV), cmap)],
        out_specs=pl.BlockSpec((tq, ATT_DV), qmap),
        out_shape=jax.ShapeDtypeStruct((t, ATT_W), BF16),
        scratch_shapes=[pltpu.VMEM((seq // tk, ATT_VA, tk), BF16), pltpu.VMEM((2 * tq, ATT_DV), BF16),
                        pltpu.VMEM((ATT_VA, 2 * tq), F32),
                        pltpu.VMEM((tk, 2 * tq), F32), pltpu.VMEM((tk, 2 * tq), F32)],
        compiler_params=_cparams(("parallel", "parallel", "arbitrary")),
        name="diffattn",
    )(q, k, v, g, lq1, lk1, lq2, lk2, sw)


def _head_ones(dtype):
    r = lax.broadcasted_iota(jnp.int32, (HGRN_W, HGRN_W), 0) // HGRN_HD
    c = lax.broadcasted_iota(jnp.int32, (HGRN_W, HGRN_W), 1) // HGRN_HD
    return (r == c).astype(dtype)


def _hgrn_kernel(h_ref, lbl_ref, nw_ref, o_ref, st_ref, q_s, v_s, c_s, cs_s, oo_s, dect_s, qt_s, vb_s, kt_s,
                 w_s, sc_s, *, layer, ts):
    nchunk = ts // CHUNK
    ngroup = ts // HGRN_GR
    npair = HGRN_W // HGRN_PW
    half = CHUNK // 2

    @pl.when(pl.program_id(1) == 0)
    def _():
        st_ref[...] = jnp.zeros(st_ref.shape, F32)

    lg = lbl_ref[...]
    e = jnp.exp(lg - jnp.max(lg, axis=0, keepdims=True))
    p = e / jnp.sum(e, axis=0, keepdims=True)
    lb = jnp.sum(p[0:layer + 1, :], axis=0, keepdims=True) - p[0:1, :]
    log2e = math.log2(math.e)
    log_lb = jnp.log(jnp.maximum(lb, LB_FLOOR)) * log2e
    log1m_lb = jnp.log1p(-lb) * log2e
    r16 = lax.broadcasted_iota(jnp.int32, (HGRN_GR, HGRN_W), 0) % CHUNK
    cpg = HGRN_GR // CHUNK
    by_chunk = lambda a: a.reshape(cpg, CHUNK, HGRN_W)

    def prepare(g, carry):
        rows = pl.ds(pl.multiple_of(g * HGRN_GR, HGRN_GR), HGRN_GR)
        x2 = h_ref[rows, HGRN_W:2 * HGRN_W] * log2e
        hi = h_ref[rows, 2 * HGRN_W:3 * HGRN_W]
        b = log1m_lb - (jnp.maximum(-x2, 0.0) + jnp.log2(1.0 + jnp.exp2(-jnp.abs(x2))))
        log_f = jnp.maximum(log_lb, b) + jnp.log2(1.0 + jnp.exp2(-jnp.abs(log_lb - b)))
        k = 1.0 - jnp.exp2(log_f)
        q = _silu(h_ref[rows, 0:HGRN_W])
        cum = log_f
        sh = 1
        while sh < CHUNK:
            cum = cum + jnp.where(r16 >= sh, pltpu.roll(cum, sh, axis=0), 0.0)
            sh *= 2
        cum = by_chunk(cum)
        last = jnp.broadcast_to(cum[:, CHUNK - 1:CHUNK, :], cum.shape)
        rem = last - cum
        k, q = by_chunk(k), by_chunk(q)
        chunks = pl.ds(pl.multiple_of(g * cpg, cpg), cpg)
        q_s[chunks] = q
        v_s[chunks] = by_chunk(hi)
        c_s[chunks] = cum
        cs_s[chunks] = cum - jnp.log2(k)
        qt_s[chunks] = (q * jnp.exp2(cum)).astype(BF16)
        vb_s[g] = hi.astype(BF16)
        dect_s[g] = jnp.exp2(last).reshape(HGRN_GR, HGRN_W).T
        kt = (k * jnp.exp2(rem)).reshape(HGRN_GR, HGRN_W).T.astype(BF16)
        for pr in range(npair):
            kt_s[pr, g] = kt[pr * HGRN_PW:(pr + 1) * HGRN_PW, :]
        return carry

    lax.fori_loop(0, ngroup, prepare, 0, unroll=2)

    t8 = lax.broadcasted_iota(jnp.int32, (half, HGRN_W), 0)

    def weights(n, carry):
        c0, c1 = c_s[n, 0:half, :], c_s[n, half:, :]
        q0, q1 = q_s[n, 0:half, :], q_s[n, half:, :]
        tiles = []
        for s in range(CHUNK):
            bs = jnp.broadcast_to(cs_s[n, s:s + 1, :], (half, HGRN_W))
            if s < half:
                d0 = jnp.exp2(c0 - bs)
                tiles.append(q0 * (jnp.where(t8 >= s, d0, 0.0) if s else d0))
                tiles.append(q1 * jnp.exp2(c1 - bs))
            else:
                d1 = jnp.exp2(c1 - bs)
                tiles.append(q1 * (jnp.where(t8 >= s - half, d1, 0.0) if s > half else d1))
        w_s[n] = jnp.concatenate(tiles, axis=0).astype(BF16)
        return carry

    lax.fori_loop(0, nchunk, weights, 0)
    sc_s[...] = _dot(w_s[...].reshape(nchunk * HGRN_WR, HGRN_W), _head_ones(BF16)).reshape(nchunk, HGRN_WR, HGRN_W)

    lane_chunk = lax.broadcasted_iota(jnp.int32, (HGRN_PW, HGRN_GR), 1) // CHUNK
    same_head = (lax.broadcasted_iota(jnp.int32, (HGRN_PW, HGRN_PW), 0) // HGRN_HD
                 == lax.broadcasted_iota(jnp.int32, (HGRN_PW, HGRN_PW), 1) // HGRN_HD)

    def group(g, carry):
        state = list(carry)
        pcols = [slice(pr * HGRN_PW, (pr + 1) * HGRN_PW) for pr in range(npair)]
        upd = [[None] * npair for _ in range(cpg)]
        for pr in range(npair):
            kt = kt_s[pr, g]
            lhs = jnp.concatenate([jnp.where(lane_chunk == j, kt, jnp.zeros_like(kt)) for j in range(cpg)], axis=0)
            inc = _dot(lhs, vb_s[g, :, pcols[pr]])
            for j in range(cpg):
                upd[j][pr] = jnp.where(same_head, inc[j * HGRN_PW:(j + 1) * HGRN_PW], 0.0)
        entering = []
        for j in range(cpg):
            entering.append([st.astype(BF16) for st in state])
            for pr in range(npair):
                dec = dect_s[g, pr * HGRN_PW:(pr + 1) * HGRN_PW, j * CHUNK:j * CHUNK + 1]
                state[pr] = state[pr] * jnp.broadcast_to(dec, (HGRN_PW, HGRN_PW)) + upd[j][pr]
        for j in range(cpg):
            n = g * cpg + j
            o0 = jnp.zeros((half, HGRN_W), F32)
            o1 = jnp.zeros((half, HGRN_W), F32)
            for s in range(CHUNK):
                vb = jnp.broadcast_to(v_s[n, s:s + 1, :], (half, HGRN_W))
                if s < half:
                    o0 = o0 + sc_s[n, s * CHUNK:s * CHUNK + half, :] * vb
                    o1 = o1 + sc_s[n, s * CHUNK + half:(s + 1) * CHUNK, :] * vb
                else:
                    o1 = o1 + sc_s[n, half * CHUNK + (s - half) * half:half * CHUNK + (s - half + 1) * half, :] * vb
            oi = jnp.concatenate([_dot(qt_s[n, :, pcols[pr]], entering[j][pr]) for pr in range(npair)], axis=1)
            oo_s[n, 0:half, :] = o0 + oi[0:half]
            oo_s[n, half:, :] = o1 + oi[half:]
        return tuple(state)

    state = lax.fori_loop(0, ngroup, group, tuple(st_ref[pr] for pr in range(npair)))
    for pr in range(npair):
        st_ref[pr] = state[pr]

    def finish(g, carry):
        rows = pl.ds(pl.multiple_of(g * HGRN_GR, HGRN_GR), HGRN_GR)
        o = oo_s[pl.ds(pl.multiple_of(g * cpg, cpg), cpg)].reshape(HGRN_GR, HGRN_W)
        sq = o * o
        sq_hi = sq.astype(BF16)
        ones = _head_ones(BF16)
        ms = (_dot(sq_hi, ones) + _dot((sq - sq_hi.astype(F32)).astype(BF16), ones)) * (1.0 / HGRN_HD)
        hg = h_ref[rows, 3 * HGRN_W:4 * HGRN_W]
        o_ref[rows, :] = (o * lax.rsqrt(ms + EPS) * nw_ref[...] * _silu(hg)).astype(o_ref.dtype)
        return carry

    lax.fori_loop(0, ngroup, finish, 0, unroll=2)


def _hgrn(hproj, lb_logits, nw_tiled, layer, batch, seq, ts):
    t = hproj.shape[0]
    ns = seq // ts
    depth = lb_logits.shape[0]
    nchunk = ts // CHUNK
    npair = HGRN_W // HGRN_PW
    blk = pltpu.VMEM((nchunk, CHUNK, HGRN_W), F32)
    return pl.pallas_call(
        functools.partial(_hgrn_kernel, layer=layer, ts=ts),
        grid=(batch, ns),
        in_specs=[pl.BlockSpec((ts, 4 * HGRN_W), lambda b, i: (b * ns + i, 0)),
                  pl.BlockSpec((depth, HGRN_W), lambda b, i: (0, 0)),
                  pl.BlockSpec((1, HGRN_W), lambda b, i: (0, 0))],
        out_specs=pl.BlockSpec((ts, HGRN_W), lambda b, i: (b * ns + i, 0)),
        out_shape=jax.ShapeDtypeStruct((t, HGRN_W), BF16),
        scratch_shapes=[pltpu.VMEM((npair, HGRN_PW, HGRN_PW), F32),
                        blk, blk, blk, blk, blk,
                        pltpu.VMEM((ts // HGRN_GR, HGRN_W, HGRN_GR), F32),
                        pltpu.VMEM((nchunk, CHUNK, HGRN_W), BF16),
                        pltpu.VMEM((ts // HGRN_GR, HGRN_GR, HGRN_W), BF16),
                        pltpu.VMEM((npair, ts // HGRN_GR, HGRN_PW, HGRN_GR), BF16),
                        pltpu.VMEM((nchunk, HGRN_WR, HGRN_W), BF16), pltpu.VMEM((nchunk, HGRN_WR, HGRN_W), F32)],
        compiler_params=_cparams(("parallel", "arbitrary")),
        name="hgrn2",
    )(hproj, lb_logits, nw_tiled)


def _s5_param_kernel(prow_ref, bt_ref, btile_ref, ctile_ref, ks_ref, gs_ref, hs_ref, d_ref):
    g = pl.program_id(0)
    are, aim = prow_ref[0:1, :], prow_ref[1:2, :]
    dt = jnp.exp(prow_ref[2:3, :])
    lr, li = dt * are, dt * aim
    mag = jnp.exp(lr)
    abr, abi = mag * jnp.cos(li), mag * jnp.sin(li)
    den = are * are + aim * aim
    nr, ni = abr - 1.0, abi
    zr = (nr * are + ni * aim) / den
    zi = (ni * are - nr * aim) / den

    cr, ci = ctile_ref[0], ctile_ref[1]
    wide = S5_L * S5_C

    npow = 2 * S5_L
    nn = lax.broadcasted_iota(jnp.int32, (npow, S5_P), 0).astype(F32)
    mgn = jnp.exp(nn * lr[:, 0:S5_P])
    pw_r, pw_i = mgn * jnp.cos(nn * li[:, 0:S5_P]), mgn * jnp.sin(nn * li[:, 0:S5_P])
    lag = lax.broadcasted_iota(jnp.int32, (npow, wide), 1) // S5_C
    nrow = lax.broadcasted_iota(jnp.int32, (npow, wide), 0)
    tn = (((0,), (0,)), ((), ()))

    def on_lags(a, first):
        sel = (nrow == lag + first).astype(BF16)
        hi = a.astype(BF16)
        lo = (a - hi.astype(F32)).astype(BF16)
        return (lax.dot_general(hi, sel, tn, preferred_element_type=F32)
                + lax.dot_general(lo, sel, tn, preferred_element_type=F32))

    er, ei = on_lags(pw_r, 0), on_lags(pw_i, 0)
    w1 = er * cr - ei * ci
    w2 = -(ei * cr + er * ci)
    zr1, zi1 = zr[:, 0:S5_P], zi[:, 0:S5_P]
    bbr = zr1 * bt_ref[0] - zi1 * bt_ref[1]
    bbi = zr1 * bt_ref[1] + zi1 * bt_ref[0]
    r0 = _dot_split(bbr, w1) + _dot_split(bbi, w2)
    mine = lax.broadcasted_iota(jnp.int32, (S5_C, wide), 1) // S5_C == g
    for j in range(S5_L):
        shift = (g * S5_C + (j + 1) * S5_C) % wide
        ks_ref[j] = jnp.where(mine, pltpu.roll(r0, shift, axis=1), 0.0).astype(ks_ref.dtype)

    e = (S5_L - 1 - lax.broadcasted_iota(jnp.int32, (S5_L, S5_ST), 0)).astype(F32)
    mg = jnp.exp(e * lr)
    rep = (lax.broadcasted_iota(jnp.int32, (wide, S5_L), 0) // S5_C
           == lax.broadcasted_iota(jnp.int32, (wide, S5_L), 1)).astype(BF16)

    def on_rows(a):
        hi = a.astype(BF16)
        return _dot(rep, hi) + _dot(rep, (a - hi.astype(F32)).astype(BF16))

    fr, fi = on_rows(mg * jnp.cos(e * li)), on_rows(mg * jnp.sin(e * li))
    btr = zr * btile_ref[0] - zi * btile_ref[1]
    bti = zr * btile_ref[1] + zi * btile_ref[0]
    re_half = lax.broadcasted_iota(jnp.int32, (wide, S5_ST), 1) < S5_P
    g128 = jnp.where(re_half, fr * btr - fi * bti, fr * bti + fi * btr)
    tile = lax.broadcasted_iota(jnp.int32, (wide, S5_SW), 1) // S5_ST
    gfull = jnp.where(tile == g, jnp.concatenate([g128] * S5_G, axis=1), 0.0)
    gs_ref[...] = gfull.reshape(S5_L, S5_C, S5_SW).astype(gs_ref.dtype)

    pr, pi = on_lags(pw_r, 1), on_lags(pw_i, 1)
    hcat = jnp.concatenate([cr * pr - ci * pi, -(cr * pi + ci * pr)], axis=0)
    mine_h = lax.broadcasted_iota(jnp.int32, (S5_ST, wide), 1) // S5_C == g
    for t in range(S5_L):
        shift = (g * S5_C + wide - t * S5_C) % wide
        hs_ref[t] = jnp.where(mine_h, pltpu.roll(hcat, shift, axis=1), 0.0).astype(hs_ref.dtype)

    mg = jnp.exp(float(S5_L) * lr)
    d_ref[0:1, :] = mg * jnp.cos(float(S5_L) * li)
    d_ref[1:2, :] = mg * jnp.sin(float(S5_L) * li)


def _s5_params(prow, bt, btile, ctile):
    g = prow.shape[0]
    wide = S5_L * S5_C
    m3 = lambda i: (i, 0, 0)
    m4 = lambda i: (i, 0, 0, 0)
    g4 = lambda i: (0, i, 0, 0)
    return pl.pallas_call(
        _s5_param_kernel,
        grid=(g,),
        in_specs=[pl.BlockSpec((None, 3, S5_ST), m3),
                  pl.BlockSpec((None, 2, S5_C, S5_P), m4), pl.BlockSpec((None, 2, wide, S5_ST), m4),
                  pl.BlockSpec((None, 2, S5_P, wide), m4)],
        out_specs=[pl.BlockSpec((S5_L, None, S5_C, wide), g4), pl.BlockSpec((S5_L, None, S5_C, S5_SW), g4),
                   pl.BlockSpec((S5_L, None, S5_ST, wide), g4), pl.BlockSpec((None, 2, S5_ST), m3)],
        out_shape=[jax.ShapeDtypeStruct((S5_L, g, S5_C, wide), BF16), jax.ShapeDtypeStruct((S5_L, g, S5_C, S5_SW), BF16),
                   jax.ShapeDtypeStruct((S5_L, g, S5_ST, wide), BF16), jax.ShapeDtypeStruct((g, 2, S5_ST), F32)],
        compiler_params=_cparams(("parallel",)),
        name="s5_params",
    )(prow, bt, btile, ctile)


def _s5_state_kernel(ucat_ref, gs_ref, d_ref, xs_ref, *, nchunk):
    width = xs_ref.shape[1]
    x = _dot(ucat_ref[...], gs_ref[...])
    sgn = jnp.where(lax.broadcasted_iota(jnp.int32, (1, width), 1) % S5_ST < S5_P, -1.0, 1.0)
    rn = lax.broadcasted_iota(jnp.int32, (nchunk, width), 0)
    pr, pi = d_ref[0:1, :], d_ref[1:2, :]

    def swap(a):
        return jnp.concatenate([pltpu.roll(a[:, k * S5_ST:(k + 1) * S5_ST], S5_P, axis=1)
                                for k in range(width // S5_ST)], axis=1)

    sh = 1
    while sh < nchunk:
        sx = jnp.where(rn >= sh, pltpu.roll(x, sh, axis=0), 0.0)
        x = x + pr * sx + (pi * sgn) * swap(sx)
        pr, pi = pr * pr - pi * pi, 2.0 * pr * pi
        sh *= 2
    xs_ref[...] = jnp.where(rn >= 1, pltpu.roll(x, 1, axis=0), 0.0).astype(xs_ref.dtype)


def _s5_state(ucat, gs, d, batch, seq):
    nchunk = seq // S5_L
    ntile = S5_SW // S5_W
    return pl.pallas_call(
        functools.partial(_s5_state_kernel, nchunk=nchunk),
        grid=(batch, ntile),
        in_specs=[pl.BlockSpec((nchunk, S5_CW), lambda b, j: (b, 0)),
                  pl.BlockSpec((S5_CW, S5_W), lambda b, j: (0, j)),
                  pl.BlockSpec((2, S5_W), lambda b, j: (0, j))],
        out_specs=pl.BlockSpec((nchunk, S5_W), lambda b, j: (b, j)),
        out_shape=jax.ShapeDtypeStruct((batch * nchunk, S5_SW), BF16),
        compiler_params=_cparams(("parallel", "arbitrary")),
        name="s5_state",
    )(ucat, gs, d)


def _s5_out_kernel(ucat_s, u0_ref, u1_ref, g0_ref, g1_ref, xs_ref, ks_ref, hs_ref, dsk_ref, w_ref, b_ref,
                   o0_ref, o1_ref, *, nb):
    step = pl.program_id(1)

    def response(tt, k):
        return (_dot(ucat_s[:, 0:(tt + 1) * S5_W], ks_ref[(S5_L - 1 - tt) * S5_W:, :])
                + _dot(xs_ref[...], hs_ref[k]))

    def finish(tt, y):
        rows = pl.ds(tt, nb, stride=S5_L)
        cat = lambda a, b: jnp.concatenate([a[rows, :], b[rows, :]], axis=1)
        y = y + dsk_ref[...] * cat(u0_ref, u1_ref)
        y = y * (0.5 * (1.0 + jnp.tanh(math.sqrt(2.0 / math.pi) * (y + 0.044715 * (y * y * y)))))
        z = _dot(y.astype(BF16), w_ref[...]) + b_ref[...]
        out = y * _sigmoid(z) * _silu(cat(g0_ref, g1_ref))
        o0_ref[rows, :] = out[:, 0:S5_HW]
        o1_ref[rows, :] = out[:, S5_HW:]

    for sg in range(S5_L // S5_TG):
        @pl.when(step == sg)
        def _():
            ys = [response(sg * S5_TG + k, k) for k in range(S5_TG)]
            for k in range(S5_TG):
                finish(sg * S5_TG + k, ys[k])


def _s5_out(ucat, sproj, xs, ks, hs, d_skip, w_bf, b, rb):
    t = sproj.shape[0]
    nb = rb // S5_L
    cst = lambda r, i: (0, 0)
    return pl.pallas_call(
        functools.partial(_s5_out_kernel, nb=nb),
        grid=(t // rb, S5_L // S5_TG),
        in_specs=[pl.BlockSpec((nb, S5_CW), lambda r, i: (r, 0)),
                  pl.BlockSpec((rb, S5_HW), lambda r, i: (r, 0)), pl.BlockSpec((rb, S5_HW), lambda r, i: (r, 1)),
                  pl.BlockSpec((rb, S5_HW), lambda r, i: (r, 2)), pl.BlockSpec((rb, S5_HW), lambda r, i: (r, 3)),
                  pl.BlockSpec((nb, S5_SW), lambda r, i: (r, 0)),
                  pl.BlockSpec((S5_CW, S5_W), cst),
                  pl.BlockSpec((S5_TG, S5_SW, S5_W), lambda r, i: (i, 0, 0)),
                  pl.BlockSpec((1, S5_W), cst), pl.BlockSpec((S5_W, S5_W), cst), pl.BlockSpec((1, S5_W), cst)],
        out_specs=[pl.BlockSpec((rb, S5_HW), lambda r, i: (r, 0))] * 2,
        out_shape=[jax.ShapeDtypeStruct((t, S5_HW), F32)] * 2,
        compiler_params=_cparams(("parallel", "arbitrary")),
        name="s5_out",
    )(ucat, sproj, sproj, sproj, sproj, xs, ks, hs, d_skip, w_bf, b)


def _s5(ucat, sproj, a_re, a_im, b_re, b_im, c_re, c_im, log_dt, d_skip, glu_w, glu_b, batch, seq, rb):
    ldt = jnp.broadcast_to(log_dt[:, None], a_re.shape)
    twice = lambda a: jnp.concatenate([a, a], axis=-1)
    prow = twice(jnp.stack([a_re, a_im, ldt], axis=1))
    bt = jnp.stack([b_re, b_im], axis=1).transpose(0, 1, 3, 2)
    btile = twice(jnp.tile(bt, (1, 1, S5_L, 1)))
    ctile = jnp.tile(jnp.stack([c_re, c_im], axis=1).transpose(0, 1, 3, 2), (1, 1, 1, S5_L))
    ks, gs, hs, d = _s5_params(prow, bt, btile, ctile)
    ks = ks.reshape(S5_CW, S5_W)
    gs = gs.reshape(S5_CW, S5_SW)
    hs = hs.reshape(S5_L, S5_SW, S5_W)
    d = d.transpose(1, 0, 2).reshape(2, S5_SW)
    xs = _s5_state(ucat, gs, d, batch, seq)
    return _s5_out(ucat, sproj, xs, ks, hs, d_skip[None, :], glu_w.astype(BF16), glu_b[None, :], rb)


def _pick(n, pref):
    b = min(n, pref)
    while n % b:
        b //= 2
    return b


def kernel(x, norm_w, w_in, w_out, hgrn_lb_logits, hgrn_norm_w, s5_a_re, s5_a_im, s5_b_re, s5_b_im, s5_c_re, s5_c_im, s5_log_dt, s5_d, s5_glu_w, s5_glu_b, diff_lq1, diff_lk1, diff_lq2, diff_lk2, diff_subln_w, final_norm_w):
    batch, seq, _ = x.shape
    depth = norm_w.shape[0]
    t = batch * seq
    tm = _pick(seq, 512)
    tq = _pick(seq, ATT_TQ)
    ts = _pick(seq, 512)
    rope = _rope_tables(seq)
    h_res = x.astype(F32).reshape(t, D_MODEL)
    nw = lambda l: norm_w[l][None, :].astype(F32)
    hp, sp, uc, aq, ak, av, ag = _inproj(h_res, nw(0), w_in[0].astype(BF16), rope, seq, tm)
    for l in range(depth):
        mix_h = _hgrn(hp, hgrn_lb_logits.astype(F32), jnp.tile(hgrn_norm_w[l].astype(F32), HGRN_W // HGRN_HD)[None, :],
                      l, batch, seq, ts)
        mix_s = _s5(uc, sp, s5_a_re[l], s5_a_im[l], s5_b_re[l], s5_b_im[l], s5_c_re[l], s5_c_im[l], s5_log_dt[l],
                    s5_d[l], s5_glu_w[l], s5_glu_b[l], batch, seq, _pick(seq, 4096))
        lam_init = 0.8 - 0.6 * math.exp(-0.3 * l)
        mix_a = _attention(aq, ak, av, ag, diff_lq1[l][None, :], diff_lk1[l][None, :], diff_lq2[l][None, :],
                           diff_lk2[l][None, :], diff_subln_w[l][None, :], lam_init, batch, seq, tq)
        mixes = (mix_h, mix_s[0], mix_s[1], mix_a, w_out[l].astype(BF16))
        if l + 1 < depth:
            h_res, hp, sp, uc, aq, ak, av, ag = _outin(h_res, *mixes, nw(l + 1), w_in[l + 1].astype(BF16), rope, seq, tm)
        else:
            h_res = _outproj_final(h_res, *mixes, final_norm_w[None, :].astype(F32), _pick(seq, 1024))
    return h_res.reshape(batch, seq, D_MODEL).astype(x.dtype)
```

```python
import functools
import math

import jax
import jax.numpy as jnp
from jax import lax
from jax.experimental import pallas as pl
from jax.experimental.pallas import tpu as pltpu

F32 = jnp.float32
BF16 = jnp.bfloat16

D_MODEL = 1024
HGRN_W = 256
HGRN_HD = 64
CHUNK = 16
HGRN_WR = 8 * 16 + 8 * 8
HGRN_PW = 2 * HGRN_HD
HGRN_GR = 128
LB_FLOOR = 1e-30
S5_W = 256
S5_C = 16
S5_G = S5_W // S5_C
S5_P = 64
S5_L = 16
S5_ST = 2 * S5_P
S5_SW = S5_G * S5_ST
S5_CW = S5_L * S5_W
S5_TG = 8
S5_HW = 128
ATT_W = 512
ATT_DH = 64
ATT_DV = 128
ATT_H = ATT_W // ATT_DV
ATT_CT = 256
ATT_VA = ATT_DV + 16
ATT_TQ = 2048
ATT_TK = 512
ROPE_DIM = 16
ROPE_THETA = 500000.0
MASK_VALUE = -1e30
Q_SCALE = ATT_DH ** -0.5 * math.log2(math.e)
EPS = 1e-6
D_IN = 4 * HGRN_W + 2 * S5_W + 4 * ATT_W
COL_S5 = 4 * HGRN_W
COL_AQ = COL_S5 + 2 * S5_W
COL_AK = COL_AQ + ATT_W
COL_AV = COL_AK + ATT_W
COL_AG = COL_AV + ATT_W

VMEM_LIMIT = 56 * 1024 * 1024


def _cparams(sem):
    return pltpu.CompilerParams(dimension_semantics=sem, vmem_limit_bytes=VMEM_LIMIT)


def _sigmoid(x):
    return 1.0 / (1.0 + jnp.exp(-x))


def _softplus_neg_abs(x):
    return jnp.log(1.0 + jnp.exp(-jnp.abs(x)))


def _silu(x):
    return x * _sigmoid(x)


def _dot(a, b):
    return jnp.dot(a, b, preferred_element_type=F32)


def _dot_split(a, b):
    ah = a.astype(BF16)
    al = (a - ah.astype(F32)).astype(BF16)
    bh = b.astype(BF16)
    bl = (b - bh.astype(F32)).astype(BF16)
    return _dot(ah, bh) + _dot(ah, bl) + _dot(al, bh)


def _rope(t, rc, ra, rb):
    return t * rc + pltpu.roll(t, 128 - ROPE_DIM // 2, axis=1) * ra + pltpu.roll(t, ROPE_DIM // 2, axis=1) * rb


def _project(x, nw_ref, w_ref, rope_ref, h_ref, s_ref, uc_ref, q_ref, k_ref, v_ref, g_ref, us_s):
    hn = x * lax.rsqrt(jnp.mean(x * x, axis=-1, keepdims=True) + EPS) * nw_ref[...]
    proj = _dot(hn.astype(BF16), w_ref[...])
    h_ref[...] = proj[:, :COL_S5]
    s_ref[...] = proj[:, COL_S5:COL_AQ]
    nchunk = uc_ref.shape[0]
    for k in range(S5_W // S5_HW):
        us_s[k] = proj[:, COL_S5 + k * S5_HW:COL_S5 + (k + 1) * S5_HW]
    for tau in range(S5_L):
        for k in range(S5_W // S5_HW):
            lo = tau * S5_W + k * S5_HW
            uc_ref[:, lo:lo + S5_HW] = us_s[k, pl.ds(tau, nchunk, stride=S5_L), :].astype(BF16)
    rc = rope_ref[:, 0:128]
    ra = rope_ref[:, 128:256]
    rb = rope_ref[:, 256:384]
    for h in range(ATT_H):
        lo = h * ATT_DV
        q = proj[:, COL_AQ + lo:COL_AQ + lo + ATT_DV]
        k = proj[:, COL_AK + lo:COL_AK + lo + ATT_DV]
        q_ref[:, lo:lo + ATT_DV] = (_rope(q, rc, ra, rb) * Q_SCALE).astype(BF16)
        k_ref[:, lo:lo + ATT_DV] = _rope(k, rc, ra, rb).astype(BF16)
    v_ref[...] = proj[:, COL_AV:COL_AG].astype(BF16)
    g_ref[...] = proj[:, COL_AG:]


def _inproj_kernel(x_ref, nw_ref, w_ref, rope_ref, *outs_and_scratch):
    _project(x_ref[...], nw_ref, w_ref, rope_ref, *outs_and_scratch)


def _rope_tables(seq):
    pos = jnp.arange(seq, dtype=F32)
    inv_freq = ROPE_THETA ** (-jnp.arange(0, ROPE_DIM, 2, dtype=F32) / ROPE_DIM)
    ang = pos[:, None] * inv_freq[None, :]
    cos, sin = jnp.cos(ang), jnp.sin(ang)
    half = ROPE_DIM // 2
    zeros = jnp.zeros((seq, ATT_DH - ROPE_DIM), F32)
    zh = jnp.zeros((seq, half), F32)
    rc = jnp.concatenate([cos, cos, jnp.ones((seq, ATT_DH - ROPE_DIM), F32)], axis=1)
    ra = jnp.concatenate([-sin, zh, zeros], axis=1)
    rb = jnp.concatenate([zh, sin, zeros], axis=1)
    return jnp.concatenate([jnp.tile(t, (1, 2)) for t in (rc, ra, rb)], axis=1)


def _inproj(x2, norm_w, w_bf, rope, seq, tm):
    t = x2.shape[0]
    return pl.pallas_call(
        _inproj_kernel,
        grid=(t // tm,),
        in_specs=_project_in_specs(seq, tm),
        out_specs=_project_out_specs(tm),
        out_shape=_project_out_shapes(t),
        scratch_shapes=[pltpu.VMEM((S5_W // S5_HW, tm, S5_HW), F32)],
        compiler_params=_cparams(("parallel",)),
        name="inproj",
    )(x2, norm_w, w_bf, rope)


def _project_in_specs(seq, tm):
    nblk = seq // tm
    return [pl.BlockSpec((tm, D_MODEL), lambda i: (i, 0)),
            pl.BlockSpec((1, D_MODEL), lambda i: (0, 0)),
            pl.BlockSpec((D_MODEL, D_IN), lambda i: (0, 0)),
            pl.BlockSpec((tm, 384), lambda i: (i % nblk, 0))]


def _project_out_specs(tm):
    row = lambda i: (i, 0)
    return [pl.BlockSpec((tm, COL_S5), row), pl.BlockSpec((tm, 2 * S5_W), row),
            pl.BlockSpec((tm // S5_L, S5_CW), row),
            pl.BlockSpec((tm, ATT_W), row), pl.BlockSpec((tm, ATT_W), row),
            pl.BlockSpec((tm, ATT_W), row), pl.BlockSpec((tm, ATT_W), row)]


def _project_out_shapes(t):
    return [jax.ShapeDtypeStruct((t, COL_S5), F32), jax.ShapeDtypeStruct((t, 2 * S5_W), F32),
            jax.ShapeDtypeStruct((t // S5_L, S5_CW), BF16),
            jax.ShapeDtypeStruct((t, ATT_W), BF16), jax.ShapeDtypeStruct((t, ATT_W), BF16),
            jax.ShapeDtypeStruct((t, ATT_W), BF16), jax.ShapeDtypeStruct((t, ATT_W), F32)]


def _residual(res_ref, mh_ref, ms0_ref, ms1_ref, ma_ref, w_ref):
    acc = res_ref[...]
    acc = acc + _dot(mh_ref[...].astype(BF16), w_ref[0:HGRN_W, :])
    acc = acc + _dot(ms0_ref[...].astype(BF16), w_ref[HGRN_W:HGRN_W + S5_HW, :])
    acc = acc + _dot(ms1_ref[...].astype(BF16), w_ref[HGRN_W + S5_HW:HGRN_W + S5_W, :])
    return acc + _dot(ma_ref[...].astype(BF16), w_ref[HGRN_W + S5_W:, :])


def _mix_in_specs(tm):
    row = lambda i: (i, 0)
    return [pl.BlockSpec((tm, D_MODEL), row), pl.BlockSpec((tm, HGRN_W), row),
            pl.BlockSpec((tm, S5_HW), row), pl.BlockSpec((tm, S5_HW), row), pl.BlockSpec((tm, ATT_W), row),
            pl.BlockSpec((D_MODEL, D_MODEL), lambda i: (0, 0))]


def _outproj_final_kernel(res_ref, mh_ref, ms0_ref, ms1_ref, ma_ref, w_ref, fw_ref, o_ref):
    acc = _residual(res_ref, mh_ref, ms0_ref, ms1_ref, ma_ref, w_ref)
    o_ref[...] = acc * lax.rsqrt(jnp.mean(acc * acc, axis=-1, keepdims=True) + EPS) * fw_ref[...]


def _outproj_final(res, mh, ms0, ms1, ma, w_bf, fw, tm):
    t = res.shape[0]
    return pl.pallas_call(
        _outproj_final_kernel,
        grid=(t // tm,),
        in_specs=_mix_in_specs(tm) + [pl.BlockSpec((1, D_MODEL), lambda i: (0, 0))],
        out_specs=pl.BlockSpec((tm, D_MODEL), lambda i: (i, 0)),
        out_shape=jax.ShapeDtypeStruct((t, D_MODEL), F32),
        compiler_params=_cparams(("parallel",)),
        name="outproj",
    )(res, mh, ms0, ms1, ma, w_bf, fw)


def _outin_kernel(res_ref, mh_ref, ms0_ref, ms1_ref, ma_ref, wo_ref, nw_ref, wi_ref, rope_ref, o_ref,
                  *outs_and_scratch):
    acc = _residual(res_ref, mh_ref, ms0_ref, ms1_ref, ma_ref, wo_ref)
    o_ref[...] = acc
    _project(acc, nw_ref, wi_ref, rope_ref, *outs_and_scratch)


def _outin(res, mh, ms0, ms1, ma, wo_bf, norm_w, wi_bf, rope, seq, tm):
    t = res.shape[0]
    return pl.pallas_call(
        _outin_kernel,
        grid=(t // tm,),
        in_specs=_mix_in_specs(tm) + _project_in_specs(seq, tm)[1:],
        out_specs=[pl.BlockSpec((tm, D_MODEL), lambda i: (i, 0))] + _project_out_specs(tm),
        out_shape=[jax.ShapeDtypeStruct((t, D_MODEL), F32)] + _project_out_shapes(t),
        scratch_shapes=[pltpu.VMEM((S5_W // S5_HW, tm, S5_HW), F32)],
        compiler_params=_cparams(("parallel",)),
        name="outproj_inproj",
    )(res, mh, ms0, ms1, ma, wo_bf, norm_w, wi_bf, rope)


def _attn_kernel(q_ref, k_ref, v_ref, g_ref, lq1_ref, lk1_ref, lq2_ref, lk2_ref, sw_ref, o_ref,
                 vt_s, qq_s, acc_s, s0_s, s1_s, *, lam_init, tq, tk):
    i = pl.program_id(2)
    nkb = vt_s.shape[0]
    ntile = 2 * tq // ATT_CT
    per_map = tq // ATT_CT
    nt = (((1,), (1,)), ((), ()))

    @pl.when(i == 0)
    def _():
        for j in range(nkb):
            vt_s[j, 0:ATT_DV, :] = v_ref[j * tk:(j + 1) * tk, :].T
            vt_s[j, ATT_DV:, :] = jnp.ones((ATT_VA - ATT_DV, tk), BF16)

    q = q_ref[...]
    lane = lax.broadcasted_iota(jnp.int32, (tq, ATT_DV), 1)
    qq_s[0:tq, :] = jnp.where(lane < ATT_DH, q, jnp.zeros_like(q))
    qq_s[tq:2 * tq, :] = jnp.where(lane >= ATT_DH, q, jnp.zeros_like(q))
    acc_s[...] = jnp.zeros(acc_s.shape, F32)

    def tile_mode(c, koff):
        qs = (c % per_map) * ATT_CT
        if koff is None or qs >= koff + tk - 1:
            return "full"
        return "skip" if qs + ATT_CT - 1 < koff else "causal"

    def stage(kq, sq_s, koff_q, ssm_s, ksm, koff_sm, m_run, cmax):
        new_m, new_cmax = list(m_run), [None] * ntile
        if kq is not None:
            kb = k_ref[pl.ds(kq * tk if isinstance(kq, int) else pl.multiple_of(kq * tk, tk), tk), :]
        if ksm is not None:
            vtb = vt_s[ksm]
        for c in range(ntile):
            cols = slice(c * ATT_CT, (c + 1) * ATT_CT)
            if kq is not None and tile_mode(c, koff_q) != "skip":
                s = lax.dot_general(kb, qq_s[cols, :], nt, preferred_element_type=F32)
                if tile_mode(c, koff_q) == "causal":
                    kpos = lax.broadcasted_iota(jnp.int32, (tk, ATT_CT), 0) + koff_q
                    qpos = lax.broadcasted_iota(jnp.int32, (tk, ATT_CT), 1) + (c % per_map) * ATT_CT
                    s = jnp.where(kpos <= qpos, s, MASK_VALUE)
                sq_s[:, cols] = s
                new_cmax[c] = jnp.max(s, axis=0, keepdims=True)
            if ksm is not None and tile_mode(c, koff_sm) != "skip":
                m_new = jnp.maximum(m_run[c], cmax[c])
                alpha = jnp.exp2(m_run[c] - m_new)
                p = jnp.exp2(ssm_s[:, cols] - m_new)
                acc_s[:, cols] = alpha * acc_s[:, cols] + _dot(vtb, p.astype(BF16))
                new_m[c] = m_new
        return tuple(new_m), tuple(new_cmax)

    neg = tuple(jnp.full((1, ATT_CT), MASK_VALUE, F32) for _ in range(ntile))
    ndiag = tq // tk
    nfull = ndiag * i
    bufs = (s0_s, s1_s)
    fill = lambda cmx: tuple(neg[c] if x is None else x for c, x in enumerate(cmx))
    _, cm = stage(nfull, bufs[0], 0, None, None, None, neg, None)
    m_run = neg
    for m in range(1, ndiag):
        m_run, cm = stage(nfull + m, bufs[m % 2], m * tk, bufs[(m - 1) % 2], nfull + m - 1, (m - 1) * tk,
                          m_run, fill(cm))
    cm = fill(cm)
    last = ndiag - 1

    def pair(t, carry):
        m_run, cm0 = carry
        m_run, cm1 = stage(2 * t + 1, s1_s, None, s0_s, 2 * t, None, m_run, cm0)
        return stage(2 * t + 2, s0_s, None, s1_s, 2 * t + 1, None, m_run, cm1)

    @pl.when(i == 0)
    def _():
        stage(None, None, None, s1_s, nfull + last, last * tk, m_run, cm)

    @pl.when(i > 0)
    def _():
        m2, cm0 = stage(0, s0_s, None, s1_s, nfull + last, last * tk, m_run, cm)
        m2, cm0 = lax.fori_loop(0, nfull // 2 - 1, pair, (m2, cm0))
        m2, cml = stage(nfull - 1, s1_s, None, s0_s, nfull - 2, None, m2, cm0)
        stage(None, None, None, s1_s, nfull - 1, None, m2, cml)

    lam = (jnp.exp(jnp.sum(lq1_ref[...] * lk1_ref[...], axis=-1, keepdims=True))
           - jnp.exp(jnp.sum(lq2_ref[...] * lk2_ref[...], axis=-1, keepdims=True)) + lam_init)
    on = acc_s[0:ATT_DV, :] / acc_s[ATT_DV:ATT_DV + 1, :]
    ot = on[:, 0:tq] - lam * on[:, tq:2 * tq]
    o = ot.T
    o = o * lax.rsqrt(jnp.mean(o * o, axis=-1, keepdims=True) + EPS) * sw_ref[...] * (1.0 - lam_init)
    o_ref[...] = (o * _silu(g_ref[...])).astype(o_ref.dtype)


def _attention(q, k, v, g, lq1, lk1, lq2, lk2, sw, lam_init, batch, seq, tq):
    t = q.shape[0]
    tk = min(ATT_TK, tq // 2)
    nq = seq // tq
    qmap = lambda b, h, i: (b * nq + i, h)
    kvmap = lambda b, h, i: (b, h)
    cmap = lambda b, h, i: (0, 0)
    return pl.pallas_call(
        functools.partial(_attn_kernel, lam_init=lam_init, tq=tq, tk=tk),
        grid=(batch, ATT_H, nq),
        in_specs=[pl.BlockSpec((tq, ATT_DV), qmap), pl.BlockSpec((seq, ATT_DV), kvmap),
                  pl.BlockSpec((seq, ATT_DV), kvmap), pl.BlockSpec((tq, ATT_DV), qmap),
                  pl.BlockSpec((1, ATT_DH), cmap), pl.BlockSpec((1, ATT_DH), cmap),
                  pl.BlockSpec((1, ATT_DH), cmap), pl.BlockSpec((1, ATT_DH), cmap),
                  pl.BlockSpec((1, ATT_DV), cmap)],
        out_specs=pl.BlockSpec((tq, ATT_DV), qmap),
        out_shape=jax.ShapeDtypeStruct((t, ATT_W), BF16),
        scratch_shapes=[pltpu.VMEM((seq // tk, ATT_VA, tk), BF16), pltpu.VMEM((2 * tq, ATT_DV), BF16),
                        pltpu.VMEM((ATT_VA, 2 * tq), F32),
                        pltpu.VMEM((tk, 2 * tq), F32), pltpu.VMEM((tk, 2 * tq), F32)],
        compiler_params=_cparams(("parallel", "parallel", "arbitrary")),
        name="diffattn",
    )(q, k, v, g, lq1, lk1, lq2, lk2, sw)


def _head_ones(dtype):
    r = lax.broadcasted_iota(jnp.int32, (HGRN_W, HGRN_W), 0) // HGRN_HD
    c = lax.broadcasted_iota(jnp.int32, (HGRN_W, HGRN_W), 1) // HGRN_HD
    return (r == c).astype(dtype)


def _hgrn_kernel(h_ref, lbl_ref, nw_ref, o_ref, st_ref, q_s, v_s, c_s, cs_s, oo_s, dect_s, qt_s, vb_s, kt_s,
                 w_s, sc_s, *, layer, ts):
    nchunk = ts // CHUNK
    ngroup = ts // HGRN_GR
    npair = HGRN_W // HGRN_PW
    half = CHUNK // 2

    @pl.when(pl.program_id(1) == 0)
    def _():
        st_ref[...] = jnp.zeros(st_ref.shape, F32)

    lg = lbl_ref[...]
    e = jnp.exp(lg - jnp.max(lg, axis=0, keepdims=True))
    p = e / jnp.sum(e, axis=0, keepdims=True)
    lb = jnp.sum(p[0:layer + 1, :], axis=0, keepdims=True) - p[0:1, :]
    log2e = math.log2(math.e)
    log_lb = jnp.log(jnp.maximum(lb, LB_FLOOR)) * log2e
    log1m_lb = jnp.log1p(-lb) * log2e
    r16 = lax.broadcasted_iota(jnp.int32, (HGRN_GR, HGRN_W), 0) % CHUNK
    cpg = HGRN_GR // CHUNK
    by_chunk = lambda a: a.reshape(cpg, CHUNK, HGRN_W)

    def prepare(g, carry):
        rows = pl.ds(pl.multiple_of(g * HGRN_GR, HGRN_GR), HGRN_GR)
        x2 = h_ref[rows, HGRN_W:2 * HGRN_W] * log2e
        hi = h_ref[rows, 2 * HGRN_W:3 * HGRN_W]
        b = log1m_lb - (jnp.maximum(-x2, 0.0) + jnp.log2(1.0 + jnp.exp2(-jnp.abs(x2))))
        log_f = jnp.maximum(log_lb, b) + jnp.log2(1.0 + jnp.exp2(-jnp.abs(log_lb - b)))
        k = 1.0 - jnp.exp2(log_f)
        q = _silu(h_ref[rows, 0:HGRN_W])
        cum = log_f
        sh = 1
        while sh < CHUNK:
            cum = cum + jnp.where(r16 >= sh, pltpu.roll(cum, sh, axis=0), 0.0)
            sh *= 2
        cum = by_chunk(cum)
        last = jnp.broadcast_to(cum[:, CHUNK - 1:CHUNK, :], cum.shape)
        rem = last - cum
        k, q = by_chunk(k), by_chunk(q)
        chunks = pl.ds(pl.multiple_of(g * cpg, cpg), cpg)
        q_s[chunks] = q
        v_s[chunks] = by_chunk(hi)
        c_s[chunks] = cum
        cs_s[chunks] = cum - jnp.log2(k)
        qt_s[chunks] = (q * jnp.exp2(cum)).astype(BF16)
        vb_s[g] = hi.astype(BF16)
        dect_s[g] = jnp.exp2(last).reshape(HGRN_GR, HGRN_W).T
        kt = (k * jnp.exp2(rem)).reshape(HGRN_GR, HGRN_W).T.astype(BF16)
        for pr in range(npair):
            kt_s[pr, g] = kt[pr * HGRN_PW:(pr + 1) * HGRN_PW, :]
        return carry

    lax.fori_loop(0, ngroup, prepare, 0, unroll=2)

    t8 = lax.broadcasted_iota(jnp.int32, (half, HGRN_W), 0)

    def weights(n, carry):
        c0, c1 = c_s[n, 0:half, :], c_s[n, half:, :]
        q0, q1 = q_s[n, 0:half, :], q_s[n, half:, :]
        tiles = []
        for s in range(CHUNK):
            bs = jnp.broadcast_to(cs_s[n, s:s + 1, :], (half, HGRN_W))
            if s < half:
                d0 = jnp.exp2(c0 - bs)
                tiles.append(q0 * (jnp.where(t8 >= s, d0, 0.0) if s else d0))
                tiles.append(q1 * jnp.exp2(c1 - bs))
            else:
                d1 = jnp.exp2(c1 - bs)
                tiles.append(q1 * (jnp.where(t8 >= s - half, d1, 0.0) if s > half else d1))
        w_s[n] = jnp.concatenate(tiles, axis=0).astype(BF16)
        return carry

    lax.fori_loop(0, nchunk, weights, 0)
    sc_s[...] = _dot(w_s[...].reshape(nchunk * HGRN_WR, HGRN_W), _head_ones(BF16)).reshape(nchunk, HGRN_WR, HGRN_W)

    lane_chunk = lax.broadcasted_iota(jnp.int32, (HGRN_PW, HGRN_GR), 1) // CHUNK
    same_head = (lax.broadcasted_iota(jnp.int32, (HGRN_PW, HGRN_PW), 0) // HGRN_HD
                 == lax.broadcasted_iota(jnp.int32, (HGRN_PW, HGRN_PW), 1) // HGRN_HD)

    def group(g, carry):
        state = list(carry)
        pcols = [slice(pr * HGRN_PW, (pr + 1) * HGRN_PW) for pr in range(npair)]
        upd = [[None] * npair for _ in range(cpg)]
        for pr in range(npair):
            kt = kt_s[pr, g]
            lhs = jnp.concatenate([jnp.where(lane_chunk == j, kt, jnp.zeros_like(kt)) for j in range(cpg)], axis=0)
            inc = _dot(lhs, vb_s[g, :, pcols[pr]])
            for j in range(cpg):
                upd[j][pr] = jnp.where(same_head, inc[j * HGRN_PW:(j + 1) * HGRN_PW], 0.0)
        entering = []
        for j in range(cpg):
            entering.append([st.astype(BF16) for st in state])
            for pr in range(npair):
                dec = dect_s[g, pr * HGRN_PW:(pr + 1) * HGRN_PW, j * CHUNK:j * CHUNK + 1]
                state[pr] = state[pr] * jnp.broadcast_to(dec, (HGRN_PW, HGRN_PW)) + upd[j][pr]
        for j in range(cpg):
            n = g * cpg + j
            o0 = jnp.zeros((half, HGRN_W), F32)
            o1 = jnp.zeros((half, HGRN_W), F32)
            for s in range(CHUNK):
                vb = jnp.broadcast_to(v_s[n, s:s + 1, :], (half, HGRN_W))
                if s < half:
                    o0 = o0 + sc_s[n, s * CHUNK:s * CHUNK + half, :] * vb
                    o1 = o1 + sc_s[n, s * CHUNK + half:(s + 1) * CHUNK, :] * vb
                else:
                    o1 = o1 + sc_s[n, half * CHUNK + (s - half) * half:half * CHUNK + (s - half + 1) * half, :] * vb
            oi = jnp.concatenate([_dot(qt_s[n, :, pcols[pr]], entering[j][pr]) for pr in range(npair)], axis=1)
            oo_s[n, 0:half, :] = o0 + oi[0:half]
            oo_s[n, half:, :] = o1 + oi[half:]
        return tuple(state)

    state = lax.fori_loop(0, ngroup, group, tuple(st_ref[pr] for pr in range(npair)), unroll=2)
    for pr in range(npair):
        st_ref[pr] = state[pr]

    def finish(g, carry):
        rows = pl.ds(pl.multiple_of(g * HGRN_GR, HGRN_GR), HGRN_GR)
        o = oo_s[pl.ds(pl.multiple_of(g * cpg, cpg), cpg)].reshape(HGRN_GR, HGRN_W)
        sq = o * o
        sq_hi = sq.astype(BF16)
        ones = _head_ones(BF16)
        ms = (_dot(sq_hi, ones) + _dot((sq - sq_hi.astype(F32)).astype(BF16), ones)) * (1.0 / HGRN_HD)
        hg = h_ref[rows, 3 * HGRN_W:4 * HGRN_W]
        o_ref[rows, :] = (o * lax.rsqrt(ms + EPS) * nw_ref[...] * _silu(hg)).astype(o_ref.dtype)
        return carry

    lax.fori_loop(0, ngroup, finish, 0, unroll=2)


def _hgrn(hproj, lb_logits, nw_tiled, layer, batch, seq, ts):
    t = hproj.shape[0]
    ns = seq // ts
    depth = lb_logits.shape[0]
    nchunk = ts // CHUNK
    npair = HGRN_W // HGRN_PW
    blk = pltpu.VMEM((nchunk, CHUNK, HGRN_W), F32)
    return pl.pallas_call(
        functools.partial(_hgrn_kernel, layer=layer, ts=ts),
        grid=(batch, ns),
        in_specs=[pl.BlockSpec((ts, 4 * HGRN_W), lambda b, i: (b * ns + i, 0)),
                  pl.BlockSpec((depth, HGRN_W), lambda b, i: (0, 0)),
                  pl.BlockSpec((1, HGRN_W), lambda b, i: (0, 0))],
        out_specs=pl.BlockSpec((ts, HGRN_W), lambda b, i: (b * ns + i, 0)),
        out_shape=jax.ShapeDtypeStruct((t, HGRN_W), BF16),
        scratch_shapes=[pltpu.VMEM((npair, HGRN_PW, HGRN_PW), F32),
                        blk, blk, blk, blk, blk,
                        pltpu.VMEM((ts // HGRN_GR, HGRN_W, HGRN_GR), F32),
                        pltpu.VMEM((nchunk, CHUNK, HGRN_W), BF16),
                        pltpu.VMEM((ts // HGRN_GR, HGRN_GR, HGRN_W), BF16),
                        pltpu.VMEM((npair, ts // HGRN_GR, HGRN_PW, HGRN_GR), BF16),
                        pltpu.VMEM((nchunk, HGRN_WR, HGRN_W), BF16), pltpu.VMEM((nchunk, HGRN_WR, HGRN_W), F32)],
        compiler_params=_cparams(("parallel", "arbitrary")),
        name="hgrn2",
    )(hproj, lb_logits, nw_tiled)


def _s5_param_kernel(prow_ref, bt_ref, btile_ref, ctile_ref, ks_ref, gs_ref, hs_ref, d_ref):
    g = pl.program_id(0)
    are, aim = prow_ref[0:1, :], prow_ref[1:2, :]
    dt = jnp.exp(prow_ref[2:3, :])
    lr, li = dt * are, dt * aim
    mag = jnp.exp(lr)
    abr, abi = mag * jnp.cos(li), mag * jnp.sin(li)
    den = are * are + aim * aim
    nr, ni = abr - 1.0, abi
    zr = (nr * are + ni * aim) / den
    zi = (ni * are - nr * aim) / den

    cr, ci = ctile_ref[0], ctile_ref[1]
    wide = S5_L * S5_C

    npow = 2 * S5_L
    nn = lax.broadcasted_iota(jnp.int32, (npow, S5_P), 0).astype(F32)
    mgn = jnp.exp(nn * lr[:, 0:S5_P])
    pw_r, pw_i = mgn * jnp.cos(nn * li[:, 0:S5_P]), mgn * jnp.sin(nn * li[:, 0:S5_P])
    lag = lax.broadcasted_iota(jnp.int32, (npow, wide), 1) // S5_C
    nrow = lax.broadcasted_iota(jnp.int32, (npow, wide), 0)
    tn = (((0,), (0,)), ((), ()))

    def on_lags(a, first):
        sel = (nrow == lag + first).astype(BF16)
        hi = a.astype(BF16)
        lo = (a - hi.astype(F32)).astype(BF16)
        return (lax.dot_general(hi, sel, tn, preferred_element_type=F32)
                + lax.dot_general(lo, sel, tn, preferred_element_type=F32))

    er, ei = on_lags(pw_r, 0), on_lags(pw_i, 0)
    w1 = er * cr - ei * ci
    w2 = -(ei * cr + er * ci)
    zr1, zi1 = zr[:, 0:S5_P], zi[:, 0:S5_P]
    bbr = zr1 * bt_ref[0] - zi1 * bt_ref[1]
    bbi = zr1 * bt_ref[1] + zi1 * bt_ref[0]
    r0 = _dot_split(bbr, w1) + _dot_split(bbi, w2)
    mine = lax.broadcasted_iota(jnp.int32, (S5_C, wide), 1) // S5_C == g
    for j in range(S5_L):
        shift = (g * S5_C + (j + 1) * S5_C) % wide
        ks_ref[j] = jnp.where(mine, pltpu.roll(r0, shift, axis=1), 0.0).astype(ks_ref.dtype)

    e = (S5_L - 1 - lax.broadcasted_iota(jnp.int32, (S5_L, S5_ST), 0)).astype(F32)
    mg = jnp.exp(e * lr)
    rep = (lax.broadcasted_iota(jnp.int32, (wide, S5_L), 0) // S5_C
           == lax.broadcasted_iota(jnp.int32, (wide, S5_L), 1)).astype(BF16)

    def on_rows(a):
        hi = a.astype(BF16)
        return _dot(rep, hi) + _dot(rep, (a - hi.astype(F32)).astype(BF16))

    fr, fi = on_rows(mg * jnp.cos(e * li)), on_rows(mg * jnp.sin(e * li))
    btr = zr * btile_ref[0] - zi * btile_ref[1]
    bti = zr * btile_ref[1] + zi * btile_ref[0]
    re_half = lax.broadcasted_iota(jnp.int32, (wide, S5_ST), 1) < S5_P
    g128 = jnp.where(re_half, fr * btr - fi * bti, fr * bti + fi * btr)
    tile = lax.broadcasted_iota(jnp.int32, (wide, S5_SW), 1) // S5_ST
    gfull = jnp.where(tile == g, jnp.concatenate([g128] * S5_G, axis=1), 0.0)
    gs_ref[...] = gfull.reshape(S5_L, S5_C, S5_SW).astype(gs_ref.dtype)

    pr, pi = on_lags(pw_r, 1), on_lags(pw_i, 1)
    hcat = jnp.concatenate([cr * pr - ci * pi, -(cr * pi + ci * pr)], axis=0)
    mine_h = lax.broadcasted_iota(jnp.int32, (S5_ST, wide), 1) // S5_C == g
    for t in range(S5_L):
        shift = (g * S5_C + wide - t * S5_C) % wide
        hs_ref[t] = jnp.where(mine_h, pltpu.roll(hcat, shift, axis=1), 0.0).astype(hs_ref.dtype)

    mg = jnp.exp(float(S5_L) * lr)
    d_ref[0:1, :] = mg * jnp.cos(float(S5_L) * li)
    d_ref[1:2, :] = mg * jnp.sin(float(S5_L) * li)


def _s5_params(prow, bt, btile, ctile):
    g = prow.shape[0]
    wide = S5_L * S5_C
    m3 = lambda i: (i, 0, 0)
    m4 = lambda i: (i, 0, 0, 0)
    g4 = lambda i: (0, i, 0, 0)
    return pl.pallas_call(
        _s5_param_kernel,
        grid=(g,),
        in_specs=[pl.BlockSpec((None, 3, S5_ST), m3),
                  pl.BlockSpec((None, 2, S5_C, S5_P), m4), pl.BlockSpec((None, 2, wide, S5_ST), m4),
                  pl.BlockSpec((None, 2, S5_P, wide), m4)],
        out_specs=[pl.BlockSpec((S5_L, None, S5_C, wide), g4), pl.BlockSpec((S5_L, None, S5_C, S5_SW), g4),
                   pl.BlockSpec((S5_L, None, S5_ST, wide), g4), pl.BlockSpec((None, 2, S5_ST), m3)],
        out_shape=[jax.ShapeDtypeStruct((S5_L, g, S5_C, wide), BF16), jax.ShapeDtypeStruct((S5_L, g, S5_C, S5_SW), BF16),
                   jax.ShapeDtypeStruct((S5_L, g, S5_ST, wide), BF16), jax.ShapeDtypeStruct((g, 2, S5_ST), F32)],
        compiler_params=_cparams(("parallel",)),
        name="s5_params",
    )(prow, bt, btile, ctile)


def _s5_state_kernel(ucat_ref, gs_ref, d_ref, xs_ref, *, nchunk):
    width = xs_ref.shape[1]
    x = _dot(ucat_ref[...], gs_ref[...])
    sgn = jnp.where(lax.broadcasted_iota(jnp.int32, (1, width), 1) % S5_ST < S5_P, -1.0, 1.0)
    rn = lax.broadcasted_iota(jnp.int32, (nchunk, width), 0)
    pr, pi = d_ref[0:1, :], d_ref[1:2, :]

    def swap(a):
        return jnp.concatenate([pltpu.roll(a[:, k * S5_ST:(k + 1) * S5_ST], S5_P, axis=1)
                                for k in range(width // S5_ST)], axis=1)

    sh = 1
    while sh < nchunk:
        sx = jnp.where(rn >= sh, pltpu.roll(x, sh, axis=0), 0.0)
        x = x + pr * sx + (pi * sgn) * swap(sx)
        pr, pi = pr * pr - pi * pi, 2.0 * pr * pi
        sh *= 2
    xs_ref[...] = jnp.where(rn >= 1, pltpu.roll(x, 1, axis=0), 0.0).astype(xs_ref.dtype)


def _s5_state(ucat, gs, d, batch, seq):
    nchunk = seq // S5_L
    ntile = S5_SW // S5_W
    return pl.pallas_call(
        functools.partial(_s5_state_kernel, nchunk=nchunk),
        grid=(batch, ntile),
        in_specs=[pl.BlockSpec((nchunk, S5_CW), lambda b, j: (b, 0)),
                  pl.BlockSpec((S5_CW, S5_W), lambda b, j: (0, j)),
                  pl.BlockSpec((2, S5_W), lambda b, j: (0, j))],
        out_specs=pl.BlockSpec((nchunk, S5_W), lambda b, j: (b, j)),
        out_shape=jax.ShapeDtypeStruct((batch * nchunk, S5_SW), BF16),
        compiler_params=_cparams(("parallel", "arbitrary")),
        name="s5_state",
    )(ucat, gs, d)


def _s5_out_kernel(ucat_s, u0_ref, u1_ref, g0_ref, g1_ref, xs_ref, ks_ref, hs_ref, dsk_ref, w_ref, b_ref,
                   o0_ref, o1_ref, *, nb):
    step = pl.program_id(1)

    def response(tt, k):
        return (_dot(ucat_s[:, 0:(tt + 1) * S5_W], ks_ref[(S5_L - 1 - tt) * S5_W:, :])
                + _dot(xs_ref[...], hs_ref[k]))

    def finish(tt, y):
        rows = pl.ds(tt, nb, stride=S5_L)
        cat = lambda a, b: jnp.concatenate([a[rows, :], b[rows, :]], axis=1)
        y = y + dsk_ref[...] * cat(u0_ref, u1_ref)
        y = y * (0.5 * (1.0 + jnp.tanh(math.sqrt(2.0 / math.pi) * (y + 0.044715 * (y * y * y)))))
        z = _dot(y.astype(BF16), w_ref[...]) + b_ref[...]
        out = y * _sigmoid(z) * _silu(cat(g0_ref, g1_ref))
        o0_ref[rows, :] = out[:, 0:S5_HW]
        o1_ref[rows, :] = out[:, S5_HW:]

    for sg in range(S5_L // S5_TG):
        @pl.when(step == sg)
        def _():
            ys = [response(sg * S5_TG + k, k) for k in range(S5_TG)]
            for k in range(S5_TG):
                finish(sg * S5_TG + k, ys[k])


def _s5_out(ucat, sproj, xs, ks, hs, d_skip, w_bf, b, rb):
    t = sproj.shape[0]
    nb = rb // S5_L
    cst = lambda r, i: (0, 0)
    return pl.pallas_call(
        functools.partial(_s5_out_kernel, nb=nb),
        grid=(t // rb, S5_L // S5_TG),
        in_specs=[pl.BlockSpec((nb, S5_CW), lambda r, i: (r, 0)),
                  pl.BlockSpec((rb, S5_HW), lambda r, i: (r, 0)), pl.BlockSpec((rb, S5_HW), lambda r, i: (r, 1)),
                  pl.BlockSpec((rb, S5_HW), lambda r, i: (r, 2)), pl.BlockSpec((rb, S5_HW), lambda r, i: (r, 3)),
                  pl.BlockSpec((nb, S5_SW), lambda r, i: (r, 0)),
                  pl.BlockSpec((S5_CW, S5_W), cst),
                  pl.BlockSpec((S5_TG, S5_SW, S5_W), lambda r, i: (i, 0, 0)),
                  pl.BlockSpec((1, S5_W), cst), pl.BlockSpec((S5_W, S5_W), cst), pl.BlockSpec((1, S5_W), cst)],
        out_specs=[pl.BlockSpec((rb, S5_HW), lambda r, i: (r, 0))] * 2,
        out_shape=[jax.ShapeDtypeStruct((t, S5_HW), F32)] * 2,
        compiler_params=_cparams(("parallel", "arbitrary")),
        name="s5_out",
    )(ucat, sproj, sproj, sproj, sproj, xs, ks, hs, d_skip, w_bf, b)


def _s5(ucat, sproj, a_re, a_im, b_re, b_im, c_re, c_im, log_dt, d_skip, glu_w, glu_b, batch, seq, rb):
    ldt = jnp.broadcast_to(log_dt[:, None], a_re.shape)
    twice = lambda a: jnp.concatenate([a, a], axis=-1)
    prow = twice(jnp.stack([a_re, a_im, ldt], axis=1))
    bt = jnp.stack([b_re, b_im], axis=1).transpose(0, 1, 3, 2)
    btile = twice(jnp.tile(bt, (1, 1, S5_L, 1)))
    ctile = jnp.tile(jnp.stack([c_re, c_im], axis=1).transpose(0, 1, 3, 2), (1, 1, 1, S5_L))
    ks, gs, hs, d = _s5_params(prow, bt, btile, ctile)
    ks = ks.reshape(S5_CW, S5_W)
    gs = gs.reshape(S5_CW, S5_SW)
    hs = hs.reshape(S5_L, S5_SW, S5_W)
    d = d.transpose(1, 0, 2).reshape(2, S5_SW)
    xs = _s5_state(ucat, gs, d, batch, seq)
    return _s5_out(ucat, sproj, xs, ks, hs, d_skip[None, :], glu_w.astype(BF16), glu_b[None, :], rb)


def _pick(n, pref):
    b = min(n, pref)
    while n % b:
        b //= 2
    return b


def kernel(x, norm_w, w_in, w_out, hgrn_lb_logits, hgrn_norm_w, s5_a_re, s5_a_im, s5_b_re, s5_b_im, s5_c_re, s5_c_im, s5_log_dt, s5_d, s5_glu_w, s5_glu_b, diff_lq1, diff_lk1, diff_lq2, diff_lk2, diff_subln_w, final_norm_w):
    batch, seq, _ = x.shape
    depth = norm_w.shape[0]
    t = batch * seq
    tm = _pick(seq, 512)
    tq = _pick(seq, ATT_TQ)
    ts = _pick(seq, 1024)
    rope = _rope_tables(seq)
    h_res = x.astype(F32).reshape(t, D_MODEL)
    nw = lambda l: norm_w[l][None, :].astype(F32)
    hp, sp, uc, aq, ak, av, ag = _inproj(h_res, nw(0), w_in[0].astype(BF16), rope, seq, tm)
    for l in range(depth):
        mix_h = _hgrn(hp, hgrn_lb_logits.astype(F32), jnp.tile(hgrn_norm_w[l].astype(F32), HGRN_W // HGRN_HD)[None, :],
                      l, batch, seq, ts)
        mix_s = _s5(uc, sp, s5_a_re[l], s5_a_im[l], s5_b_re[l], s5_b_im[l], s5_c_re[l], s5_c_im[l], s5_log_dt[l],
                    s5_d[l], s5_glu_w[l], s5_glu_b[l], batch, seq, _pick(seq, 4096))
        lam_init = 0.8 - 0.6 * math.exp(-0.3 * l)
        mix_a = _attention(aq, ak, av, ag, diff_lq1[l][None, :], diff_lk1[l][None, :], diff_lq2[l][None, :],
                           diff_lk2[l][None, :], diff_subln_w[l][None, :], lam_init, batch, seq, tq)
        mixes = (mix_h, mix_s[0], mix_s[1], mix_a, w_out[l].astype(BF16))
        if l + 1 < depth:
            h_res, hp, sp, uc, aq, ak, av, ag = _outin(h_res, *mixes, nw(l + 1), w_in[l + 1].astype(BF16), rope, seq, tm)
        else:
            h_res = _outproj_final(h_res, *mixes, final_norm_w[None, :].astype(F32), _pick(seq, 1024))
    return h_res.reshape(batch, seq, D_MODEL).astype(x.dtype)
```

```python
import functools
import math

import jax
import jax.numpy as jnp
from jax import lax
from jax.experimental import pallas as pl
from jax.experimental.pallas import tpu as pltpu

F32 = jnp.float32
BF16 = jnp.bfloat16

D_MODEL = 1024
HGRN_W = 256
HGRN_HD = 64
CHUNK = 16
HGRN_WR = 8 * 16 + 8 * 8
HGRN_PW = 2 * HGRN_HD
HGRN_GR = 128
LB_FLOOR = 1e-30
S5_W = 256
S5_C = 16
S5_G = S5_W // S5_C
S5_P = 64
S5_L = 16
S5_ST = 2 * S5_P
S5_SW = S5_G * S5_ST
S5_CW = S5_L * S5_W
S5_TG = 8
S5_HW = 128
ATT_W = 512
ATT_DH = 64
ATT_DV = 128
ATT_H = ATT_W // ATT_DV
ATT_CT = 256
ATT_VA = ATT_DV + 16
ATT_TQ = 2048
ATT_TK = 512
ROPE_DIM = 16
ROPE_THETA = 500000.0
MASK_VALUE = -1e30
Q_SCALE = ATT_DH ** -0.5 * math.log2(math.e)
EPS = 1e-6
D_IN = 4 * HGRN_W + 2 * S5_W + 4 * ATT_W
COL_S5 = 4 * HGRN_W
COL_AQ = COL_S5 + 2 * S5_W
COL_AK = COL_AQ + ATT_W
COL_AV = COL_AK + ATT_W
COL_AG = COL_AV + ATT_W

VMEM_LIMIT = 56 * 1024 * 1024


def _cparams(sem):
    return pltpu.CompilerParams(dimension_semantics=sem, vmem_limit_bytes=VMEM_LIMIT)


def _sigmoid(x):
    return 1.0 / (1.0 + jnp.exp(-x))


def _softplus_neg_abs(x):
    return jnp.log(1.0 + jnp.exp(-jnp.abs(x)))


def _silu(x):
    return x * _sigmoid(x)


def _dot(a, b):
    return jnp.dot(a, b, preferred_element_type=F32)


def _dot_split(a, b):
    ah = a.astype(BF16)
    al = (a - ah.astype(F32)).astype(BF16)
    bh = b.astype(BF16)
    bl = (b - bh.astype(F32)).astype(BF16)
    return _dot(ah, bh) + _dot(ah, bl) + _dot(al, bh)


def _rope(t, rc, ra, rb):
    return t * rc + pltpu.roll(t, 128 - ROPE_DIM // 2, axis=1) * ra + pltpu.roll(t, ROPE_DIM // 2, axis=1) * rb


def _project(x, nw_ref, w_ref, rope_ref, h_ref, s_ref, uc_ref, q_ref, k_ref, v_ref, g_ref, us_s):
    hn = x * lax.rsqrt(jnp.mean(x * x, axis=-1, keepdims=True) + EPS) * nw_ref[...]
    proj = _dot(hn.astype(BF16), w_ref[...])
    h_ref[...] = proj[:, :COL_S5]
    s_ref[...] = proj[:, COL_S5:COL_AQ]
    nchunk = uc_ref.shape[0]
    for k in range(S5_W // S5_HW):
        us_s[k] = proj[:, COL_S5 + k * S5_HW:COL_S5 + (k + 1) * S5_HW]
    for tau in range(S5_L):
        for k in range(S5_W // S5_HW):
            lo = tau * S5_W + k * S5_HW
            uc_ref[:, lo:lo + S5_HW] = us_s[k, pl.ds(tau, nchunk, stride=S5_L), :].astype(BF16)
    rc = rope_ref[:, 0:128]
    ra = rope_ref[:, 128:256]
    rb = rope_ref[:, 256:384]
    for h in range(ATT_H):
        lo = h * ATT_DV
        q = proj[:, COL_AQ + lo:COL_AQ + lo + ATT_DV]
        k = proj[:, COL_AK + lo:COL_AK + lo + ATT_DV]
        q_ref[:, lo:lo + ATT_DV] = (_rope(q, rc, ra, rb) * Q_SCALE).astype(BF16)
        k_ref[:, lo:lo + ATT_DV] = _rope(k, rc, ra, rb).astype(BF16)
    v_ref[...] = proj[:, COL_AV:COL_AG].astype(BF16)
    g_ref[...] = proj[:, COL_AG:]


def _inproj_kernel(x_ref, nw_ref, w_ref, rope_ref, *outs_and_scratch):
    _project(x_ref[...], nw_ref, w_ref, rope_ref, *outs_and_scratch)


def _rope_tables(seq):
    pos = jnp.arange(seq, dtype=F32)
    inv_freq = ROPE_THETA ** (-jnp.arange(0, ROPE_DIM, 2, dtype=F32) / ROPE_DIM)
    ang = pos[:, None] * inv_freq[None, :]
    cos, sin = jnp.cos(ang), jnp.sin(ang)
    half = ROPE_DIM // 2
    zeros = jnp.zeros((seq, ATT_DH - ROPE_DIM), F32)
    zh = jnp.zeros((seq, half), F32)
    rc = jnp.concatenate([cos, cos, jnp.ones((seq, ATT_DH - ROPE_DIM), F32)], axis=1)
    ra = jnp.concatenate([-sin, zh, zeros], axis=1)
    rb = jnp.concatenate([zh, sin, zeros], axis=1)
    return jnp.concatenate([jnp.tile(t, (1, 2)) for t in (rc, ra, rb)], axis=1)


def _inproj(x2, norm_w, w_bf, rope, seq, tm):
    t = x2.shape[0]
    return pl.pallas_call(
        _inproj_kernel,
        grid=(t // tm,),
        in_specs=_project_in_specs(seq, tm),
        out_specs=_project_out_specs(tm),
        out_shape=_project_out_shapes(t),
        scratch_shapes=[pltpu.VMEM((S5_W // S5_HW, tm, S5_HW), F32)],
        compiler_params=_cparams(("parallel",)),
        name="inproj",
    )(x2, norm_w, w_bf, rope)


def _project_in_specs(seq, tm):
    nblk = seq // tm
    return [pl.BlockSpec((tm, D_MODEL), lambda i: (i, 0)),
            pl.BlockSpec((1, D_MODEL), lambda i: (0, 0)),
            pl.BlockSpec((D_MODEL, D_IN), lambda i: (0, 0)),
            pl.BlockSpec((tm, 384), lambda i: (i % nblk, 0))]


def _project_out_specs(tm):
    row = lambda i: (i, 0)
    return [pl.BlockSpec((tm, COL_S5), row), pl.BlockSpec((tm, 2 * S5_W), row),
            pl.BlockSpec((tm // S5_L, S5_CW), row),
            pl.BlockSpec((tm, ATT_W), row), pl.BlockSpec((tm, ATT_W), row),
            pl.BlockSpec((tm, ATT_W), row), pl.BlockSpec((tm, ATT_W), row)]


def _project_out_shapes(t):
    return [jax.ShapeDtypeStruct((t, COL_S5), F32), jax.ShapeDtypeStruct((t, 2 * S5_W), F32),
            jax.ShapeDtypeStruct((t // S5_L, S5_CW), BF16),
            jax.ShapeDtypeStruct((t, ATT_W), BF16), jax.ShapeDtypeStruct((t, ATT_W), BF16),
            jax.ShapeDtypeStruct((t, ATT_W), BF16), jax.ShapeDtypeStruct((t, ATT_W), F32)]


def _residual(res_ref, mh_ref, ms0_ref, ms1_ref, ma_ref, w_ref):
    acc = res_ref[...]
    acc = acc + _dot(mh_ref[...].astype(BF16), w_ref[0:HGRN_W, :])
    acc = acc + _dot(ms0_ref[...].astype(BF16), w_ref[HGRN_W:HGRN_W + S5_HW, :])
    acc = acc + _dot(ms1_ref[...].astype(BF16), w_ref[HGRN_W + S5_HW:HGRN_W + S5_W, :])
    return acc + _dot(ma_ref[...].astype(BF16), w_ref[HGRN_W + S5_W:, :])


def _mix_in_specs(tm):
    row = lambda i: (i, 0)
    return [pl.BlockSpec((tm, D_MODEL), row), pl.BlockSpec((tm, HGRN_W), row),
            pl.BlockSpec((tm, S5_HW), row), pl.BlockSpec((tm, S5_HW), row), pl.BlockSpec((tm, ATT_W), row),
            pl.BlockSpec((D_MODEL, D_MODEL), lambda i: (0, 0))]


def _outproj_final_kernel(res_ref, mh_ref, ms0_ref, ms1_ref, ma_ref, w_ref, fw_ref, o_ref):
    acc = _residual(res_ref, mh_ref, ms0_ref, ms1_ref, ma_ref, w_ref)
    o_ref[...] = acc * lax.rsqrt(jnp.mean(acc * acc, axis=-1, keepdims=True) + EPS) * fw_ref[...]


def _outproj_final(res, mh, ms0, ms1, ma, w_bf, fw, tm):
    t = res.shape[0]
    return pl.pallas_call(
        _outproj_final_kernel,
        grid=(t // tm,),
        in_specs=_mix_in_specs(tm) + [pl.BlockSpec((1, D_MODEL), lambda i: (0, 0))],
        out_specs=pl.BlockSpec((tm, D_MODEL), lambda i: (i, 0)),
        out_shape=jax.ShapeDtypeStruct((t, D_MODEL), F32),
        compiler_params=_cparams(("parallel",)),
        name="outproj",
    )(res, mh, ms0, ms1, ma, w_bf, fw)


def _outin_kernel(res_ref, mh_ref, ms0_ref, ms1_ref, ma_ref, wo_ref, nw_ref, wi_ref, rope_ref, o_ref,
                  *outs_and_scratch):
    acc = _residual(res_ref, mh_ref, ms0_ref, ms1_ref, ma_ref, wo_ref)
    o_ref[...] = acc
    _project(acc, nw_ref, wi_ref, rope_ref, *outs_and_scratch)


def _outin(res, mh, ms0, ms1, ma, wo_bf, norm_w, wi_bf, rope, seq, tm):
    t = res.shape[0]
    return pl.pallas_call(
        _outin_kernel,
        grid=(t // tm,),
        in_specs=_mix_in_specs(tm) + _project_in_specs(seq, tm)[1:],
        out_specs=[pl.BlockSpec((tm, D_MODEL), lambda i: (i, 0))] + _project_out_specs(tm),
        out_shape=[jax.ShapeDtypeStruct((t, D_MODEL), F32)] + _project_out_shapes(t),
        scratch_shapes=[pltpu.VMEM((S5_W // S5_HW, tm, S5_HW), F32)],
        compiler_params=_cparams(("parallel",)),
        name="outproj_inproj",
    )(res, mh, ms0, ms1, ma, wo_bf, norm_w, wi_bf, rope)


def _attn_kernel(q_ref, k_ref, v_ref, g_ref, lq1_ref, lk1_ref, lq2_ref, lk2_ref, sw_ref, o_ref,
                 vt_s, qq_s, acc_s, s0_s, s1_s, *, lam_init, tq, tk):
    i = pl.program_id(2)
    nkb = vt_s.shape[0]
    ntile = 2 * tq // ATT_CT
    per_map = tq // ATT_CT
    nt = (((1,), (1,)), ((), ()))

    @pl.when(i == 0)
    def _():
        for j in range(nkb):
            vt_s[j, 0:ATT_DV, :] = v_ref[j * tk:(j + 1) * tk, :].T
            vt_s[j, ATT_DV:, :] = jnp.ones((ATT_VA - ATT_DV, tk), BF16)

    q = q_ref[...]
    lane = lax.broadcasted_iota(jnp.int32, (tq, ATT_DV), 1)
    qq_s[0:tq, :] = jnp.where(lane < ATT_DH, q, jnp.zeros_like(q))
    qq_s[tq:2 * tq, :] = jnp.where(lane >= ATT_DH, q, jnp.zeros_like(q))
    acc_s[...] = jnp.zeros(acc_s.shape, F32)

    def tile_mode(c, koff):
        qs = (c % per_map) * ATT_CT
        if koff is None or qs >= koff + tk - 1:
            return "full"
        return "skip" if qs + ATT_CT - 1 < koff else "causal"

    def live_keys(c, koff):
        if tile_mode(c, koff) != "causal":
            return tk
        return min(tk, (c % per_map) * ATT_CT + ATT_CT - koff)

    def stage(kq, sq_s, koff_q, ssm_s, ksm, koff_sm, m_run, cmax):
        new_m, new_cmax = list(m_run), [None] * ntile
        if kq is not None:
            kb = k_ref[pl.ds(kq * tk if isinstance(kq, int) else pl.multiple_of(kq * tk, tk), tk), :]
        if ksm is not None:
            vtb = vt_s[ksm]
        for c in range(ntile):
            cols = slice(c * ATT_CT, (c + 1) * ATT_CT)
            if kq is not None and tile_mode(c, koff_q) != "skip":
                nk = live_keys(c, koff_q)
                s = lax.dot_general(kb[0:nk], qq_s[cols, :], nt, preferred_element_type=F32)
                if tile_mode(c, koff_q) == "causal":
                    kpos = lax.broadcasted_iota(jnp.int32, (nk, ATT_CT), 0) + koff_q
                    qpos = lax.broadcasted_iota(jnp.int32, (nk, ATT_CT), 1) + (c % per_map) * ATT_CT
                    s = jnp.where(kpos <= qpos, s, MASK_VALUE)
                sq_s[0:nk, cols] = s
                new_cmax[c] = jnp.max(s, axis=0, keepdims=True)
            if ksm is not None and tile_mode(c, koff_sm) != "skip":
                nk = live_keys(c, koff_sm)
                m_new = jnp.maximum(m_run[c], cmax[c])
                alpha = jnp.exp2(m_run[c] - m_new)
                p = jnp.exp2(ssm_s[0:nk, cols] - m_new)
                acc_s[:, cols] = alpha * acc_s[:, cols] + _dot(vtb[:, 0:nk], p.astype(BF16))
                new_m[c] = m_new
        return tuple(new_m), tuple(new_cmax)

    neg = tuple(jnp.full((1, ATT_CT), MASK_VALUE, F32) for _ in range(ntile))
    ndiag = tq // tk
    nfull = ndiag * i
    bufs = (s0_s, s1_s)
    fill = lambda cmx: tuple(neg[c] if x is None else x for c, x in enumerate(cmx))
    _, cm = stage(nfull, bufs[0], 0, None, None, None, neg, None)
    m_run = neg
    for m in range(1, ndiag):
        m_run, cm = stage(nfull + m, bufs[m % 2], m * tk, bufs[(m - 1) % 2], nfull + m - 1, (m - 1) * tk,
                          m_run, fill(cm))
    cm = fill(cm)
    last = ndiag - 1

    def pair(t, carry):
        m_run, cm0 = carry
        m_run, cm1 = stage(2 * t + 1, s1_s, None, s0_s, 2 * t, None, m_run, cm0)
        return stage(2 * t + 2, s0_s, None, s1_s, 2 * t + 1, None, m_run, cm1)

    @pl.when(i == 0)
    def _():
        stage(None, None, None, s1_s, nfull + last, last * tk, m_run, cm)

    @pl.when(i > 0)
    def _():
        m2, cm0 = stage(0, s0_s, None, s1_s, nfull + last, last * tk, m_run, cm)
        m2, cm0 = lax.fori_loop(0, nfull // 2 - 1, pair, (m2, cm0))
        m2, cml = stage(nfull - 1, s1_s, None, s0_s, nfull - 2, None, m2, cm0)
        stage(None, None, None, s1_s, nfull - 1, None, m2, cml)

    lam = (jnp.exp(jnp.sum(lq1_ref[...] * lk1_ref[...], axis=-1, keepdims=True))
           - jnp.exp(jnp.sum(lq2_ref[...] * lk2_ref[...], axis=-1, keepdims=True)) + lam_init)
    on = acc_s[0:ATT_DV, :] / acc_s[ATT_DV:ATT_DV + 1, :]
    ot = on[:, 0:tq] - lam * on[:, tq:2 * tq]
    o = ot.T
    o = o * lax.rsqrt(jnp.mean(o * o, axis=-1, keepdims=True) + EPS) * sw_ref[...] * (1.0 - lam_init)
    o_ref[...] = (o * _silu(g_ref[...])).astype(o_ref.dtype)


def _attention(q, k, v, g, lq1, lk1, lq2, lk2, sw, lam_init, batch, seq, tq):
    t = q.shape[0]
    tk = min(ATT_TK, tq // 2)
    nq = seq // tq
    qmap = lambda b, h, i: (b * nq + i, h)
    kvmap = lambda b, h, i: (b, h)
    cmap = lambda b, h, i: (0, 0)
    return pl.pallas_call(
        functools.partial(_attn_kernel, lam_init=lam_init, tq=tq, tk=tk),
        grid=(batch, ATT_H, nq),
        in_specs=[pl.BlockSpec((tq, ATT_DV), qmap), pl.BlockSpec((seq, ATT_DV), kvmap),
                  pl.BlockSpec((seq, ATT_DV), kvmap), pl.BlockSpec((tq, ATT_DV), qmap),
                  pl.BlockSpec((1, ATT_DH), cmap), pl.BlockSpec((1, ATT_DH), cmap),
                  pl.BlockSpec((1, ATT_DH), cmap), pl.BlockSpec((1, ATT_DH), cmap),
                  pl.BlockSpec((1, ATT_DV), cmap)],
        out_specs=pl.BlockSpec((tq, ATT_DV), qmap),
        out_shape=jax.ShapeDtypeStruct((t, ATT_W), BF16),
        scratch_shapes=[pltpu.VMEM((seq // tk, ATT_VA, tk), BF16), pltpu.VMEM((2 * tq, ATT_DV), BF16),
                        pltpu.VMEM((ATT_VA, 2 * tq), F32),
                        pltpu.VMEM((tk, 2 * tq), F32), pltpu.VMEM((tk, 2 * tq), F32)],
        compiler_params=_cparams(("parallel", "parallel", "arbitrary")),
        name="diffattn",
    )(q, k, v, g, lq1, lk1, lq2, lk2, sw)


def _head_ones(dtype):
    r = lax.broadcasted_iota(jnp.int32, (HGRN_W, HGRN_W), 0) // HGRN_HD
    c = lax.broadcasted_iota(jnp.int32, (HGRN_W, HGRN_W), 1) // HGRN_HD
    return (r == c).astype(dtype)


def _hgrn_kernel(h_ref, lbl_ref, nw_ref, o_ref, st_ref, q_s, v_s, c_s, cs_s, oo_s, dect_s, qt_s, vb_s, kt_s,
                 w_s, sc_s, *, layer, ts):
    nchunk = ts // CHUNK
    ngroup = ts // HGRN_GR
    npair = HGRN_W // HGRN_PW
    half = CHUNK // 2

    @pl.when(pl.program_id(1) == 0)
    def _():
        st_ref[...] = jnp.zeros(st_ref.shape, F32)

    lg = lbl_ref[...]
    e = jnp.exp(lg - jnp.max(lg, axis=0, keepdims=True))
    p = e / jnp.sum(e, axis=0, keepdims=True)
    lb = jnp.sum(p[0:layer + 1, :], axis=0, keepdims=True) - p[0:1, :]
    log2e = math.log2(math.e)
    log_lb = jnp.log(jnp.maximum(lb, LB_FLOOR)) * log2e
    log1m_lb = jnp.log1p(-lb) * log2e
    r16 = lax.broadcasted_iota(jnp.int32, (HGRN_GR, HGRN_W), 0) % CHUNK
    cpg = HGRN_GR // CHUNK
    by_chunk = lambda a: a.reshape(cpg, CHUNK, HGRN_W)

    def prepare(g, carry):
        rows = pl.ds(pl.multiple_of(g * HGRN_GR, HGRN_GR), HGRN_GR)
        x2 = h_ref[rows, HGRN_W:2 * HGRN_W] * log2e
        hi = h_ref[rows, 2 * HGRN_W:3 * HGRN_W]
        b = log1m_lb - (jnp.maximum(-x2, 0.0) + jnp.log2(1.0 + jnp.exp2(-jnp.abs(x2))))
        log_f = jnp.maximum(log_lb, b) + jnp.log2(1.0 + jnp.exp2(-jnp.abs(log_lb - b)))
        k = 1.0 - jnp.exp2(log_f)
        q = _silu(h_ref[rows, 0:HGRN_W])
        cum = log_f
        sh = 1
        while sh < CHUNK:
            cum = cum + jnp.where(r16 >= sh, pltpu.roll(cum, sh, axis=0), 0.0)
            sh *= 2
        cum = by_chunk(cum)
        last = jnp.broadcast_to(cum[:, CHUNK - 1:CHUNK, :], cum.shape)
        rem = last - cum
        k, q = by_chunk(k), by_chunk(q)
        chunks = pl.ds(pl.multiple_of(g * cpg, cpg), cpg)
        q_s[chunks] = q
        v_s[chunks] = by_chunk(hi)
        c_s[chunks] = cum
        cs_s[chunks] = cum - jnp.log2(k)
        qt_s[chunks] = (q * jnp.exp2(cum)).astype(BF16)
        vb_s[g] = hi.astype(BF16)
        dect_s[g] = jnp.exp2(last).reshape(HGRN_GR, HGRN_W).T
        kt = (k * jnp.exp2(rem)).reshape(HGRN_GR, HGRN_W).T.astype(BF16)
        for pr in range(npair):
            kt_s[pr, g] = kt[pr * HGRN_PW:(pr + 1) * HGRN_PW, :]
        return carry

    lax.fori_loop(0, ngroup, prepare, 0, unroll=2)

    t8 = lax.broadcasted_iota(jnp.int32, (half, HGRN_W), 0)

    def weights(n, carry):
        c0, c1 = c_s[n, 0:half, :], c_s[n, half:, :]
        q0, q1 = q_s[n, 0:half, :], q_s[n, half:, :]
        tiles = []
        for s in range(CHUNK):
            bs = jnp.broadcast_to(cs_s[n, s:s + 1, :], (half, HGRN_W))
            if s < half:
                d0 = jnp.exp2(c0 - bs)
                tiles.append(q0 * (jnp.where(t8 >= s, d0, 0.0) if s else d0))
                tiles.append(q1 * jnp.exp2(c1 - bs))
            else:
                d1 = jnp.exp2(c1 - bs)
                tiles.append(q1 * (jnp.where(t8 >= s - half, d1, 0.0) if s > half else d1))
        w_s[n] = jnp.concatenate(tiles, axis=0).astype(BF16)
        return carry

    lax.fori_loop(0, nchunk, weights, 0)
    sc_s[...] = _dot(w_s[...].reshape(nchunk * HGRN_WR, HGRN_W), _head_ones(BF16)).reshape(nchunk, HGRN_WR, HGRN_W)

    lane_chunk = lax.broadcasted_iota(jnp.int32, (HGRN_PW, HGRN_GR), 1) // CHUNK
    same_head = (lax.broadcasted_iota(jnp.int32, (HGRN_PW, HGRN_PW), 0) // HGRN_HD
                 == lax.broadcasted_iota(jnp.int32, (HGRN_PW, HGRN_PW), 1) // HGRN_HD)

    def group(g, carry):
        state = list(carry)
        pcols = [slice(pr * HGRN_PW, (pr + 1) * HGRN_PW) for pr in range(npair)]
        upd = [[None] * npair for _ in range(cpg)]
        for pr in range(npair):
            kt = kt_s[pr, g]
            lhs = jnp.concatenate([jnp.where(lane_chunk == j, kt, jnp.zeros_like(kt)) for j in range(cpg)], axis=0)
            inc = _dot(lhs, vb_s[g, :, pcols[pr]])
            for j in range(cpg):
                upd[j][pr] = jnp.where(same_head, inc[j * HGRN_PW:(j + 1) * HGRN_PW], 0.0)
        entering = []
        for j in range(cpg):
            entering.append([st.astype(BF16) for st in state])
            for pr in range(npair):
                dec = dect_s[g, pr * HGRN_PW:(pr + 1) * HGRN_PW, j * CHUNK:j * CHUNK + 1]
                state[pr] = state[pr] * jnp.broadcast_to(dec, (HGRN_PW, HGRN_PW)) + upd[j][pr]
        for j in range(cpg):
            n = g * cpg + j
            o0 = jnp.zeros((half, HGRN_W), F32)
            o1 = jnp.zeros((half, HGRN_W), F32)
            for s in range(CHUNK):
                vb = jnp.broadcast_to(v_s[n, s:s + 1, :], (half, HGRN_W))
                if s < half:
                    o0 = o0 + sc_s[n, s * CHUNK:s * CHUNK + half, :] * vb
                    o1 = o1 + sc_s[n, s * CHUNK + half:(s + 1) * CHUNK, :] * vb
                else:
                    o1 = o1 + sc_s[n, half * CHUNK + (s - half) * half:half * CHUNK + (s - half + 1) * half, :] * vb
            oi = jnp.concatenate([_dot(qt_s[n, :, pcols[pr]], entering[j][pr]) for pr in range(npair)], axis=1)
            oo_s[n, 0:half, :] = o0 + oi[0:half]
            oo_s[n, half:, :] = o1 + oi[half:]
        return tuple(state)

    state = lax.fori_loop(0, ngroup, group, tuple(st_ref[pr] for pr in range(npair)), unroll=2)
    for pr in range(npair):
        st_ref[pr] = state[pr]

    def finish(g, carry):
        rows = pl.ds(pl.multiple_of(g * HGRN_GR, HGRN_GR), HGRN_GR)
        o = oo_s[pl.ds(pl.multiple_of(g * cpg, cpg), cpg)].reshape(HGRN_GR, HGRN_W)
        sq = o * o
        sq_hi = sq.astype(BF16)
        ones = _head_ones(BF16)
        ms = (_dot(sq_hi, ones) + _dot((sq - sq_hi.astype(F32)).astype(BF16), ones)) * (1.0 / HGRN_HD)
        hg = h_ref[rows, 3 * HGRN_W:4 * HGRN_W]
        o_ref[rows, :] = (o * lax.rsqrt(ms + EPS) * nw_ref[...] * _silu(hg)).astype(o_ref.dtype)
        return carry

    lax.fori_loop(0, ngroup, finish, 0, unroll=2)


def _hgrn(hproj, lb_logits, nw_tiled, layer, batch, seq, ts):
    t = hproj.shape[0]
    ns = seq // ts
    depth = lb_logits.shape[0]
    nchunk = ts // CHUNK
    npair = HGRN_W // HGRN_PW
    blk = pltpu.VMEM((nchunk, CHUNK, HGRN_W), F32)
    return pl.pallas_call(
        functools.partial(_hgrn_kernel, layer=layer, ts=ts),
        grid=(batch, ns),
        in_specs=[pl.BlockSpec((ts, 4 * HGRN_W), lambda b, i: (b * ns + i, 0)),
                  pl.BlockSpec((depth, HGRN_W), lambda b, i: (0, 0)),
                  pl.BlockSpec((1, HGRN_W), lambda b, i: (0, 0))],
        out_specs=pl.BlockSpec((ts, HGRN_W), lambda b, i: (b * ns + i, 0)),
        out_shape=jax.ShapeDtypeStruct((t, HGRN_W), BF16),
        scratch_shapes=[pltpu.VMEM((npair, HGRN_PW, HGRN_PW), F32),
                        blk, blk, blk, blk, blk,
                        pltpu.VMEM((ts // HGRN_GR, HGRN_W, HGRN_GR), F32),
                        pltpu.VMEM((nchunk, CHUNK, HGRN_W), BF16),
                        pltpu.VMEM((ts // HGRN_GR, HGRN_GR, HGRN_W), BF16),
                        pltpu.VMEM((npair, ts // HGRN_GR, HGRN_PW, HGRN_GR), BF16),
                        pltpu.VMEM((nchunk, HGRN_WR, HGRN_W), BF16), pltpu.VMEM((nchunk, HGRN_WR, HGRN_W), F32)],
        compiler_params=_cparams(("parallel", "arbitrary")),
        name="hgrn2",
    )(hproj, lb_logits, nw_tiled)


def _s5_param_kernel(prow_ref, bt_ref, btile_ref, ctile_ref, ks_ref, gs_ref, hs_ref, d_ref):
    g = pl.program_id(0)
    are, aim = prow_ref[0:1, :], prow_ref[1:2, :]
    dt = jnp.exp(prow_ref[2:3, :])
    lr, li = dt * are, dt * aim
    mag = jnp.exp(lr)
    abr, abi = mag * jnp.cos(li), mag * jnp.sin(li)
    den = are * are + aim * aim
    nr, ni = abr - 1.0, abi
    zr = (nr * are + ni * aim) / den
    zi = (ni * are - nr * aim) / den

    cr, ci = ctile_ref[0], ctile_ref[1]
    wide = S5_L * S5_C

    npow = 2 * S5_L
    nn = lax.broadcasted_iota(jnp.int32, (npow, S5_P), 0).astype(F32)
    mgn = jnp.exp(nn * lr[:, 0:S5_P])
    pw_r, pw_i = mgn * jnp.cos(nn * li[:, 0:S5_P]), mgn * jnp.sin(nn * li[:, 0:S5_P])
    lag = lax.broadcasted_iota(jnp.int32, (npow, wide), 1) // S5_C
    nrow = lax.broadcasted_iota(jnp.int32, (npow, wide), 0)
    tn = (((0,), (0,)), ((), ()))

    def on_lags(a, first):
        sel = (nrow == lag + first).astype(BF16)
        hi = a.astype(BF16)
        lo = (a - hi.astype(F32)).astype(BF16)
        return (lax.dot_general(hi, sel, tn, preferred_element_type=F32)
                + lax.dot_general(lo, sel, tn, preferred_element_type=F32))

    er, ei = on_lags(pw_r, 0), on_lags(pw_i, 0)
    w1 = er * cr - ei * ci
    w2 = -(ei * cr + er * ci)
    zr1, zi1 = zr[:, 0:S5_P], zi[:, 0:S5_P]
    bbr = zr1 * bt_ref[0] - zi1 * bt_ref[1]
    bbi = zr1 * bt_ref[1] + zi1 * bt_ref[0]
    r0 = _dot_split(bbr, w1) + _dot_split(bbi, w2)
    mine = lax.broadcasted_iota(jnp.int32, (S5_C, wide), 1) // S5_C == g
    for j in range(S5_L):
        shift = (g * S5_C + (j + 1) * S5_C) % wide
        ks_ref[j] = jnp.where(mine, pltpu.roll(r0, shift, axis=1), 0.0).astype(ks_ref.dtype)

    e = (S5_L - 1 - lax.broadcasted_iota(jnp.int32, (S5_L, S5_ST), 0)).astype(F32)
    mg = jnp.exp(e * lr)
    rep = (lax.broadcasted_iota(jnp.int32, (wide, S5_L), 0) // S5_C
           == lax.broadcasted_iota(jnp.int32, (wide, S5_L), 1)).astype(BF16)

    def on_rows(a):
        hi = a.astype(BF16)
        return _dot(rep, hi) + _dot(rep, (a - hi.astype(F32)).astype(BF16))

    fr, fi = on_rows(mg * jnp.cos(e * li)), on_rows(mg * jnp.sin(e * li))
    btr = zr * btile_ref[0] - zi * btile_ref[1]
    bti = zr * btile_ref[1] + zi * btile_ref[0]
    re_half = lax.broadcasted_iota(jnp.int32, (wide, S5_ST), 1) < S5_P
    g128 = jnp.where(re_half, fr * btr - fi * bti, fr * bti + fi * btr)
    tile = lax.broadcasted_iota(jnp.int32, (wide, S5_SW), 1) // S5_ST
    gfull = jnp.where(tile == g, jnp.concatenate([g128] * S5_G, axis=1), 0.0)
    gs_ref[...] = gfull.reshape(S5_L, S5_C, S5_SW).astype(gs_ref.dtype)

    pr, pi = on_lags(pw_r, 1), on_lags(pw_i, 1)
    hcat = jnp.concatenate([cr * pr - ci * pi, -(cr * pi + ci * pr)], axis=0)
    mine_h = lax.broadcasted_iota(jnp.int32, (S5_ST, wide), 1) // S5_C == g
    for t in range(S5_L):
        shift = (g * S5_C + wide - t * S5_C) % wide
        hs_ref[t] = jnp.where(mine_h, pltpu.roll(hcat, shift, axis=1), 0.0).astype(hs_ref.dtype)

    mg = jnp.exp(float(S5_L) * lr)
    d_ref[0:1, :] = mg * jnp.cos(float(S5_L) * li)
    d_ref[1:2, :] = mg * jnp.sin(float(S5_L) * li)


def _s5_params(prow, bt, btile, ctile):
    g = prow.shape[0]
    wide = S5_L * S5_C
    m3 = lambda i: (i, 0, 0)
    m4 = lambda i: (i, 0, 0, 0)
    g4 = lambda i: (0, i, 0, 0)
    return pl.pallas_call(
        _s5_param_kernel,
        grid=(g,),
        in_specs=[pl.BlockSpec((None, 3, S5_ST), m3),
                  pl.BlockSpec((None, 2, S5_C, S5_P), m4), pl.BlockSpec((None, 2, wide, S5_ST), m4),
                  pl.BlockSpec((None, 2, S5_P, wide), m4)],
        out_specs=[pl.BlockSpec((S5_L, None, S5_C, wide), g4), pl.BlockSpec((S5_L, None, S5_C, S5_SW), g4),
                   pl.BlockSpec((S5_L, None, S5_ST, wide), g4), pl.BlockSpec((None, 2, S5_ST), m3)],
        out_shape=[jax.ShapeDtypeStruct((S5_L, g, S5_C, wide), BF16), jax.ShapeDtypeStruct((S5_L, g, S5_C, S5_SW), BF16),
                   jax.ShapeDtypeStruct((S5_L, g, S5_ST, wide), BF16), jax.ShapeDtypeStruct((g, 2, S5_ST), F32)],
        compiler_params=_cparams(("parallel",)),
        name="s5_params",
    )(prow, bt, btile, ctile)


def _s5_state_kernel(ucat_ref, gs_ref, d_ref, xs_ref, *, nchunk):
    width = xs_ref.shape[1]
    x = _dot(ucat_ref[...], gs_ref[...])
    sgn = jnp.where(lax.broadcasted_iota(jnp.int32, (1, width), 1) % S5_ST < S5_P, -1.0, 1.0)
    rn = lax.broadcasted_iota(jnp.int32, (nchunk, width), 0)
    pr, pi = d_ref[0:1, :], d_ref[1:2, :]

    def swap(a):
        return jnp.concatenate([pltpu.roll(a[:, k * S5_ST:(k + 1) * S5_ST], S5_P, axis=1)
                                for k in range(width // S5_ST)], axis=1)

    sh = 1
    while sh < nchunk:
        sx = jnp.where(rn >= sh, pltpu.roll(x, sh, axis=0), 0.0)
        x = x + pr * sx + (pi * sgn) * swap(sx)
        pr, pi = pr * pr - pi * pi, 2.0 * pr * pi
        sh *= 2
    xs_ref[...] = jnp.where(rn >= 1, pltpu.roll(x, 1, axis=0), 0.0).astype(xs_ref.dtype)


def _s5_state(ucat, gs, d, batch, seq):
    nchunk = seq // S5_L
    ntile = S5_SW // S5_W
    return pl.pallas_call(
        functools.partial(_s5_state_kernel, nchunk=nchunk),
        grid=(batch, ntile),
        in_specs=[pl.BlockSpec((nchunk, S5_CW), lambda b, j: (b, 0)),
                  pl.BlockSpec((S5_CW, S5_W), lambda b, j: (0, j)),
                  pl.BlockSpec((2, S5_W), lambda b, j: (0, j))],
        out_specs=pl.BlockSpec((nchunk, S5_W), lambda b, j: (b, j)),
        out_shape=jax.ShapeDtypeStruct((batch * nchunk, S5_SW), BF16),
        compiler_params=_cparams(("parallel", "arbitrary")),
        name="s5_state",
    )(ucat, gs, d)


def _s5_out_kernel(ucat_s, u0_ref, u1_ref, g0_ref, g1_ref, xs_ref, ks_ref, hs_ref, dsk_ref, w_ref, b_ref,
                   o0_ref, o1_ref, *, nb):
    step = pl.program_id(1)

    def response(tt, k):
        return (_dot(ucat_s[:, 0:(tt + 1) * S5_W], ks_ref[(S5_L - 1 - tt) * S5_W:, :])
                + _dot(xs_ref[...], hs_ref[k]))

    def finish(tt, y):
        rows = pl.ds(tt, nb, stride=S5_L)
        cat = lambda a, b: jnp.concatenate([a[rows, :], b[rows, :]], axis=1)
        y = y + dsk_ref[...] * cat(u0_ref, u1_ref)
        y = y * (0.5 * (1.0 + jnp.tanh(math.sqrt(2.0 / math.pi) * (y + 0.044715 * (y * y * y)))))
        z = _dot(y.astype(BF16), w_ref[...]) + b_ref[...]
        out = y * _sigmoid(z) * _silu(cat(g0_ref, g1_ref))
        o0_ref[rows, :] = out[:, 0:S5_HW]
        o1_ref[rows, :] = out[:, S5_HW:]

    for sg in range(S5_L // S5_TG):
        @pl.when(step == sg)
        def _():
            ys = [response(sg * S5_TG + k, k) for k in range(S5_TG)]
            for k in range(S5_TG):
                finish(sg * S5_TG + k, ys[k])


def _s5_out(ucat, sproj, xs, ks, hs, d_skip, w_bf, b, rb):
    t = sproj.shape[0]
    nb = rb // S5_L
    cst = lambda r, i: (0, 0)
    return pl.pallas_call(
        functools.partial(_s5_out_kernel, nb=nb),
        grid=(t // rb, S5_L // S5_TG),
        in_specs=[pl.BlockSpec((nb, S5_CW), lambda r, i: (r, 0)),
                  pl.BlockSpec((rb, S5_HW), lambda r, i: (r, 0)), pl.BlockSpec((rb, S5_HW), lambda r, i: (r, 1)),
                  pl.BlockSpec((rb, S5_HW), lambda r, i: (r, 2)), pl.BlockSpec((rb, S5_HW), lambda r, i: (r, 3)),
                  pl.BlockSpec((nb, S5_SW), lambda r, i: (r, 0)),
                  pl.BlockSpec((S5_CW, S5_W), cst),
                  pl.BlockSpec((S5_TG, S5_SW, S5_W), lambda r, i: (i, 0, 0)),
                  pl.BlockSpec((1, S5_W), cst), pl.BlockSpec((S5_W, S5_W), cst), pl.BlockSpec((1, S5_W), cst)],
        out_specs=[pl.BlockSpec((rb, S5_HW), lambda r, i: (r, 0))] * 2,
        out_shape=[jax.ShapeDtypeStruct((t, S5_HW), F32)] * 2,
        compiler_params=_cparams(("parallel", "arbitrary")),
        name="s5_out",
    )(ucat, sproj, sproj, sproj, sproj, xs, ks, hs, d_skip, w_bf, b)


def _s5(ucat, sproj, a_re, a_im, b_re, b_im, c_re, c_im, log_dt, d_skip, glu_w, glu_b, batch, seq, rb):
    ldt = jnp.broadcast_to(log_dt[:, None], a_re.shape)
    twice = lambda a: jnp.concatenate([a, a], axis=-1)
    prow = twice(jnp.stack([a_re, a_im, ldt], axis=1))
    bt = jnp.stack([b_re, b_im], axis=1).transpose(0, 1, 3, 2)
    btile = twice(jnp.tile(bt, (1, 1, S5_L, 1)))
    ctile = jnp.tile(jnp.stack([c_re, c_im], axis=1).transpose(0, 1, 3, 2), (1, 1, 1, S5_L))
    ks, gs, hs, d = _s5_params(prow, bt, btile, ctile)
    ks = ks.reshape(S5_CW, S5_W)
    gs = gs.reshape(S5_CW, S5_SW)
    hs = hs.reshape(S5_L, S5_SW, S5_W)
    d = d.transpose(1, 0, 2).reshape(2, S5_SW)
    xs = _s5_state(ucat, gs, d, batch, seq)
    return _s5_out(ucat, sproj, xs, ks, hs, d_skip[None, :], glu_w.astype(BF16), glu_b[None, :], rb)


def _pick(n, pref):
    b = min(n, pref)
    while n % b:
        b //= 2
    return b


def kernel(x, norm_w, w_in, w_out, hgrn_lb_logits, hgrn_norm_w, s5_a_re, s5_a_im, s5_b_re, s5_b_im, s5_c_re, s5_c_im, s5_log_dt, s5_d, s5_glu_w, s5_glu_b, diff_lq1, diff_lk1, diff_lq2, diff_lk2, diff_subln_w, final_norm_w):
    batch, seq, _ = x.shape
    depth = norm_w.shape[0]
    t = batch * seq
    tm = _pick(seq, 512)
    tq = _pick(seq, ATT_TQ)
    ts = _pick(seq, 1024)
    rope = _rope_tables(seq)
    h_res = x.astype(F32).reshape(t, D_MODEL)
    nw = lambda l: norm_w[l][None, :].astype(F32)
    hp, sp, uc, aq, ak, av, ag = _inproj(h_res, nw(0), w_in[0].astype(BF16), rope, seq, tm)
    for l in range(depth):
        mix_h = _hgrn(hp, hgrn_lb_logits.astype(F32), jnp.tile(hgrn_norm_w[l].astype(F32), HGRN_W // HGRN_HD)[None, :],
                      l, batch, seq, ts)
        mix_s = _s5(uc, sp, s5_a_re[l], s5_a_im[l], s5_b_re[l], s5_b_im[l], s5_c_re[l], s5_c_im[l], s5_log_dt[l],
                    s5_d[l], s5_glu_w[l], s5_glu_b[l], batch, seq, _pick(seq, 4096))
        lam_init = 0.8 - 0.6 * math.exp(-0.3 * l)
        mix_a = _attention(aq, ak, av, ag, diff_lq1[l][None, :], diff_lk1[l][None, :], diff_lq2[l][None, :],
                           diff_lk2[l][None, :], diff_subln_w[l][None, :], lam_init, batch, seq, tq)
        mixes = (mix_h, mix_s[0], mix_s[1], mix_a, w_out[l].astype(BF16))
        if l + 1 < depth:
            h_res, hp, sp, uc, aq, ak, av, ag = _outin(h_res, *mixes, nw(l + 1), w_in[l + 1].astype(BF16), rope, seq, tm)
        else:
            h_res = _outproj_final(h_res, *mixes, final_norm_w[None, :].astype(F32), _pick(seq, 1024))
    return h_res.reshape(batch, seq, D_MODEL).astype(x.dtype)
```
